```python
import math
import jax
import jax.numpy as jnp
from jax import lax
import numpy as np


D_MODEL = 1024
BATCH = 8
SEQ = 2048
DEPTH = 2

CTX_LEN = 256
GRID_W = 64
Q_BLOCK = 128
ROPE_THETA = 10000.0
EPS = 1e-6
NEG_BIG = -1e30

MIX_W = D_MODEL // 2
A_HEAD_DIM = 64
A_HEADS = MIX_W // (2 * A_HEAD_DIM)
B_HEAD_DIM = 64
B_Q_HEADS = MIX_W // B_HEAD_DIM
B_KV_HEADS = 2
C_HEAD_DIM = 128
C_HEADS = MIX_W // C_HEAD_DIM
C_CONV = 3
C_CHUNK = 128
N_BRANCH = 3
IN_WIDTHS = (MIX_W, MIX_W, MIX_W, MIX_W, B_KV_HEADS * B_HEAD_DIM, B_KV_HEADS * B_HEAD_DIM,
             MIX_W, MIX_W, MIX_W, MIX_W, 4 * C_HEADS, N_BRANCH * D_MODEL)
IN_COLS = sum(IN_WIDTHS)
N_EXPERTS = 16
EXPERT_FF = 2 * D_MODEL
EC_CAPACITY_FACTOR = 2

kernel_name = 'hybrid_diffusion_block'


def rmsnorm(x, w):
    xf = x.astype(jnp.float32)
    y = xf * lax.rsqrt(jnp.mean(xf * xf, axis=-1, keepdims=True) + EPS)
    return (y * w.astype(jnp.float32)).astype(x.dtype)


def modulate(x, shift, scale):
    return x * (1.0 + scale) + shift


def split_columns(p):
    return jnp.split(p, np.cumsum(IN_WIDTHS)[:-1].tolist(), axis=-1)


def diff_lambda_init(layer):
    return 0.8 - 0.6 * math.exp(-0.3 * layer)


def axial_rope_tables(n_tokens, head_dim):
    n_rows = n_tokens // GRID_W
    rows = jnp.repeat(jnp.arange(n_rows, dtype=jnp.float32), GRID_W)
    cols = jnp.tile(jnp.arange(GRID_W, dtype=jnp.float32), n_rows)
    n_freq = head_dim // 4
    inv_freq = ROPE_THETA ** (-jnp.arange(n_freq, dtype=jnp.float32) / n_freq)
    ang = jnp.concatenate([rows[:, None] * inv_freq, cols[:, None] * inv_freq], axis=-1)
    return jnp.cos(ang), jnp.sin(ang)


def apply_rope(x, cos, sin):
    shape = (1, x.shape[1]) + (1,) * (x.ndim - 3) + (x.shape[-1] // 2,)
    c = cos.reshape(shape).astype(x.dtype)
    s = sin.reshape(shape).astype(x.dtype)
    x1, x2 = jnp.split(x, 2, axis=-1)
    return jnp.concatenate([x1 * c - x2 * s, x1 * s + x2 * c], axis=-1)


def sweep_query_blocks(fn, q):
    B, L = q.shape[:2]
    nb = L // Q_BLOCK
    qb = jnp.moveaxis(q.reshape((B, nb, Q_BLOCK) + q.shape[2:]), 1, 0)
    out = lax.map(fn, qb)
    return jnp.moveaxis(out, 0, 1).reshape((B, L) + out.shape[3:])


def diff_attention(q, k, v, lam, subln_w, lam_init):
    s = jnp.einsum('bqhjd,bkhjd->bhjqk', q, k).astype(jnp.float32) * (A_HEAD_DIM ** -0.5)
    p = jax.nn.softmax(s, axis=-1)
    p = p[:, :, 0] - lam * p[:, :, 1]
    o = jnp.einsum('bhqk,bkhe->bqhe', p.astype(v.dtype), v)
    o = rmsnorm(o, subln_w) * (1.0 - lam_init)
    return o.reshape(o.shape[:2] + (-1,))


def gqa_attention(q, k, v):
    B, Lq = q.shape[:2]
    qg = q.reshape(B, Lq, B_KV_HEADS, B_Q_HEADS // B_KV_HEADS, B_HEAD_DIM)
    s = jnp.einsum('bqhgd,bkhd->bhgqk', qg, k).astype(jnp.float32) * (B_HEAD_DIM ** -0.5)
    p = jax.nn.softmax(s, axis=-1)
    o = jnp.einsum('bhgqk,bkhd->bqhgd', p.astype(v.dtype), v)
    return o.reshape(B, Lq, B_Q_HEADS * B_HEAD_DIM)


def centred_depthwise_conv(x, w):
    K = w.shape[0]
    pad = K // 2
    L = x.shape[1]
    xp = jnp.pad(x, ((0, 0), (pad, pad), (0, 0)))
    return sum(xp[:, j:j + L] * w[j] for j in range(K))


def mlstm_chunkwise(q, k, v, ig, lf, state):
    B, L, H, _ = q.shape
    nc = L // C_CHUNK

    def chunks(a):
        return jnp.moveaxis(a.reshape((B, nc, C_CHUNK) + a.shape[2:]), 1, 0)

    tril = jnp.tril(jnp.ones((C_CHUNK, C_CHUNK), dtype=bool))

    def step(carry, xs):
        C, n, m = carry
        qc, kc, vc, ic, fc = xs
        b = jnp.cumsum(fc, axis=1).transpose(0, 2, 1)
        i = ic.transpose(0, 2, 1)
        log_d = jnp.where(tril, b[..., :, None] - b[..., None, :] + i[..., None, :], NEG_BIG)
        log_inter = b + m[..., None]
        m_t = jnp.maximum(log_inter, log_d.max(axis=-1))
        d = jnp.exp(log_d - m_t[..., None])
        w_inter = jnp.exp(log_inter - m_t)
        s = jnp.einsum('bthd,bshd->bhts', qc, kc).astype(jnp.float32) * d
        num = (jnp.einsum('bhts,bshv->bhtv', s, vc)
               + w_inter[..., None] * jnp.einsum('bhvd,bthd->bhtv', C, qc))
        den = s.sum(axis=-1) + w_inter * jnp.einsum('bhd,bthd->bht', n, qc)
        h = num / jnp.maximum(jnp.abs(den), jnp.exp(-m_t))[..., None]
        b_last = b[..., -1]
        log_w = b_last[..., None] - b + i
        m_new = jnp.maximum(b_last + m, log_w.max(axis=-1))
        w = jnp.exp(log_w - m_new[..., None])
        decay = jnp.exp(b_last + m - m_new)
        C_new = decay[..., None, None] * C + jnp.einsum('bhs,bshv,bshd->bhvd', w, vc, kc)
        n_new = decay[..., None] * n + jnp.einsum('bhs,bshd->bhd', w, kc)
        return (C_new, n_new, m_new), h.transpose(0, 2, 1, 3)

    state, hs = lax.scan(step, state, (chunks(q), chunks(k), chunks(v), chunks(ig), chunks(lf)))
    return state, jnp.moveaxis(hs, 0, 1).reshape(B, L, H, -1)


def mlstm_inputs(q, k, v, g, conv_w, gate_b):
    qk = jax.nn.silu(centred_depthwise_conv(jnp.concatenate([q, k], axis=-1), conv_w))
    q, k = jnp.split(qk, 2, axis=-1)
    heads = lambda a: a.reshape(a.shape[:2] + (C_HEADS, C_HEAD_DIM))
    g = g.astype(jnp.float32) + gate_b.astype(jnp.float32)
    i_f, f_f, i_b, f_b = jnp.split(g, 4, axis=-1)
    qkv = (heads(q), heads(k) * (C_HEAD_DIM ** -0.5), heads(v))
    gates = ((i_f, jax.nn.log_sigmoid(f_f)), (i_b, jax.nn.log_sigmoid(f_b)))
    return qkv, gates


def mlstm_bidirectional(ctx_in, lat_in):
    (qc, kc, vc), gates_c = ctx_in
    (ql, kl, vl), gates_l = lat_in
    B = ql.shape[0]
    outs_c, outs_l = [], []
    for direction in range(2):
        flip = (lambda a: a[:, ::-1]) if direction == 1 else (lambda a: a)
        state0 = (jnp.zeros((B, C_HEADS, C_HEAD_DIM, C_HEAD_DIM), jnp.float32),
                  jnp.zeros((B, C_HEADS, C_HEAD_DIM), jnp.float32),
                  jnp.zeros((B, C_HEADS), jnp.float32))
        ic, fc = gates_c[direction]
        il, fl = gates_l[direction]
        st, hc = mlstm_chunkwise(flip(qc), flip(kc), flip(vc), flip(ic), flip(fc), state0)
        _, hl = mlstm_chunkwise(flip(ql), flip(kl), flip(vl), flip(il), flip(fl), st)
        outs_c.append(flip(hc))
        outs_l.append(flip(hl))
    return outs_c[0] + outs_c[1], outs_l[0] + outs_l[1]


def mlstm_output(h, o, norm_w):
    h = rmsnorm(h.astype(o.dtype), norm_w.reshape(C_HEADS, C_HEAD_DIM))
    return jax.nn.sigmoid(o) * h.reshape(o.shape)


def hybrid_mixer(h_c, h_l, cos, sin, lam_init, w_in, conv_w, gate_b, c_norm_w, lam_vecs, subln_w,
                 q_norm_w, k_norm_w, w_br_a, w_br_b, w_br_c, w_out, with_ctx):
    p_c = split_columns(h_c @ w_in)
    p_l = split_columns(h_l @ w_in)

    lv = lam_vecs.astype(jnp.float32)
    lam = jnp.exp(jnp.sum(lv[0] * lv[1])) - jnp.exp(jnp.sum(lv[2] * lv[3])) + lam_init
    heads_a = lambda a: a.reshape(a.shape[:2] + (A_HEADS, 2, A_HEAD_DIM))
    vheads_a = lambda a: a.reshape(a.shape[:2] + (A_HEADS, 2 * A_HEAD_DIM))
    qa_c, ka_c, va_c = heads_a(p_c[0]), heads_a(p_c[1]), vheads_a(p_c[2])
    qa_l = apply_rope(heads_a(p_l[0]), cos, sin)
    ka_all = jnp.concatenate([ka_c, apply_rope(heads_a(p_l[1]), cos, sin)], axis=1)
    va_all = jnp.concatenate([va_c, vheads_a(p_l[2])], axis=1)
    oa_l = sweep_query_blocks(lambda qblk: diff_attention(qblk, ka_all, va_all, lam, subln_w, lam_init), qa_l)

    heads_b = lambda a, nh: a.reshape(a.shape[:2] + (nh, B_HEAD_DIM))
    qb_c = rmsnorm(heads_b(p_c[3], B_Q_HEADS), q_norm_w)
    kb_c = rmsnorm(heads_b(p_c[4], B_KV_HEADS), k_norm_w)
    vb_c = heads_b(p_c[5], B_KV_HEADS)
    qb_l = apply_rope(rmsnorm(heads_b(p_l[3], B_Q_HEADS), q_norm_w), cos, sin)
    kb_all = jnp.concatenate([kb_c, apply_rope(rmsnorm(heads_b(p_l[4], B_KV_HEADS), k_norm_w), cos, sin)], axis=1)
    vb_all = jnp.concatenate([vb_c, heads_b(p_l[5], B_KV_HEADS)], axis=1)
    ob_l = sweep_query_blocks(lambda qblk: gqa_attention(qblk, kb_all, vb_all), qb_l)

    ctx_in = mlstm_inputs(p_c[6], p_c[7], p_c[8], p_c[10], conv_w, gate_b)
    lat_in = mlstm_inputs(p_l[6], p_l[7], p_l[8], p_l[10], conv_w, gate_b)
    hc_sum, hl_sum = mlstm_bidirectional(ctx_in, lat_in)
    oc_l = mlstm_output(hl_sum, p_l[9], c_norm_w)

    def merge(oa, ob, oc, g):
        gates = jax.nn.sigmoid(g).reshape(g.shape[:2] + (N_BRANCH, D_MODEL))
        merged = (gates[..., 0, :] * (oa @ w_br_a) + gates[..., 1, :] * (ob @ w_br_b)
                  + gates[..., 2, :] * (oc @ w_br_c))
        return merged @ w_out

    y_l = merge(oa_l, ob_l, oc_l, p_l[11])
    y_c = None
    if with_ctx:
        oa_c = diff_attention(qa_c, ka_c, va_c, lam, subln_w, lam_init)
        ob_c = gqa_attention(qb_c, kb_c, vb_c)
        oc_c = mlstm_output(hc_sum, p_c[9], c_norm_w)
        y_c = merge(oa_c, ob_c, oc_c, p_c[11])
    return y_c, y_l


def expert_choice_ffn(h, w_router, w_gate, w_up, w_down):
    B, N, D = h.shape
    cap = EC_CAPACITY_FACTOR * N // N_EXPERTS
    logits = jnp.einsum('bnd,de->bne', h, w_router).astype(jnp.float32)
    aff = jax.nn.softmax(logits, axis=-1).transpose(0, 2, 1)
    g, idx = lax.top_k(aff, cap)
    xe = jax.vmap(lambda hb, ib: hb[ib])(h, idx)
    a = jnp.einsum('becd,edf->becf', xe, w_gate)
    u = jnp.einsum('becd,edf->becf', xe, w_up)
    ye = jnp.einsum('becf,efd->becd', jax.nn.silu(a) * u, w_down) * g[..., None].astype(h.dtype)
    return jax.vmap(lambda yb, ib: jnp.zeros((N, D), yb.dtype).at[ib.reshape(-1)].add(yb.reshape(-1, D)))(ye, idx)


def setup_inputs(seed: int = 0) -> dict:
    key = jax.random.key(seed)
    ks = jax.random.split(key, 32)
    D = D_MODEL
    nrm = lambda k, shape, scale: jax.random.normal(k, shape, jnp.float32) * scale
    gate_b = jnp.concatenate([
        nrm(ks[10], (DEPTH, C_HEADS), 0.1),
        jax.random.uniform(ks[11], (DEPTH, C_HEADS), jnp.float32, 3.0, 6.0),
        nrm(ks[12], (DEPTH, C_HEADS), 0.1),
        jax.random.uniform(ks[13], (DEPTH, C_HEADS), jnp.float32, 3.0, 6.0)], axis=-1)
    return {
        'x': nrm(ks[0], (BATCH, SEQ, D), 1.0),
        'c': nrm(ks[1], (BATCH, D), 1.0),
        'ctx': nrm(ks[2], (BATCH, CTX_LEN, D), 1.0),
        'c_ctx': nrm(ks[3], (D,), 1.0),
        'w_ada': nrm(ks[4], (DEPTH, D, 6 * D), 0.5 * D ** -0.5),
        'b_ada': nrm(ks[5], (DEPTH, 6 * D), 0.02),
        'norm1_w': 1.0 + nrm(ks[6], (DEPTH, D), 0.05),
        'norm2_w': 1.0 + nrm(ks[7], (DEPTH, D), 0.05),
        'w_in': nrm(ks[8], (DEPTH, D, IN_COLS), D ** -0.5),
        'mlstm_conv_w': nrm(ks[9], (DEPTH, C_CONV, 2 * MIX_W), C_CONV ** -0.5),
        'mlstm_gate_b': gate_b,
        'mlstm_norm_w': 1.0 + nrm(ks[14], (DEPTH, MIX_W), 0.05),
        'diff_lambda': nrm(ks[15], (DEPTH, 4, A_HEAD_DIM), 0.1),
        'diff_subln_w': 1.0 + nrm(ks[16], (DEPTH, 2 * A_HEAD_DIM), 0.05),
        'gqa_qnorm_w': 1.0 + nrm(ks[17], (DEPTH, B_HEAD_DIM), 0.05),
        'gqa_knorm_w': 1.0 + nrm(ks[18], (DEPTH, B_HEAD_DIM), 0.05),
        'w_branch_a': nrm(ks[19], (DEPTH, MIX_W, D), MIX_W ** -0.5),
        'w_branch_b': nrm(ks[20], (DEPTH, MIX_W, D), MIX_W ** -0.5),
        'w_branch_c': nrm(ks[21], (DEPTH, MIX_W, D), MIX_W ** -0.5),
        'w_out': nrm(ks[22], (DEPTH, D, D), D ** -0.5),
        'w_router': nrm(ks[23], (DEPTH, D, N_EXPERTS), D ** -0.5),
        'w_exp_gate': nrm(ks[24], (DEPTH, N_EXPERTS, D, EXPERT_FF), D ** -0.5),
        'w_exp_up': nrm(ks[25], (DEPTH, N_EXPERTS, D, EXPERT_FF), D ** -0.5),
        'w_exp_down': nrm(ks[26], (DEPTH, N_EXPERTS, EXPERT_FF, D), EXPERT_FF ** -0.5),
        'final_norm_w': 1.0 + nrm(ks[27], (D,), 0.05),
    }


def reference(x, c, ctx, c_ctx, w_ada, b_ada, norm1_w, norm2_w, w_in, mlstm_conv_w, mlstm_gate_b,
              mlstm_norm_w, diff_lambda, diff_subln_w, gqa_qnorm_w, gqa_knorm_w, w_branch_a, w_branch_b,
              w_branch_c, w_out, w_router, w_exp_gate, w_exp_up, w_exp_down, final_norm_w):
    cos, sin = axial_rope_tables(x.shape[1], A_HEAD_DIM)
    for layer in range(DEPTH):
        with_ctx = layer < DEPTH - 1
        mod_l = (jax.nn.silu(c) @ w_ada[layer] + b_ada[layer])[:, None, :]
        mod_c = (jax.nn.silu(c_ctx) @ w_ada[layer] + b_ada[layer])[None, None, :]
        sh1_l, sc1_l, g1_l, sh2_l, sc2_l, g2_l = jnp.split(mod_l, 6, axis=-1)
        sh1_c, sc1_c, g1_c, sh2_c, sc2_c, g2_c = jnp.split(mod_c, 6, axis=-1)

        h_l = modulate(rmsnorm(x, norm1_w[layer]), sh1_l, sc1_l)
        h_c = modulate(rmsnorm(ctx, norm1_w[layer]), sh1_c, sc1_c)
        y_c, y_l = hybrid_mixer(h_c, h_l, cos, sin, diff_lambda_init(layer), w_in[layer],
                                mlstm_conv_w[layer], mlstm_gate_b[layer], mlstm_norm_w[layer],
                                diff_lambda[layer], diff_subln_w[layer], gqa_qnorm_w[layer],
                                gqa_knorm_w[layer], w_branch_a[layer], w_branch_b[layer],
                                w_branch_c[layer], w_out[layer], with_ctx)
        x = x + g1_l * y_l
        h_l = modulate(rmsnorm(x, norm2_w[layer]), sh2_l, sc2_l)
        x = x + g2_l * expert_choice_ffn(h_l, w_router[layer], w_exp_gate[layer], w_exp_up[layer], w_exp_down[layer])
        if with_ctx:
            ctx = ctx + g1_c * y_c
            h_c = modulate(rmsnorm(ctx, norm2_w[layer]), sh2_c, sc2_c)
            ctx = ctx + g2_c * expert_choice_ffn(h_c, w_router[layer], w_exp_gate[layer], w_exp_up[layer], w_exp_down[layer])
    return rmsnorm(x, final_norm_w)
```

```python
import functools
import math

import jax
import jax.numpy as jnp
from jax import lax
from jax.experimental import pallas as pl
from jax.experimental.pallas import tpu as pltpu

F32 = jnp.float32
BF16 = jnp.bfloat16

D_MODEL = 1024
DEPTH = 2
GRID_W = 64
ROPE_THETA = 10000.0
EPS = 1e-6
NEG_BIG = -1e30
MIX_W = D_MODEL // 2
A_HEAD_DIM = 64
A_HEADS = MIX_W // (2 * A_HEAD_DIM)
B_HEAD_DIM = 64
B_Q_HEADS = MIX_W // B_HEAD_DIM
B_KV_HEADS = 2
B_GROUP = B_Q_HEADS // B_KV_HEADS
C_HEAD_DIM = 128
C_HEADS = MIX_W // C_HEAD_DIM
C_CONV = 3
C_CHUNK = 128
N_BRANCH = 3
N_EXPERTS = 16
EXPERT_FF = 2 * D_MODEL
EC_CAPACITY_FACTOR = 2

LANES = 128
KV_B = B_KV_HEADS * B_HEAD_DIM
N_GATES = 4 * C_HEADS
MAIN_COLS = 8 * MIX_W + 2 * KV_B
GATE_COL0 = MAIN_COLS
MERGE_COL0 = MAIN_COLS + N_GATES
TM = 256
TQ = 256
FF_TILE = 512
FFN_ROWS = 512
VMEM_LIMIT = 56 * 1024 * 1024


def _cparams(n_axes, vmem=VMEM_LIMIT):
    return pltpu.CompilerParams(dimension_semantics=("arbitrary",) * n_axes, vmem_limit_bytes=vmem)


def _dot(a, b):
    return jnp.dot(a, b, preferred_element_type=F32)


def _dot_nt(a, b):
    return lax.dot_general(a, b, (((1,), (1,)), ((), ())), preferred_element_type=F32)


def _split3(x):
    a = x.astype(BF16)
    r = x - a.astype(F32)
    b = r.astype(BF16)
    c = (r - b.astype(F32)).astype(BF16)
    return a, b, c


def _sigmoid(x):
    return 1.0 / (1.0 + jnp.exp(-x))


def _silu(x):
    return x * _sigmoid(x)


def _log_sigmoid(x):
    return jnp.minimum(x, 0.0) - jnp.log(1.0 + jnp.exp(-jnp.abs(x)))


def _norm_mod(x, nw, shift, scale):
    ms = jnp.mean(x * x, axis=-1, keepdims=True)
    return (x * lax.rsqrt(ms + EPS) * nw) * (1.0 + scale) + shift


def _ada_kernel(c_ref, w_ref, b_ref, o_ref):
    s = _silu(c_ref[...])
    s1, s2, _ = _split3(s)
    w = w_ref[0]
    w1, w2, _ = _split3(w)
    o_ref[0] = _dot(s1, w1) + _dot(s1, w2) + _dot(s2, w1) + b_ref[0]


def _ada(cvec, w_ada, b_ada):
    depth, d, n = w_ada.shape
    tn = 1536
    return pl.pallas_call(
        _ada_kernel,
        grid=(depth, n // tn),
        in_specs=[pl.BlockSpec((16, d), lambda l, j: (0, 0)),
                  pl.BlockSpec((1, d, tn), lambda l, j: (l, 0, j)),
                  pl.BlockSpec((1, 1, tn), lambda l, j: (l, 0, j))],
        out_specs=pl.BlockSpec((1, 16, tn), lambda l, j: (l, 0, j)),
        out_shape=jax.ShapeDtypeStruct((depth, 16, n), F32),
        compiler_params=_cparams(2),
        name="ada_mod",
    )(cvec, w_ada, b_ada.reshape(depth, 1, n))


def _inproj_kernel(x_ref, mod_ref, nw_ref, w_ref, wg_ref, wgt_ref, gb_ref, gbt_ref, cos_ref, sin_ref,
                   qnw_ref, knw_ref, gm_ref,
                   qa_ref, ka_ref, va_ref, qb_ref, kb_ref, vb_ref, qkc_ref, vc_ref, oc_ref, g_ref, gt_ref):
    d = D_MODEL
    x = x_ref[0]
    mod = mod_ref[0]
    h = _norm_mod(x, nw_ref[...], mod[:, 0:d], mod[:, d:2 * d]).astype(BF16)
    tm = x.shape[0]
    cos = cos_ref[...]
    sin = sin_ref[...]
    lane = lax.broadcasted_iota(jnp.int32, (tm, LANES), 1)
    first_half = (lane % A_HEAD_DIM) < (A_HEAD_DIM // 2)
    gm = gm_ref[...]

    def rope(p):
        partner = jnp.where(first_half, pltpu.roll(p, LANES - 32, 1), pltpu.roll(p, 32, 1))
        return p * cos + partner * sin

    def head_norm(p, w):
        sq = p * p
        hi = sq.astype(BF16)
        lo = (sq - hi.astype(F32)).astype(BF16)
        ms = _dot(hi, gm) + _dot(lo, gm)
        return p * lax.rsqrt(ms + EPS) * w

    def proj(c0):
        return _dot(h, w_ref[:, c0:c0 + 2 * LANES])

    def halves(p):
        return p[:, :LANES], p[:, LANES:]

    for j in range(2):
        for half, p in enumerate(halves(proj(j * 256))):
            c = j * 256 + half * LANES
            qa_ref[0, :, c:c + LANES] = (rope(p) * (A_HEAD_DIM ** -0.5)).astype(BF16)
    for j in range(2):
        for half, p in enumerate(halves(proj(512 + j * 256))):
            c = j * 256 + half * LANES
            ka_ref[0, :, c:c + LANES] = rope(p).astype(BF16)
    for j in range(2):
        va_ref[0, :, j * 256:(j + 1) * 256] = proj(1024 + j * 256).astype(BF16)
    qnw = qnw_ref[...]
    for j in range(2):
        for half, p in enumerate(halves(proj(1536 + j * 256))):
            c = j * 256 + half * LANES
            qb_ref[0, :, c:c + LANES] = (rope(head_norm(p, qnw)) * (B_HEAD_DIM ** -0.5)).astype(BF16)
    pk, pv = halves(proj(2048))
    kb_ref[0] = rope(head_norm(pk, knw_ref[...])).astype(BF16)
    vb_ref[0] = pv.astype(BF16)
    for j in range(4):
        qkc_ref[0, :, j * 256:(j + 1) * 256] = proj(2304 + j * 256)
    for j in range(2):
        vc_ref[0, :, j * 256:(j + 1) * 256] = proj(3328 + j * 256).astype(BF16)
    for j in range(2):
        oc_ref[0, :, j * 256:(j + 1) * 256] = _sigmoid(proj(3840 + j * 256)).astype(BF16)
    g = _dot(h, wg_ref[...]) + gb_ref[...]
    is_f = ((lane % 8) >= 4) & (lane < N_GATES)
    g_ref[0] = jnp.where(is_f, _log_sigmoid(g), g)
    gt = _dot_nt(wgt_ref[...], h) + gbt_ref[...]
    row = lax.broadcasted_iota(jnp.int32, (N_GATES, tm), 0)
    gt_ref[0] = jnp.where((row % 8) >= 4, _log_sigmoid(gt), gt)


def _inproj(xs, mod, nw, w_main, wg, wgt, gb, gbt, cos, sin, qnw, knw, gmat, n_ctx_tiles):
    b, ts, d = xs.shape
    nt = ts // TM
    tok = lambda width: pl.BlockSpec((1, TM, width), lambda i, j: (i, j, 0))
    full = lambda a: pl.BlockSpec(a.shape, lambda i, j: (0,) * a.ndim)
    outs = [(MIX_W, BF16), (MIX_W, BF16), (MIX_W, BF16), (MIX_W, BF16), (KV_B, BF16), (KV_B, BF16),
            (2 * MIX_W, F32), (MIX_W, BF16), (MIX_W, BF16), (LANES, F32)]
    out_shape = [jax.ShapeDtypeStruct((b, ts, w), dt) for w, dt in outs]
    out_specs = [tok(w) for w, _ in outs]
    out_shape.append(jax.ShapeDtypeStruct((b, N_GATES, ts), F32))
    out_specs.append(pl.BlockSpec((1, N_GATES, TM), lambda i, j: (i, 0, j)))
    return pl.pallas_call(
        _inproj_kernel,
        grid=(b, nt),
        in_specs=[tok(d),
                  pl.BlockSpec((1, 1, mod.shape[-1]), lambda i, j: (jnp.where(j < n_ctx_tiles, 8, i), 0, 0)),
                  full(nw), full(w_main), full(wg), full(wgt), full(gb), full(gbt),
                  pl.BlockSpec((TM, LANES), lambda i, j: (j, 0)),
                  pl.BlockSpec((TM, LANES), lambda i, j: (j, 0)),
                  full(qnw), full(knw), full(gmat)],
        out_specs=out_specs,
        out_shape=out_shape,
        compiler_params=_cparams(2),
        name="in_proj",
    )(xs, mod, nw, w_main, wg, wgt, gb, gbt, cos, sin, qnw, knw, gmat)


def _diff_attn_kernel(lam_ref, sub_ref, q_ref, k_ref, v_ref, o_ref, *, lam_init, n_ctx_blocks, q_off, lc):
    qi = pl.program_id(2) + q_off
    lv = lam_ref[...]
    lam = (jnp.exp(jnp.sum(lv[0:1] * lv[1:2], axis=-1, keepdims=True))
           - jnp.exp(jnp.sum(lv[2:3] * lv[3:4], axis=-1, keepdims=True)) + lam_init)

    def body(lk):
        q = q_ref[0]
        k = k_ref[0, :lk, :]
        v = v_ref[0, :lk, :]
        lane = lax.broadcasted_iota(jnp.int32, q.shape, 1)
        zero = jnp.zeros_like(q)
        s0 = _dot_nt(jnp.where(lane < A_HEAD_DIM, q, zero), k)
        s1 = _dot_nt(jnp.where(lane >= A_HEAD_DIM, q, zero), k)
        e0 = jnp.exp(s0 - jnp.max(s0, axis=-1, keepdims=True))
        e1 = jnp.exp(s1 - jnp.max(s1, axis=-1, keepdims=True))
        r0 = 1.0 / jnp.sum(e0, axis=-1, keepdims=True)
        r1 = lam / jnp.sum(e1, axis=-1, keepdims=True)
        p = e0 * r0 - e1 * r1
        o = _dot(p.astype(BF16), v)
        ms = jnp.mean(o * o, axis=-1, keepdims=True)
        o_ref[0] = (o * lax.rsqrt(ms + EPS) * sub_ref[...] * (1.0 - lam_init)).astype(BF16)

    @pl.when(qi < n_ctx_blocks)
    def _():
        body(lc)

    @pl.when(qi >= n_ctx_blocks)
    def _():
        body(k_ref.shape[1])


def _diff_attn(lam_vecs, sub_w, qa, ka, va, lam_init, lc, with_ctx):
    b, ts, _ = qa.shape
    n_ctx_blocks = lc // TQ
    q_off = 0 if with_ctx else n_ctx_blocks
    nq = ts // TQ - q_off
    w = 2 * A_HEAD_DIM
    kern = functools.partial(_diff_attn_kernel, lam_init=lam_init, n_ctx_blocks=n_ctx_blocks, q_off=q_off, lc=lc)
    return pl.pallas_call(
        kern,
        grid=(b, A_HEADS, nq),
        in_specs=[pl.BlockSpec(lam_vecs.shape, lambda i, h, j: (0, 0)),
                  pl.BlockSpec(sub_w.shape, lambda i, h, j: (0, 0)),
                  pl.BlockSpec((1, TQ, w), lambda i, h, j: (i, j + q_off, h)),
                  pl.BlockSpec((1, ts, w), lambda i, h, j: (i, 0, h)),
                  pl.BlockSpec((1, ts, w), lambda i, h, j: (i, 0, h))],
        out_specs=pl.BlockSpec((1, TQ, w), lambda i, h, j: (i, j + q_off, h)),
        out_shape=jax.ShapeDtypeStruct((b, ts, MIX_W), BF16),
        compiler_params=_cparams(3),
        name="diff_attn",
    )(lam_vecs, sub_w, qa, ka, va)


def _gqa_kernel(q_ref, k_ref, v_ref, o_ref, *, n_ctx_blocks, q_off, lc):
    g = pl.program_id(1)
    qi = pl.program_id(2) + q_off

    def body(lk):
        k = k_ref[0, :lk, :]
        v = v_ref[0, :lk, :]
        tq = q_ref.shape[1]
        lane = lax.broadcasted_iota(jnp.int32, (tq, LANES), 1)
        mine = (lane >= g * B_HEAD_DIM) & (lane < (g + 1) * B_HEAD_DIM)
        low = lane < B_HEAD_DIM
        first = g == 0
        for pair in range(B_GROUP // 2):
            x = q_ref[0, :, pair * LANES:(pair + 1) * LANES].astype(F32)
            xr = pltpu.roll(x, B_HEAD_DIM, 1)
            outs = []
            for which in range(2):
                src = jnp.where(first, x, xr) if which == 0 else jnp.where(first, xr, x)
                qh = jnp.where(mine, src, 0.0).astype(BF16)
                s = _dot_nt(qh, k)
                e = jnp.exp(s - jnp.max(s, axis=-1, keepdims=True))
                r = 1.0 / jnp.sum(e, axis=-1, keepdims=True)
                outs.append(_dot(e.astype(BF16), v) * r)
            oa, ob = outs
            oar = pltpu.roll(oa, B_HEAD_DIM, 1)
            obr = pltpu.roll(ob, B_HEAD_DIM, 1)
            left = jnp.where(first, oa, oar)
            right = jnp.where(first, obr, ob)
            o_ref[0, :, pair * LANES:(pair + 1) * LANES] = jnp.where(low, left, right).astype(BF16)

    @pl.when(qi < n_ctx_blocks)
    def _():
        body(lc)

    @pl.when(qi >= n_ctx_blocks)
    def _():
        body(k_ref.shape[1])


def _gqa_attn(qb, kb, vb, lc, with_ctx):
    b, ts, _ = qb.shape
    n_ctx_blocks = lc // TQ
    q_off = 0 if with_ctx else n_ctx_blocks
    nq = ts // TQ - q_off
    w = B_GROUP * B_HEAD_DIM
    kern = functools.partial(_gqa_kernel, n_ctx_blocks=n_ctx_blocks, q_off=q_off, lc=lc)
    return pl.pallas_call(
        kern,
        grid=(b, B_KV_HEADS, nq),
        in_specs=[pl.BlockSpec((1, TQ, w), lambda i, g, j: (i, j + q_off, g)),
                  pl.BlockSpec((1, ts, KV_B), lambda i, g, j: (i, 0, 0)),
                  pl.BlockSpec((1, ts, KV_B), lambda i, g, j: (i, 0, 0))],
        out_specs=pl.BlockSpec((1, TQ, w), lambda i, g, j: (i, j + q_off, g)),
        out_shape=jax.ShapeDtypeStruct((b, ts, MIX_W), BF16),
        compiler_params=_cparams(3),
        name="gqa_attn",
    )(qb, kb, vb)


def _mlstm_kernel(q_ref, k_ref, v_ref, o_ref, g_ref, gt_ref, cwq_ref, cwk_ref, nw_ref, out_ref,
                  q_s, kt_s, bc_s, ac_s, rows_s, hacc_s, st_s, *, lc):
    ts = q_ref.shape[1]
    hp = q_ref.shape[2] // C_HEAD_DIM
    head0 = pl.program_id(1) * hp
    nc = ts // C_CHUNK
    ncc = lc // C_CHUNK
    ch = C_CHUNK

    row = lax.broadcasted_iota(jnp.int32, (ts, LANES), 0)
    prev_ok = (row != 0) & (row != lc)
    next_ok = (row != lc - 1) & (row != ts - 1)

    def conv(x, w):
        xp = jnp.where(prev_ok, pltpu.roll(x, 1, 0), 0.0)
        xn = jnp.where(next_ok, pltpu.roll(x, ts - 1, 0), 0.0)
        return _silu(xp * w[0:1] + x * w[1:2] + xn * w[2:3])

    for j in range(hp):
        cols = slice(j * LANES, (j + 1) * LANES)
        q_s[:, cols] = conv(q_ref[0, :, cols], cwq_ref[0, :, cols]).astype(BF16)
        y = conv(k_ref[0, :, cols], cwk_ref[0, :, cols]) * (C_HEAD_DIM ** -0.5)
        for c in range(nc):
            kt_s[c, cols, :] = y[c * ch:(c + 1) * ch, :].T.astype(BF16)

    ri = lax.broadcasted_iota(jnp.int32, (ch, ch), 0)
    ci = lax.broadcasted_iota(jnp.int32, (ch, ch), 1)
    lower = jnp.where(ci <= ri, 1.0, 0.0).astype(BF16)
    upper = jnp.where(ci >= ri, 1.0, 0.0).astype(BF16)
    lane = ci
    rowi = lax.broadcasted_iota(jnp.int32, (N_GATES, ch), 0)
    for c in range(nc):
        rs = slice(c * ch, (c + 1) * ch)
        g = g_ref[0, rs, :]
        g1, g2, g3 = _split3(g)
        pre = _dot(lower, g1) + _dot(lower, g2) + _dot(lower, g3)
        suf = _dot(upper, g1) + _dot(upper, g2) + _dot(upper, g3)
        gt = gt_ref[0, :, rs]
        t1, t2, t3 = _split3(gt)
        pre_t = _dot(t1, upper) + _dot(t2, upper) + _dot(t3, upper)
        suf_t = _dot(t1, lower) + _dot(t2, lower) + _dot(t3, lower)
        for j in range(hp):
            for direction in range(2):
                idx = j * 2 + direction
                li = head0 + j + 8 * direction
                lf = li + 4
                cum, cum_t = (pre, pre_t) if direction == 0 else (suf, suf_t)
                b_col = jnp.sum(jnp.where(lane == lf, cum, 0.0), axis=-1, keepdims=True)
                i_col = jnp.sum(jnp.where(lane == li, g, 0.0), axis=-1, keepdims=True)
                bc_s[idx, rs, :] = jnp.broadcast_to(b_col, (ch, LANES))
                ac_s[idx, rs, :] = jnp.broadcast_to(i_col - b_col, (ch, LANES))
                b_row = jnp.sum(jnp.where(rowi == lf, cum_t, 0.0), axis=0, keepdims=True)
                i_row = jnp.sum(jnp.where(rowi == li, gt, 0.0), axis=0, keepdims=True)
                rows_s[c, 2 * idx:2 * idx + 1, :] = b_row
                rows_s[c, 2 * idx + 1:2 * idx + 2, :] = i_row - b_row

    hacc_s[...] = jnp.zeros_like(hacc_s)
    st_s[...] = jnp.zeros_like(st_s)
    tri_f = ci <= ri
    tri_b = ci >= ri

    def chain(c, j, direction, m):
        idx = j * 2 + direction
        c0 = pl.multiple_of(c * ch, ch)
        cols = slice(j * LANES, (j + 1) * LANES)
        q = q_s[pl.ds(c0, ch), cols]
        kt = kt_s[c, cols, :]
        v = v_ref[0, pl.ds(c0, ch), cols]
        bc = bc_s[idx, pl.ds(c0, ch), :]
        ac = ac_s[idx, pl.ds(c0, ch), :]
        rows = rows_s[c]
        b_row = rows[2 * idx:2 * idx + 1, :]
        ib_row = rows[2 * idx + 1:2 * idx + 2, :]
        tri = tri_f if direction == 0 else tri_b
        log_d = jnp.where(tri, bc + ib_row, NEG_BIG)
        m_intra = jnp.max(log_d, axis=-1, keepdims=True)
        log_inter = bc + m
        m_t = jnp.maximum(log_inter, m_intra)
        dm = jnp.exp(log_d - m_t)
        w_inter = jnp.exp(log_inter - m_t)
        s = _dot(q, kt) * dm
        st = st_s[idx]
        inter = _dot(q, st.astype(BF16))
        num = _dot(s.astype(BF16), v) + w_inter * inter[:, :LANES]
        den = jnp.sum(s, axis=-1, keepdims=True) + w_inter * inter[:, LANES:LANES + 1]
        hout = num / jnp.maximum(jnp.abs(den), jnp.exp(-m_t))
        hacc_s[pl.ds(c0, ch), cols] = hacc_s[pl.ds(c0, ch), cols] + hout
        total = b_row[:, ch - 1:ch] if direction == 0 else b_row[:, 0:1]
        m_new = jnp.maximum(total + m, jnp.max(total + ib_row, axis=-1, keepdims=True))
        w = jnp.exp(total + ac - m_new)
        decay = jnp.exp(total + m - m_new)
        wv = jnp.concatenate([w * v.astype(F32), jnp.where(lane == 0, w, 0.0)], axis=1).astype(BF16)
        st_s[idx] = decay * st + _dot(kt, wv)
        return m_new

    def step(i, ms):
        c_f = i
        c_b = jnp.where(i < ncc, ncc - 1 - i, nc + ncc - 1 - i)
        out = []
        for j in range(hp):
            out.append(chain(c_f, j, 0, ms[j * 2]))
            out.append(chain(c_b, j, 1, ms[j * 2 + 1]))
        return tuple(out)

    lax.fori_loop(0, nc, step, tuple(jnp.zeros((1, 1), F32) for _ in range(2 * hp)))

    for j in range(hp):
        cols = slice(j * LANES, (j + 1) * LANES)
        x = hacc_s[:, cols]
        ms = jnp.mean(x * x, axis=-1, keepdims=True)
        y = x * lax.rsqrt(ms + EPS) * nw_ref[0, :, cols]
        out_ref[0, :, cols] = (o_ref[0, :, cols].astype(F32) * y).astype(BF16)


def _mlstm(qkc, vc, oc, g, gt, conv_w, norm_w, lc, heads_per_step=2):
    b, ts, _ = vc.shape
    wq = heads_per_step * C_HEAD_DIM
    nhp = C_HEADS // heads_per_step
    nc = ts // C_CHUNK
    cw = conv_w.reshape(C_CONV, 2 * nhp, wq).transpose(1, 0, 2)
    nw = norm_w.reshape(1, nhp, wq).transpose(1, 0, 2)
    kern = functools.partial(_mlstm_kernel, lc=lc)
    tokw = lambda off: pl.BlockSpec((1, ts, wq), lambda i, p: (i, 0, p + off))
    return pl.pallas_call(
        kern,
        grid=(b, nhp),
        in_specs=[tokw(0), tokw(nhp), tokw(0), tokw(0),
                  pl.BlockSpec((1, ts, LANES), lambda i, p: (i, 0, 0)),
                  pl.BlockSpec((1, N_GATES, ts), lambda i, p: (i, 0, 0)),
                  pl.BlockSpec((1, C_CONV, wq), lambda i, p: (p, 0, 0)),
                  pl.BlockSpec((1, C_CONV, wq), lambda i, p: (p + nhp, 0, 0)),
                  pl.BlockSpec((1, 1, wq), lambda i, p: (p, 0, 0))],
        out_specs=tokw(0),
        out_shape=jax.ShapeDtypeStruct((b, ts, MIX_W), BF16),
        scratch_shapes=[pltpu.VMEM((ts, wq), BF16),
                        pltpu.VMEM((nc, wq, C_CHUNK), BF16),
                        pltpu.VMEM((2 * heads_per_step, ts, LANES), F32),
                        pltpu.VMEM((2 * heads_per_step, ts, LANES), F32),
                        pltpu.VMEM((nc, 4 * heads_per_step, C_CHUNK), F32),
                        pltpu.VMEM((ts, wq), F32),
                        pltpu.VMEM((2 * heads_per_step, C_HEAD_DIM, 2 * LANES), F32)],
        compiler_params=_cparams(2),
        name="mlstm",
    )(qkc, qkc, vc, oc, g, gt, cw, cw, nw)


def _merge_kernel(x_ref, mod_ref, n1_ref, n2_ref, oa_ref, ob_ref, oc_ref, wm_ref, wa_ref, wb_ref, wc_ref,
                  wo_ref, wr_ref, xo_ref, h2_ref, aff_ref):
    d = D_MODEL
    x = x_ref[0]
    mod = mod_ref[0]
    h = _norm_mod(x, n1_ref[...], mod[:, 0:d], mod[:, d:2 * d]).astype(BF16)
    merged = (_sigmoid(_dot(h, wm_ref[:, 0:d])) * _dot(oa_ref[0], wa_ref[...])
              + _sigmoid(_dot(h, wm_ref[:, d:2 * d])) * _dot(ob_ref[0], wb_ref[...])
              + _sigmoid(_dot(h, wm_ref[:, 2 * d:3 * d])) * _dot(oc_ref[0], wc_ref[...]))
    y = _dot(merged.astype(BF16), wo_ref[...])
    x1 = x + mod[:, 2 * d:3 * d] * y
    xo_ref[0] = x1
    h2 = _norm_mod(x1, n2_ref[...], mod[:, 3 * d:4 * d], mod[:, 4 * d:5 * d])
    h2b = h2.astype(BF16)
    h2_ref[0] = h2b
    h2l = (h2 - h2b.astype(F32)).astype(BF16)
    wr = wr_ref[...]
    wrh = wr.astype(BF16)
    wrl = (wr - wrh.astype(F32)).astype(BF16)
    logits = _dot(h2b, wrh) + _dot(h2b, wrl) + _dot(h2l, wrh)
    lane = lax.broadcasted_iota(jnp.int32, logits.shape, 1)
    valid = lane < N_EXPERTS
    logits = jnp.where(valid, logits, NEG_BIG)
    e = jnp.where(valid, jnp.exp(logits - jnp.max(logits, axis=-1, keepdims=True)), 0.0)
    aff_ref[0] = e / jnp.sum(e, axis=-1, keepdims=True)


def _merge(xs, mod, n1, n2, oa, ob, oc, wm, wa, wb, wc, wo, wr, lc, with_ctx):
    b, ts, d = xs.shape
    n_ctx_tiles = lc // TM
    t_off = 0 if with_ctx else n_ctx_tiles
    nt = ts // TM - t_off
    tok = lambda width: pl.BlockSpec((1, TM, width), lambda i, j: (i, j + t_off, 0))
    full = lambda a: pl.BlockSpec(a.shape, lambda i, j: (0,) * a.ndim)
    return pl.pallas_call(
        _merge_kernel,
        grid=(b, nt),
        in_specs=[tok(d),
                  pl.BlockSpec((1, 1, mod.shape[-1]), lambda i, j: (jnp.where(j + t_off < n_ctx_tiles, 8, i), 0, 0)),
                  full(n1), full(n2), tok(MIX_W), tok(MIX_W), tok(MIX_W),
                  full(wm), full(wa), full(wb), full(wc), full(wo), full(wr)],
        out_specs=[tok(d), tok(d), tok(LANES)],
        out_shape=[jax.ShapeDtypeStruct((b, ts, d), F32),
                   jax.ShapeDtypeStruct((b, ts, d), BF16),
                   jax.ShapeDtypeStruct((b, ts, LANES), F32)],
        input_output_aliases={0: 0},
        compiler_params=_cparams(2),
        name="merge_out",
    )(xs, mod, n1, n2, oa, ob, oc, wm, wa, wb, wc, wo, wr)


def _route_kernel(aff_ref, post_ref, posr_ref, affr_ref, *, cap):
    n = aff_ref.shape[0]
    ch = LANES
    aff = aff_ref[...]

    def step(i, thr_bits):
        cand = thr_bits | jnp.left_shift(jnp.int32(1), 30 - i)
        cnt = jnp.sum((aff >= pltpu.bitcast(cand, F32)).astype(jnp.int32), axis=0, keepdims=True)
        return jnp.where(cnt >= cap, cand, thr_bits)

    thr = pltpu.bitcast(lax.fori_loop(0, 31, step, jnp.zeros((1, LANES), jnp.int32)), F32)
    gt = aff > thr
    eq = aff == thr
    need = cap - jnp.sum(gt.astype(jnp.int32), axis=0, keepdims=True)

    ri = lax.broadcasted_iota(jnp.int32, (ch, ch), 0)
    ci = lax.broadcasted_iota(jnp.int32, (ch, ch), 1)
    strict_lower = jnp.where(ci < ri, 1.0, 0.0).astype(BF16)

    def excl_cumsum(mask_f):
        carry = jnp.zeros((1, LANES), F32)
        blocks = []
        for c in range(n // ch):
            blk = mask_f[c * ch:(c + 1) * ch, :]
            blocks.append(_dot(strict_lower, blk.astype(BF16)) + carry)
            carry = carry + jnp.sum(blk, axis=0, keepdims=True)
        return jnp.concatenate(blocks, axis=0)

    eq_rank = excl_cumsum(jnp.where(eq, 1.0, 0.0))
    sel = gt | (eq & (eq_rank < need.astype(F32)))
    pos = excl_cumsum(jnp.where(sel, 1.0, 0.0))
    post = jnp.where(sel, pos, -1.0)
    post_ref[...] = post
    for c in range(n // ch):
        posr_ref[:, c * ch:(c + 1) * ch] = post[c * ch:(c + 1) * ch, :].T
        affr_ref[:, c * ch:(c + 1) * ch] = aff[c * ch:(c + 1) * ch, :].T


def _route(aff_t, cap):
    n = aff_t.shape[0]
    return pl.pallas_call(
        functools.partial(_route_kernel, cap=cap),
        out_shape=[jax.ShapeDtypeStruct((n, LANES), F32),
                   jax.ShapeDtypeStruct((LANES, n), F32),
                   jax.ShapeDtypeStruct((LANES, n), F32)],
        compiler_params=pltpu.CompilerParams(vmem_limit_bytes=VMEM_LIMIT),
        name="route",
    )(aff_t)


def _gather_kernel(posr_ref, affr_ref, h_ref, xe_ref, gs_ref, *, cap, row0, n):
    pos = posr_ref[0]
    slot = lax.broadcasted_iota(jnp.int32, (cap, n), 0).astype(F32)
    pf = jnp.where(pos == slot, 1.0, 0.0)
    xe = _dot(pf.astype(BF16), h_ref[0, row0:row0 + n, :])
    xe_ref[0] = xe.astype(BF16)
    gs = jnp.sum(pf * affr_ref[0], axis=-1, keepdims=True)
    gs_ref[0] = jnp.broadcast_to(gs, (cap, LANES))


def _gather(posr, affr, h2, cap, row0, n, b):
    ts, d = h2.shape[1:]
    be = posr.shape[0]
    posr3 = posr.reshape(be, 1, n)
    affr3 = affr.reshape(be, 1, n)
    e = N_EXPERTS
    return pl.pallas_call(
        functools.partial(_gather_kernel, cap=cap, row0=row0, n=n),
        grid=(b, e),
        in_specs=[pl.BlockSpec((1, 1, n), lambda i, j: (i * e + j, 0, 0)),
                  pl.BlockSpec((1, 1, n), lambda i, j: (i * e + j, 0, 0)),
                  pl.BlockSpec((1, ts, d), lambda i, j: (i, 0, 0))],
        out_specs=[pl.BlockSpec((1, cap, d), lambda i, j: (j, i, 0)),
                   pl.BlockSpec((1, cap, LANES), lambda i, j: (j, i, 0))],
        out_shape=[jax.ShapeDtypeStruct((e, b * cap, d), BF16),
                   jax.ShapeDtypeStruct((e, b * cap, LANES), F32)],
        compiler_params=_cparams(2),
        name="expert_gather",
    )(posr3, affr3, h2)


def _ffn_kernel(*refs, n_groups):
    xe_refs = refs[0:n_groups]
    gs_refs = refs[n_groups:2 * n_groups]
    wg_ref, wu_ref, wd_ref = refs[2 * n_groups:2 * n_groups + 3]
    out_refs = refs[2 * n_groups + 3:3 * n_groups + 3]
    acc_refs = refs[3 * n_groups + 3:4 * n_groups + 3]
    wgb_s, wub_s, wdb_s = refs[4 * n_groups + 3:]
    f = pl.program_id(1)
    nf = pl.num_programs(1)
    wgb_s[...] = wg_ref[0].astype(BF16)
    wub_s[...] = wu_ref[0].astype(BF16)
    wdb_s[...] = wd_ref[0].astype(BF16)
    for xe_ref, gs_ref, out_ref, acc_ref in zip(xe_refs, gs_refs, out_refs, acc_refs):
        rows = xe_ref.shape[1]
        rb = min(FFN_ROWS, rows)

        @pl.when(f == 0)
        def _():
            acc_ref[...] = jnp.zeros_like(acc_ref)

        for r0 in range(0, rows, rb):
            xb = xe_ref[0, r0:r0 + rb, :]
            a = _dot(xb, wgb_s[...])
            u = _dot(xb, wub_s[...])
            y = _dot((_silu(a) * u).astype(BF16), wdb_s[...])
            acc_ref[r0:r0 + rb, :] = acc_ref[r0:r0 + rb, :] + y

        @pl.when(f == nf - 1)
        def _():
            gate = gs_ref[0][:, 0:1]
            out_ref[0] = (acc_ref[...] * gate).astype(BF16)


def _ffn(xes, gss, w_gate, w_up, w_down):
    n_groups = len(xes)
    e, d, ff = w_gate.shape
    nf = ff // FF_TILE
    in_specs = [pl.BlockSpec((1,) + x.shape[1:], lambda i, f: (i, 0, 0)) for x in xes]
    in_specs += [pl.BlockSpec((1,) + g.shape[1:], lambda i, f: (i, 0, 0)) for g in gss]
    in_specs += [pl.BlockSpec((1, d, FF_TILE), lambda i, f: (i, 0, f)),
                 pl.BlockSpec((1, d, FF_TILE), lambda i, f: (i, 0, f)),
                 pl.BlockSpec((1, FF_TILE, d), lambda i, f: (i, f, 0))]
    out = pl.pallas_call(
        functools.partial(_ffn_kernel, n_groups=n_groups),
        grid=(e, nf),
        in_specs=in_specs,
        out_specs=[pl.BlockSpec((1,) + x.shape[1:], lambda i, f: (i, 0, 0)) for x in xes],
        out_shape=[jax.ShapeDtypeStruct(x.shape, BF16) for x in xes],
        scratch_shapes=[pltpu.VMEM(x.shape[1:], F32) for x in xes]
        + [pltpu.VMEM((d, FF_TILE), BF16), pltpu.VMEM((d, FF_TILE), BF16), pltpu.VMEM((FF_TILE, d), BF16)],
        compiler_params=_cparams(2),
        name="expert_ffn",
    )(*xes, *gss, w_gate, w_up, w_down)
    return list(out)


def _combine_kernel(*refs, cap, final):
    if final:
        post_ref, yg_ref, x_ref, mod_ref, fw_ref, o_ref = refs
    else:
        post_ref, yg_ref, x_ref, mod_ref, o_ref = refs
    d = D_MODEL
    post = post_ref[0]
    tn = post.shape[0]
    slot = lax.broadcasted_iota(jnp.int32, (tn, cap), 1).astype(F32)
    acc = jnp.zeros((tn, d), F32)
    for e in range(N_EXPERTS):
        onehot = jnp.where(post[:, e:e + 1] == slot, 1.0, 0.0).astype(BF16)
        acc = acc + _dot(onehot, yg_ref[e])
    x2 = x_ref[0] + mod_ref[0][:, 5 * d:6 * d] * acc
    if final:
        ms = jnp.mean(x2 * x2, axis=-1, keepdims=True)
        o_ref[0] = x2 * lax.rsqrt(ms + EPS) * fw_ref[...]
    else:
        o_ref[0] = x2


def _combine(post_b, yg, xs, mod, cap, row0, n, mod_row_ctx, final_w=None):
    b, ts, d = xs.shape
    tn = min(TM, n)
    blk0 = row0 // tn
    final = final_w is not None
    in_specs = [pl.BlockSpec((1, tn, LANES), lambda i, j: (i, j, 0)),
                pl.BlockSpec((N_EXPERTS, cap, d), lambda i, j: (0, i, 0)),
                pl.BlockSpec((1, tn, d), lambda i, j: (i, j + blk0, 0)),
                pl.BlockSpec((1, 1, mod.shape[-1]), lambda i, j: (8 if mod_row_ctx else i, 0, 0))]
    args = [post_b, yg, xs, mod]
    if final:
        in_specs.append(pl.BlockSpec(final_w.shape, lambda i, j: (0, 0)))
        args.append(final_w)
        out_spec = pl.BlockSpec((1, tn, d), lambda i, j: (i, j, 0))
        out_shape = jax.ShapeDtypeStruct((b, n, d), F32)
        aliases = {}
    else:
        out_spec = pl.BlockSpec((1, tn, d), lambda i, j: (i, j + blk0, 0))
        out_shape = jax.ShapeDtypeStruct((b, ts, d), F32)
        aliases = {2: 0}
    return pl.pallas_call(
        functools.partial(_combine_kernel, cap=cap, final=final),
        grid=(b, n // tn),
        in_specs=in_specs,
        out_specs=out_spec,
        out_shape=out_shape,
        input_output_aliases=aliases,
        compiler_params=_cparams(2),
        name="expert_combine_final" if final else "expert_combine",
    )(*args)


def _rope_tables(n_lat, lc):
    n_rows = n_lat // GRID_W
    rows = jnp.repeat(jnp.arange(n_rows, dtype=F32), GRID_W)
    cols = jnp.tile(jnp.arange(GRID_W, dtype=F32), n_rows)
    n_freq = A_HEAD_DIM // 4
    inv_freq = ROPE_THETA ** (-jnp.arange(n_freq, dtype=F32) / n_freq)
    ang = jnp.concatenate([rows[:, None] * inv_freq, cols[:, None] * inv_freq], axis=-1)
    c, s = jnp.cos(ang), jnp.sin(ang)
    cos = jnp.concatenate([c, c, c, c], axis=-1)
    sin = jnp.concatenate([-s, s, -s, s], axis=-1)
    cos = jnp.concatenate([jnp.ones((lc, LANES), F32), cos], axis=0)
    sin = jnp.concatenate([jnp.zeros((lc, LANES), F32), sin], axis=0)
    return cos, sin


def _route_layout(aff, row0, n, b):
    a = aff[:, row0:row0 + n, :N_EXPERTS]
    a = jnp.transpose(a, (1, 0, 2)).reshape(n, b * N_EXPERTS)
    return jnp.pad(a, ((0, 0), (0, LANES - b * N_EXPERTS)))


def _sample_layout(post, n, b):
    p = post[:, :b * N_EXPERTS].reshape(n, b, N_EXPERTS)
    p = jnp.transpose(p, (1, 0, 2))
    return jnp.pad(p, ((0, 0), (0, 0), (0, LANES - N_EXPERTS)), constant_values=-1.0)


def kernel(x, c, ctx, c_ctx, w_ada, b_ada, norm1_w, norm2_w, w_in, mlstm_conv_w, mlstm_gate_b, mlstm_norm_w,
           diff_lambda, diff_subln_w, gqa_qnorm_w, gqa_knorm_w, w_branch_a, w_branch_b, w_branch_c, w_out,
           w_router, w_exp_gate, w_exp_up, w_exp_down, final_norm_w):
    b, n_lat, d = x.shape
    lc = ctx.shape[1]
    depth = w_ada.shape[0]
    assert d == D_MODEL and b * N_EXPERTS <= LANES and b <= 8
    assert lc % TM == 0 and n_lat % TM == 0 and lc % TQ == 0

    xs = jnp.concatenate([ctx, x], axis=1)
    cvec = jnp.zeros((16, d), F32).at[:b].set(c).at[8].set(c_ctx)
    mods = _ada(cvec, w_ada, b_ada)
    cos, sin = _rope_tables(n_lat, lc)
    gidx = jnp.arange(LANES) // B_HEAD_DIM
    gmat = jnp.where(gidx[:, None] == gidx[None, :], 1.0 / B_HEAD_DIM, 0.0).astype(BF16)
    cap_lat = EC_CAPACITY_FACTOR * n_lat // N_EXPERTS
    cap_ctx = EC_CAPACITY_FACTOR * lc // N_EXPERTS
    out = None

    for layer in range(depth):
        with_ctx = layer < depth - 1
        mod = mods[layer].reshape(16, 1, 6 * d)
        wl = w_in[layer]
        w_main = wl[:, :MAIN_COLS].astype(BF16)
        w_gates = wl[:, GATE_COL0:GATE_COL0 + N_GATES]
        wg = jnp.pad(w_gates, ((0, 0), (0, LANES - N_GATES))).astype(BF16)
        wgt = w_gates.T.astype(BF16)
        gb = jnp.pad(mlstm_gate_b[layer], (0, LANES - N_GATES)).reshape(1, LANES)
        gbt = mlstm_gate_b[layer].reshape(N_GATES, 1)
        n1 = norm1_w[layer].reshape(1, d)
        n2 = norm2_w[layer].reshape(1, d)
        qnw = jnp.tile(gqa_qnorm_w[layer], LANES // B_HEAD_DIM).reshape(1, LANES)
        knw = jnp.tile(gqa_knorm_w[layer], LANES // B_HEAD_DIM).reshape(1, LANES)

        qa, ka, va, qb, kb, vb, qkc, vc, oc, g, gt = _inproj(
            xs, mod, n1, w_main, wg, wgt, gb, gbt, cos, sin, qnw, knw, gmat, lc // TM)

        lam_init = 0.8 - 0.6 * math.exp(-0.3 * layer)
        oa = _diff_attn(diff_lambda[layer], diff_subln_w[layer].reshape(1, 2 * A_HEAD_DIM),
                        qa, ka, va, lam_init, lc, with_ctx)
        ob = _gqa_attn(qb, kb, vb, lc, with_ctx)
        ocm = _mlstm(qkc, vc, oc, g, gt, mlstm_conv_w[layer], mlstm_norm_w[layer], lc)

        wm = wl[:, MERGE_COL0:].astype(BF16)
        wr = jnp.pad(w_router[layer], ((0, 0), (0, LANES - N_EXPERTS)))
        xs, h2, aff = _merge(xs, mod, n1, n2, oa, ob, ocm, wm,
                             w_branch_a[layer].astype(BF16), w_branch_b[layer].astype(BF16),
                             w_branch_c[layer].astype(BF16), w_out[layer].astype(BF16), wr, lc, with_ctx)

        groups = [(lc, n_lat, cap_lat)]
        if with_ctx:
            groups.append((0, lc, cap_ctx))
        xes, gss, posts = [], [], []
        for row0, n, cap in groups:
            post, posr, affr = _route(_route_layout(aff, row0, n, b), cap)
            xe, gs = _gather(posr, affr, h2, cap, row0, n, b)
            xes.append(xe)
            gss.append(gs)
            posts.append(_sample_layout(post, n, b))
        ygs = _ffn(xes, gss, w_exp_gate[layer], w_exp_up[layer], w_exp_down[layer])
        for gi, (row0, n, cap) in enumerate(groups):
            is_last = (layer == depth - 1) and gi == 0
            res = _combine(posts[gi], ygs[gi], xs, mod, cap, row0, n, mod_row_ctx=(row0 == 0),
                           final_w=final_norm_w.reshape(1, d) if is_last else None)
            if is_last:
                out = res
            else:
                xs = res
    return out
```

```python
import functools
import math

import jax
import jax.numpy as jnp
from jax import lax
from jax.experimental import pallas as pl
from jax.experimental.pallas import tpu as pltpu

F32 = jnp.float32
BF16 = jnp.bfloat16

D_MODEL = 1024
DEPTH = 2
GRID_W = 64
ROPE_THETA = 10000.0
EPS = 1e-6
NEG_BIG = -1e30
MIX_W = D_MODEL // 2
A_HEAD_DIM = 64
A_HEADS = MIX_W // (2 * A_HEAD_DIM)
B_HEAD_DIM = 64
B_Q_HEADS = MIX_W // B_HEAD_DIM
B_KV_HEADS = 2
B_GROUP = B_Q_HEADS // B_KV_HEADS
C_HEAD_DIM = 128
C_HEADS = MIX_W // C_HEAD_DIM
C_CONV = 3
C_CHUNK = 128
N_BRANCH = 3
N_EXPERTS = 16
EXPERT_FF = 2 * D_MODEL
EC_CAPACITY_FACTOR = 2

LANES = 128
KV_B = B_KV_HEADS * B_HEAD_DIM
N_GATES = 4 * C_HEADS
MAIN_COLS = 8 * MIX_W + 2 * KV_B
GATE_COL0 = MAIN_COLS
MERGE_COL0 = MAIN_COLS + N_GATES
TM = 256
TQ = 256
FF_TILE = 512
FFN_ROWS = 512
VMEM_LIMIT = 56 * 1024 * 1024
SCORE_SCALE = (A_HEAD_DIM ** -0.5) * math.log2(math.e)


def _cparams(n_axes, vmem=VMEM_LIMIT):
    return pltpu.CompilerParams(dimension_semantics=("arbitrary",) * n_axes, vmem_limit_bytes=vmem)


def _dot(a, b):
    return jnp.dot(a, b, preferred_element_type=F32)


def _dot_nt(a, b):
    return lax.dot_general(a, b, (((1,), (1,)), ((), ())), preferred_element_type=F32)


def _split3(x):
    a = x.astype(BF16)
    r = x - a.astype(F32)
    b = r.astype(BF16)
    c = (r - b.astype(F32)).astype(BF16)
    return a, b, c


def _sigmoid(x):
    return 1.0 / (1.0 + jnp.exp(-x))


def _silu(x):
    return x * _sigmoid(x)


def _log_sigmoid(x):
    return jnp.minimum(x, 0.0) - jnp.log(1.0 + jnp.exp(-jnp.abs(x)))


def _norm_mod(x, nw, shift, scale):
    ms = jnp.mean(x * x, axis=-1, keepdims=True)
    return (x * lax.rsqrt(ms + EPS) * nw) * (1.0 + scale) + shift


def _ada_kernel(c_ref, w_ref, b_ref, o_ref):
    s = _silu(c_ref[...])
    s1, s2, _ = _split3(s)
    w = w_ref[0]
    w1, w2, _ = _split3(w)
    o_ref[0] = _dot(s1, w1) + _dot(s1, w2) + _dot(s2, w1) + b_ref[0]


def _ada(cvec, w_ada, b_ada):
    depth, d, n = w_ada.shape
    tn = 1536
    return pl.pallas_call(
        _ada_kernel,
        grid=(depth, n // tn),
        in_specs=[pl.BlockSpec((16, d), lambda l, j: (0, 0)),
                  pl.BlockSpec((1, d, tn), lambda l, j: (l, 0, j)),
                  pl.BlockSpec((1, 1, tn), lambda l, j: (l, 0, j))],
        out_specs=pl.BlockSpec((1, 16, tn), lambda l, j: (l, 0, j)),
        out_shape=jax.ShapeDtypeStruct((depth, 16, n), F32),
        compiler_params=_cparams(2),
        name="ada_mod",
    )(cvec, w_ada, b_ada.reshape(depth, 1, n))


def _inproj_kernel(x_ref, mod_ref, nw_ref, w_ref, wg_ref, wgt_ref, gb_ref, gbt_ref, cos_ref, sin_ref,
                   qnw_ref, knw_ref, gm_ref,
                   qa_ref, ka_ref, va_ref, qb_ref, kb_ref, vb_ref, qkc_ref, vc_ref, oc_ref, g_ref, gt_ref):
    d = D_MODEL
    x = x_ref[0]
    mod = mod_ref[0]
    h = _norm_mod(x, nw_ref[...], mod[:, 0:d], mod[:, d:2 * d]).astype(BF16)
    tm = x.shape[0]
    cos = cos_ref[...]
    sin = sin_ref[...]
    lane = lax.broadcasted_iota(jnp.int32, (tm, LANES), 1)
    first_half = (lane % A_HEAD_DIM) < (A_HEAD_DIM // 2)
    gm = gm_ref[...]

    def rope(p):
        partner = jnp.where(first_half, pltpu.roll(p, LANES - 32, 1), pltpu.roll(p, 32, 1))
        return p * cos + partner * sin

    def head_norm(p, w):
        sq = p * p
        hi = sq.astype(BF16)
        lo = (sq - hi.astype(F32)).astype(BF16)
        ms = _dot(hi, gm) + _dot(lo, gm)
        return p * lax.rsqrt(ms + EPS) * w

    def proj(c0):
        return _dot(h, w_ref[:, c0:c0 + 2 * LANES])

    def halves(p):
        return p[:, :LANES], p[:, LANES:]

    for j in range(2):
        for half, p in enumerate(halves(proj(j * 256))):
            c = j * 256 + half * LANES
            qa_ref[0, :, c:c + LANES] = (rope(p) * SCORE_SCALE).astype(BF16)
    for j in range(2):
        for half, p in enumerate(halves(proj(512 + j * 256))):
            c = j * 256 + half * LANES
            ka_ref[0, :, c:c + LANES] = rope(p).astype(BF16)
    for j in range(2):
        va_ref[0, :, j * 256:(j + 1) * 256] = proj(1024 + j * 256).astype(BF16)
    qnw = qnw_ref[...]
    for j in range(2):
        for half, p in enumerate(halves(proj(1536 + j * 256))):
            c = j * 256 + half * LANES
            qb_ref[0, :, c:c + LANES] = (rope(head_norm(p, qnw)) * SCORE_SCALE).astype(BF16)
    pk, pv = halves(proj(2048))
    kb_ref[0] = rope(head_norm(pk, knw_ref[...])).astype(BF16)
    vb_ref[0] = pv.astype(BF16)
    for j in range(4):
        qkc_ref[0, :, j * 256:(j + 1) * 256] = proj(2304 + j * 256)
    for j in range(2):
        vc_ref[0, :, j * 256:(j + 1) * 256] = proj(3328 + j * 256).astype(BF16)
    for j in range(2):
        oc_ref[0, :, j * 256:(j + 1) * 256] = _sigmoid(proj(3840 + j * 256)).astype(BF16)
    g = _dot(h, wg_ref[...]) + gb_ref[...]
    is_f = ((lane % 8) >= 4) & (lane < N_GATES)
    g_ref[0] = jnp.where(is_f, _log_sigmoid(g), g)
    gt = _dot_nt(wgt_ref[...], h) + gbt_ref[...]
    row = lax.broadcasted_iota(jnp.int32, (N_GATES, tm), 0)
    gt_ref[0] = jnp.where((row % 8) >= 4, _log_sigmoid(gt), gt)


def _inproj(xs, mod, nw, w_main, wg, wgt, gb, gbt, cos, sin, qnw, knw, gmat, n_ctx_tiles):
    b, ts, d = xs.shape
    nt = ts // TM
    tok = lambda width: pl.BlockSpec((1, TM, width), lambda i, j: (i, j, 0))
    full = lambda a: pl.BlockSpec(a.shape, lambda i, j: (0,) * a.ndim)
    outs = [(MIX_W, BF16), (MIX_W, BF16), (MIX_W, BF16), (MIX_W, BF16), (KV_B, BF16), (KV_B, BF16),
            (2 * MIX_W, F32), (MIX_W, BF16), (MIX_W, BF16), (LANES, F32)]
    out_shape = [jax.ShapeDtypeStruct((b, ts, w), dt) for w, dt in outs]
    out_specs = [tok(w) for w, _ in outs]
    out_shape.append(jax.ShapeDtypeStruct((b, N_GATES, ts), F32))
    out_specs.append(pl.BlockSpec((1, N_GATES, TM), lambda i, j: (i, 0, j)))
    return pl.pallas_call(
        _inproj_kernel,
        grid=(b, nt),
        in_specs=[tok(d),
                  pl.BlockSpec((1, 1, mod.shape[-1]), lambda i, j: (jnp.where(j < n_ctx_tiles, 8, i), 0, 0)),
                  full(nw), full(w_main), full(wg), full(wgt), full(gb), full(gbt),
                  pl.BlockSpec((TM, LANES), lambda i, j: (j, 0)),
                  pl.BlockSpec((TM, LANES), lambda i, j: (j, 0)),
                  full(qnw), full(knw), full(gmat)],
        out_specs=out_specs,
        out_shape=out_shape,
        compiler_params=_cparams(2),
        name="in_proj",
    )(xs, mod, nw, w_main, wg, wgt, gb, gbt, cos, sin, qnw, knw, gmat)


def _softmax_numerators(sb, eb, rows, lk):
    maxes = [jnp.max(sb[r:r + 8, :lk], axis=-1, keepdims=True) for r in range(0, rows, 8)]
    for rb in range(rows // 16):
        parts = [jnp.exp2(sb[rb * 16 + sub * 8:rb * 16 + sub * 8 + 8, :lk] - maxes[2 * rb + sub]) for sub in range(2)]
        eb[rb * 16:(rb + 1) * 16, :lk] = jnp.concatenate(parts, axis=0).astype(BF16)


def _with_ones_column(v):
    lane = lax.broadcasted_iota(jnp.int32, v.shape, 1)
    ones = jnp.where(lane == 0, 1.0, 0.0).astype(v.dtype)
    return jnp.concatenate([v, ones], axis=1)


def _attention_units(n_units, scores, finish, values, s_s, e_s, rows, lk):
    scores(0, s_s.at[0])
    for u in range(n_units):
        if u + 1 < n_units:
            scores(u + 1, s_s.at[(u + 1) % 2])
        _softmax_numerators(s_s.at[u % 2], e_s.at[u % 2], rows, lk)
        finish(u, _dot(e_s[u % 2, :, :lk], values(u)))


def _diff_attn_kernel(lam_ref, sub_ref, q_ref, k_ref, v_ref, o_ref, s_s, e_s, *, lam_init, n_ctx_blocks, q_off, lc):
    qi = pl.program_id(1) + q_off
    lv = lam_ref[...]
    lam = (jnp.exp(jnp.sum(lv[0:1] * lv[1:2], axis=-1, keepdims=True))
           - jnp.exp(jnp.sum(lv[2:3] * lv[3:4], axis=-1, keepdims=True)) + lam_init)
    tq = q_ref.shape[1]
    w = 2 * A_HEAD_DIM

    def body(lk):
        low = lax.broadcasted_iota(jnp.int32, (tq, w), 1) < A_HEAD_DIM

        def scores(h, sb):
            cols = slice(h * w, (h + 1) * w)
            q = q_ref[0, :, cols]
            zero = jnp.zeros_like(q)
            qs = jnp.concatenate([jnp.where(low, q, zero), jnp.where(low, zero, q)], axis=0)
            sb[:, :lk] = _dot_nt(qs, k_ref[0, :lk, cols])

        def values(h):
            return _with_ones_column(v_ref[0, :lk, h * w:(h + 1) * w])

        def finish(h, pv):
            o = (pv[:tq, :w] * (1.0 / pv[:tq, w:w + 1])
                 - pv[tq:, :w] * (lam / pv[tq:, w:w + 1]))
            ms = jnp.mean(o * o, axis=-1, keepdims=True)
            o_ref[0, :, h * w:(h + 1) * w] = (o * lax.rsqrt(ms + EPS) * sub_ref[...] * (1.0 - lam_init)).astype(BF16)

        _attention_units(A_HEADS, scores, finish, values, s_s, e_s, 2 * tq, lk)

    @pl.when(qi < n_ctx_blocks)
    def _():
        body(lc)

    @pl.when(qi >= n_ctx_blocks)
    def _():
        body(k_ref.shape[1])


def _diff_attn(lam_vecs, sub_w, qa, ka, va, lam_init, lc, with_ctx):
    b, ts, _ = qa.shape
    n_ctx_blocks = lc // TQ
    q_off = 0 if with_ctx else n_ctx_blocks
    nq = ts // TQ - q_off
    kern = functools.partial(_diff_attn_kernel, lam_init=lam_init, n_ctx_blocks=n_ctx_blocks, q_off=q_off, lc=lc)
    return pl.pallas_call(
        kern,
        grid=(b, nq),
        in_specs=[pl.BlockSpec(lam_vecs.shape, lambda i, j: (0, 0)),
                  pl.BlockSpec(sub_w.shape, lambda i, j: (0, 0)),
                  pl.BlockSpec((1, TQ, MIX_W), lambda i, j: (i, j + q_off, 0)),
                  pl.BlockSpec((1, ts, MIX_W), lambda i, j: (i, 0, 0)),
                  pl.BlockSpec((1, ts, MIX_W), lambda i, j: (i, 0, 0))],
        out_specs=pl.BlockSpec((1, TQ, MIX_W), lambda i, j: (i, j + q_off, 0)),
        out_shape=jax.ShapeDtypeStruct((b, ts, MIX_W), BF16),
        scratch_shapes=[pltpu.VMEM((2, 2 * TQ, ts), F32), pltpu.VMEM((2, 2 * TQ, ts), BF16)],
        compiler_params=_cparams(2),
        name="diff_attn",
    )(lam_vecs, sub_w, qa, ka, va)


def _gqa_kernel(q_ref, k_ref, v_ref, o_ref, s_s, e_s, *, n_ctx_blocks, q_off, lc):
    qi = pl.program_id(1) + q_off
    tq = q_ref.shape[1]

    def body(lk):
        low = lax.broadcasted_iota(jnp.int32, (tq, LANES), 1) < B_HEAD_DIM
        k = k_ref[0, :lk, :]
        v1 = _with_ones_column(v_ref[0, :lk, :])
        outs = {}
        pairs = B_GROUP // 2

        def scores(u, sb):
            g, pair = divmod(u, pairs)
            parts = []
            for j in (2 * pair, 2 * pair + 1):
                x = q_ref[0, :, j * LANES:(j + 1) * LANES]
                zero = jnp.zeros_like(x)
                parts.append(jnp.where(low, x, zero) if g == 0 else jnp.where(low, zero, x))
            sb[:, :lk] = _dot_nt(jnp.concatenate(parts, axis=0), k)

        def finish(u, pv):
            g, pair = divmod(u, pairs)
            o = pv[:, :LANES] * (1.0 / pv[:, LANES:LANES + 1])
            outs[(g, 2 * pair)] = o[:tq]
            outs[(g, 2 * pair + 1)] = o[tq:]

        _attention_units(B_KV_HEADS * pairs, scores, finish, lambda u: v1, s_s, e_s, 2 * tq, lk)
        for j in range(B_GROUP):
            o_ref[0, :, j * LANES:(j + 1) * LANES] = jnp.where(low, outs[(0, j)], outs[(1, j)]).astype(BF16)

    @pl.when(qi < n_ctx_blocks)
    def _():
        body(lc)

    @pl.when(qi >= n_ctx_blocks)
    def _():
        body(k_ref.shape[1])


def _gqa_attn(qb, kb, vb, lc, with_ctx):
    b, ts, _ = qb.shape
    n_ctx_blocks = lc // TQ
    q_off = 0 if with_ctx else n_ctx_blocks
    nq = ts // TQ - q_off
    kern = functools.partial(_gqa_kernel, n_ctx_blocks=n_ctx_blocks, q_off=q_off, lc=lc)
    return pl.pallas_call(
        kern,
        grid=(b, nq),
        in_specs=[pl.BlockSpec((1, TQ, MIX_W), lambda i, j: (i, j + q_off, 0)),
                  pl.BlockSpec((1, ts, KV_B), lambda i, j: (i, 0, 0)),
                  pl.BlockSpec((1, ts, KV_B), lambda i, j: (i, 0, 0))],
        out_specs=pl.BlockSpec((1, TQ, MIX_W), lambda i, j: (i, j + q_off, 0)),
        out_shape=jax.ShapeDtypeStruct((b, ts, MIX_W), BF16),
        scratch_shapes=[pltpu.VMEM((2, 2 * TQ, ts), F32), pltpu.VMEM((2, 2 * TQ, ts), BF16)],
        compiler_params=_cparams(2),
        name="gqa_attn",
    )(qb, kb, vb)


def _mlstm_kernel(q_ref, k_ref, v_ref, o_ref, g_ref, gt_ref, cwq_ref, cwk_ref, nw_ref, out_ref,
                  q_s, kt_s, bc_s, ac_s, rows_s, hacc_s, st_s, *, lc):
    ts = q_ref.shape[1]
    hp = q_ref.shape[2] // C_HEAD_DIM
    head0 = pl.program_id(1) * hp
    nc = ts // C_CHUNK
    ncc = lc // C_CHUNK
    ch = C_CHUNK

    row = lax.broadcasted_iota(jnp.int32, (ts, LANES), 0)
    prev_ok = (row != 0) & (row != lc)
    next_ok = (row != lc - 1) & (row != ts - 1)

    def conv(x, w):
        xp = jnp.where(prev_ok, pltpu.roll(x, 1, 0), 0.0)
        xn = jnp.where(next_ok, pltpu.roll(x, ts - 1, 0), 0.0)
        return _silu(xp * w[0:1] + x * w[1:2] + xn * w[2:3])

    for j in range(hp):
        cols = slice(j * LANES, (j + 1) * LANES)
        q_s[:, cols] = conv(q_ref[0, :, cols], cwq_ref[0, :, cols]).astype(BF16)
        y = conv(k_ref[0, :, cols], cwk_ref[0, :, cols]) * (C_HEAD_DIM ** -0.5)
        for c in range(nc):
            kt_s[c, cols, :] = y[c * ch:(c + 1) * ch, :].T.astype(BF16)

    ri = lax.broadcasted_iota(jnp.int32, (ch, ch), 0)
    ci = lax.broadcasted_iota(jnp.int32, (ch, ch), 1)
    lower = jnp.where(ci <= ri, 1.0, 0.0).astype(BF16)
    upper = jnp.where(ci >= ri, 1.0, 0.0).astype(BF16)
    lane = ci
    rowi = lax.broadcasted_iota(jnp.int32, (N_GATES, ch), 0)
    for c in range(nc):
        rs = slice(c * ch, (c + 1) * ch)
        g = g_ref[0, rs, :]
        g1, g2, g3 = _split3(g)
        pre = _dot(lower, g1) + _dot(lower, g2) + _dot(lower, g3)
        suf = _dot(upper, g1) + _dot(upper, g2) + _dot(upper, g3)
        gt = gt_ref[0, :, rs]
        t1, t2, t3 = _split3(gt)
        pre_t = _dot(t1, upper) + _dot(t2, upper) + _dot(t3, upper)
        suf_t = _dot(t1, lower) + _dot(t2, lower) + _dot(t3, lower)
        for j in range(hp):
            for direction in range(2):
                idx = j * 2 + direction
                li = head0 + j + 8 * direction
                lf = li + 4
                cum, cum_t = (pre, pre_t) if direction == 0 else (suf, suf_t)
                b_col = jnp.sum(jnp.where(lane == lf, cum, 0.0), axis=-1, keepdims=True)
                i_col = jnp.sum(jnp.where(lane == li, g, 0.0), axis=-1, keepdims=True)
                bc_s[idx, rs, :] = jnp.broadcast_to(b_col, (ch, LANES))
                ac_s[idx, rs, :] = jnp.broadcast_to(i_col - b_col, (ch, LANES))
                b_row = jnp.sum(jnp.where(rowi == lf, cum_t, 0.0), axis=0, keepdims=True)
                i_row = jnp.sum(jnp.where(rowi == li, gt, 0.0), axis=0, keepdims=True)
                rows_s[c, 2 * idx:2 * idx + 1, :] = b_row
                rows_s[c, 2 * idx + 1:2 * idx + 2, :] = i_row - b_row

    hacc_s[...] = jnp.zeros_like(hacc_s)
    st_s[...] = jnp.zeros_like(st_s)
    tri_f = ci <= ri
    tri_b = ci >= ri

    def chain(c, j, direction, m):
        idx = j * 2 + direction
        c0 = pl.multiple_of(c * ch, ch)
        cols = slice(j * LANES, (j + 1) * LANES)
        q = q_s[pl.ds(c0, ch), cols]
        kt = kt_s[c, cols, :]
        v = v_ref[0, pl.ds(c0, ch), cols]
        bc = bc_s[idx, pl.ds(c0, ch), :]
        ac = ac_s[idx, pl.ds(c0, ch), :]
        rows = rows_s[c]
        b_row = rows[2 * idx:2 * idx + 1, :]
        ib_row = rows[2 * idx + 1:2 * idx + 2, :]
        tri = tri_f if direction == 0 else tri_b
        log_d = jnp.where(tri, bc + ib_row, NEG_BIG)
        m_intra = jnp.max(log_d, axis=-1, keepdims=True)
        log_inter = bc + m
        m_t = jnp.maximum(log_inter, m_intra)
        dm = jnp.exp(log_d - m_t)
        w_inter = jnp.exp(log_inter - m_t)
        s = _dot(q, kt) * dm
        st = st_s[idx]
        inter = _dot(q, st.astype(BF16))
        num = _dot(s.astype(BF16), v) + w_inter * inter[:, :LANES]
        den = jnp.sum(s, axis=-1, keepdims=True) + w_inter * inter[:, LANES:LANES + 1]
        hout = num / jnp.maximum(jnp.abs(den), jnp.exp(-m_t))
        hacc_s[pl.ds(c0, ch), cols] = hacc_s[pl.ds(c0, ch), cols] + hout
        total = b_row[:, ch - 1:ch] if direction == 0 else b_row[:, 0:1]
        m_new = jnp.maximum(total + m, jnp.max(total + ib_row, axis=-1, keepdims=True))
        w = jnp.exp(total + ac - m_new)
        decay = jnp.exp(total + m - m_new)
        wv = jnp.concatenate([w * v.astype(F32), jnp.where(lane == 0, w, 0.0)], axis=1).astype(BF16)
        st_s[idx] = decay * st + _dot(kt, wv)
        return m_new

    def step(i, ms):
        c_f = i
        c_b = jnp.where(i < ncc, ncc - 1 - i, nc + ncc - 1 - i)
        out = []
        for j in range(hp):
            out.append(chain(c_f, j, 0, ms[j * 2]))
            out.append(chain(c_b, j, 1, ms[j * 2 + 1]))
        return tuple(out)

    lax.fori_loop(0, nc, step, tuple(jnp.zeros((1, 1), F32) for _ in range(2 * hp)))

    for j in range(hp):
        cols = slice(j * LANES, (j + 1) * LANES)
        x = hacc_s[:, cols]
        ms = jnp.mean(x * x, axis=-1, keepdims=True)
        y = x * lax.rsqrt(ms + EPS) * nw_ref[0, :, cols]
        out_ref[0, :, cols] = (o_ref[0, :, cols].astype(F32) * y).astype(BF16)


def _mlstm(qkc, vc, oc, g, gt, conv_w, norm_w, lc, heads_per_step=2):
    b, ts, _ = vc.shape
    wq = heads_per_step * C_HEAD_DIM
    nhp = C_HEADS // heads_per_step
    nc = ts // C_CHUNK
    cw = conv_w.reshape(C_CONV, 2 * nhp, wq).transpose(1, 0, 2)
    nw = norm_w.reshape(1, nhp, wq).transpose(1, 0, 2)
    kern = functools.partial(_mlstm_kernel, lc=lc)
    tokw = lambda off: pl.BlockSpec((1, ts, wq), lambda i, p: (i, 0, p + off))
    return pl.pallas_call(
        kern,
        grid=(b, nhp),
        in_specs=[tokw(0), tokw(nhp), tokw(0), tokw(0),
                  pl.BlockSpec((1, ts, LANES), lambda i, p: (i, 0, 0)),
                  pl.BlockSpec((1, N_GATES, ts), lambda i, p: (i, 0, 0)),
                  pl.BlockSpec((1, C_CONV, wq), lambda i, p: (p, 0, 0)),
                  pl.BlockSpec((1, C_CONV, wq), lambda i, p: (p + nhp, 0, 0)),
                  pl.BlockSpec((1, 1, wq), lambda i, p: (p, 0, 0))],
        out_specs=tokw(0),
        out_shape=jax.ShapeDtypeStruct((b, ts, MIX_W), BF16),
        scratch_shapes=[pltpu.VMEM((ts, wq), BF16),
                        pltpu.VMEM((nc, wq, C_CHUNK), BF16),
                        pltpu.VMEM((2 * heads_per_step, ts, LANES), F32),
                        pltpu.VMEM((2 * heads_per_step, ts, LANES), F32),
                        pltpu.VMEM((nc, 4 * heads_per_step, C_CHUNK), F32),
                        pltpu.VMEM((ts, wq), F32),
                        pltpu.VMEM((2 * heads_per_step, C_HEAD_DIM, 2 * LANES), F32)],
        compiler_params=_cparams(2),
        name="mlstm",
    )(qkc, qkc, vc, oc, g, gt, cw, cw, nw)


def _merge_kernel(x_ref, mod_ref, n1_ref, n2_ref, oa_ref, ob_ref, oc_ref, wm_ref, wa_ref, wb_ref, wc_ref,
                  wo_ref, wr_ref, xo_ref, h2_ref, aff_ref):
    d = D_MODEL
    x = x_ref[0]
    mod = mod_ref[0]
    h = _norm_mod(x, n1_ref[...], mod[:, 0:d], mod[:, d:2 * d]).astype(BF16)
    merged = (_sigmoid(_dot(h, wm_ref[:, 0:d])) * _dot(oa_ref[0], wa_ref[...])
              + _sigmoid(_dot(h, wm_ref[:, d:2 * d])) * _dot(ob_ref[0], wb_ref[...])
              + _sigmoid(_dot(h, wm_ref[:, 2 * d:3 * d])) * _dot(oc_ref[0], wc_ref[...]))
    y = _dot(merged.astype(BF16), wo_ref[...])
    x1 = x + mod[:, 2 * d:3 * d] * y
    xo_ref[0] = x1
    h2 = _norm_mod(x1, n2_ref[...], mod[:, 3 * d:4 * d], mod[:, 4 * d:5 * d])
    h2b = h2.astype(BF16)
    h2_ref[0] = h2b
    h2l = (h2 - h2b.astype(F32)).astype(BF16)
    wr = wr_ref[...]
    wrh = wr.astype(BF16)
    wrl = (wr - wrh.astype(F32)).astype(BF16)
    logits = _dot(h2b, wrh) + _dot(h2b, wrl) + _dot(h2l, wrh)
    lane = lax.broadcasted_iota(jnp.int32, logits.shape, 1)
    valid = lane < N_EXPERTS
    logits = jnp.where(valid, logits, NEG_BIG)
    e = jnp.where(valid, jnp.exp(logits - jnp.max(logits, axis=-1, keepdims=True)), 0.0)
    aff_ref[0] = e / jnp.sum(e, axis=-1, keepdims=True)


def _merge(xs, mod, n1, n2, oa, ob, oc, wm, wa, wb, wc, wo, wr, lc, with_ctx):
    b, ts, d = xs.shape
    n_ctx_tiles = lc // TM
    t_off = 0 if with_ctx else n_ctx_tiles
    nt = ts // TM - t_off
    tok = lambda width: pl.BlockSpec((1, TM, width), lambda i, j: (i, j + t_off, 0))
    full = lambda a: pl.BlockSpec(a.shape, lambda i, j: (0,) * a.ndim)
    return pl.pallas_call(
        _merge_kernel,
        grid=(b, nt),
        in_specs=[tok(d),
                  pl.BlockSpec((1, 1, mod.shape[-1]), lambda i, j: (jnp.where(j + t_off < n_ctx_tiles, 8, i), 0, 0)),
                  full(n1), full(n2), tok(MIX_W), tok(MIX_W), tok(MIX_W),
                  full(wm), full(wa), full(wb), full(wc), full(wo), full(wr)],
        out_specs=[tok(d), tok(d), tok(LANES)],
        out_shape=[jax.ShapeDtypeStruct((b, ts, d), F32),
                   jax.ShapeDtypeStruct((b, ts, d), BF16),
                   jax.ShapeDtypeStruct((b, ts, LANES), F32)],
        input_output_aliases={0: 0},
        compiler_params=_cparams(2),
        name="merge_out",
    )(xs, mod, n1, n2, oa, ob, oc, wm, wa, wb, wc, wo, wr)


def _route_kernel(aff_ref, post_ref, posr_ref, affr_ref, *, cap):
    n = aff_ref.shape[0]
    ch = LANES
    aff = aff_ref[...]

    def step(i, thr_bits):
        cand = thr_bits | jnp.left_shift(jnp.int32(1), 30 - i)
        cnt = jnp.sum((aff >= pltpu.bitcast(cand, F32)).astype(jnp.int32), axis=0, keepdims=True)
        return jnp.where(cnt >= cap, cand, thr_bits)

    thr = pltpu.bitcast(lax.fori_loop(0, 31, step, jnp.zeros((1, LANES), jnp.int32)), F32)
    gt = aff > thr
    eq = aff == thr
    need = cap - jnp.sum(gt.astype(jnp.int32), axis=0, keepdims=True)

    ri = lax.broadcasted_iota(jnp.int32, (ch, ch), 0)
    ci = lax.broadcasted_iota(jnp.int32, (ch, ch), 1)
    strict_lower = jnp.where(ci < ri, 1.0, 0.0).astype(BF16)

    def excl_cumsum(mask_f):
        carry = jnp.zeros((1, LANES), F32)
        blocks = []
        for c in range(n // ch):
            blk = mask_f[c * ch:(c + 1) * ch, :]
            blocks.append(_dot(strict_lower, blk.astype(BF16)) + carry)
            carry = carry + jnp.sum(blk, axis=0, keepdims=True)
        return jnp.concatenate(blocks, axis=0)

    eq_rank = excl_cumsum(jnp.where(eq, 1.0, 0.0))
    sel = gt | (eq & (eq_rank < need.astype(F32)))
    pos = excl_cumsum(jnp.where(sel, 1.0, 0.0))
    post = jnp.where(sel, pos, -1.0)
    post_ref[...] = post
    for c in range(n // ch):
        posr_ref[:, c * ch:(c + 1) * ch] = post[c * ch:(c + 1) * ch, :].T
        affr_ref[:, c * ch:(c + 1) * ch] = aff[c * ch:(c + 1) * ch, :].T


def _route(aff_t, cap):
    n = aff_t.shape[0]
    return pl.pallas_call(
        functools.partial(_route_kernel, cap=cap),
        out_shape=[jax.ShapeDtypeStruct((n, LANES), F32),
                   jax.ShapeDtypeStruct((LANES, n), F32),
                   jax.ShapeDtypeStruct((LANES, n), F32)],
        compiler_params=pltpu.CompilerParams(vmem_limit_bytes=VMEM_LIMIT),
        name="route",
    )(aff_t)


def _gather_kernel(posr_ref, affr_ref, h_ref, xe_ref, gs_ref, *, cap, row0, n):
    pos = posr_ref[0]
    slot = lax.broadcasted_iota(jnp.int32, (cap, n), 0).astype(F32)
    pf = jnp.where(pos == slot, 1.0, 0.0)
    xe = _dot(pf.astype(BF16), h_ref[0, row0:row0 + n, :])
    xe_ref[0] = xe.astype(BF16)
    gs = jnp.sum(pf * affr_ref[0], axis=-1, keepdims=True)
    gs_ref[0] = jnp.broadcast_to(gs, (cap, LANES))


def _gather(posr, affr, h2, cap, row0, n, b):
    ts, d = h2.shape[1:]
    be = posr.shape[0]
    posr3 = posr.reshape(be, 1, n)
    affr3 = affr.reshape(be, 1, n)
    e = N_EXPERTS
    return pl.pallas_call(
        functools.partial(_gather_kernel, cap=cap, row0=row0, n=n),
        grid=(b, e),
        in_specs=[pl.BlockSpec((1, 1, n), lambda i, j: (i * e + j, 0, 0)),
                  pl.BlockSpec((1, 1, n), lambda i, j: (i * e + j, 0, 0)),
                  pl.BlockSpec((1, ts, d), lambda i, j: (i, 0, 0))],
        out_specs=[pl.BlockSpec((1, cap, d), lambda i, j: (j, i, 0)),
                   pl.BlockSpec((1, cap, LANES), lambda i, j: (j, i, 0))],
        out_shape=[jax.ShapeDtypeStruct((e, b * cap, d), BF16),
                   jax.ShapeDtypeStruct((e, b * cap, LANES), F32)],
        compiler_params=_cparams(2),
        name="expert_gather",
    )(posr3, affr3, h2)


def _ffn_kernel(*refs, n_groups):
    xe_refs = refs[0:n_groups]
    gs_refs = refs[n_groups:2 * n_groups]
    wg_ref, wu_ref, wd_ref = refs[2 * n_groups:2 * n_groups + 3]
    out_refs = refs[2 * n_groups + 3:3 * n_groups + 3]
    acc_refs = refs[3 * n_groups + 3:4 * n_groups + 3]
    f = pl.program_id(1)
    nf = pl.num_programs(1)
    wgb = wg_ref[0, 0].astype(BF16)
    wub = wu_ref[0, 0].astype(BF16)
    wdb = wd_ref[0, 0].astype(BF16)
    for xe_ref, gs_ref, out_ref, acc_ref in zip(xe_refs, gs_refs, out_refs, acc_refs):
        rows = xe_ref.shape[1]
        rb = min(FFN_ROWS, rows)

        @pl.when(f == 0)
        def _():
            acc_ref[...] = jnp.zeros_like(acc_ref)

        for r0 in range(0, rows, rb):
            xb = xe_ref[0, r0:r0 + rb, :]
            a = _dot(xb, wgb)
            u = _dot(xb, wub)
            y = _dot((_silu(a) * u).astype(BF16), wdb)
            acc_ref[r0:r0 + rb, :] = acc_ref[r0:r0 + rb, :] + y

        @pl.when(f == nf - 1)
        def _():
            gate = gs_ref[0][:, 0:1]
            out_ref[0] = (acc_ref[...] * gate).astype(BF16)


def _ffn(xes, gss, w_gate, w_up, w_down, layer):
    n_groups = len(xes)
    _, e, d, ff = w_gate.shape
    nf = ff // FF_TILE
    in_specs = [pl.BlockSpec((1,) + x.shape[1:], lambda i, f: (i, 0, 0)) for x in xes]
    in_specs += [pl.BlockSpec((1,) + g.shape[1:], lambda i, f: (i, 0, 0)) for g in gss]
    in_specs += [pl.BlockSpec((1, 1, d, FF_TILE), lambda i, f: (layer, i, 0, f)),
                 pl.BlockSpec((1, 1, d, FF_TILE), lambda i, f: (layer, i, 0, f)),
                 pl.BlockSpec((1, 1, FF_TILE, d), lambda i, f: (layer, i, f, 0))]
    out = pl.pallas_call(
        functools.partial(_ffn_kernel, n_groups=n_groups),
        grid=(e, nf),
        in_specs=in_specs,
        out_specs=[pl.BlockSpec((1,) + x.shape[1:], lambda i, f: (i, 0, 0)) for x in xes],
        out_shape=[jax.ShapeDtypeStruct(x.shape, BF16) for x in xes],
        scratch_shapes=[pltpu.VMEM(x.shape[1:], F32) for x in xes],
        compiler_params=_cparams(2),
        name="expert_ffn",
    )(*xes, *gss, w_gate, w_up, w_down)
    return list(out)


def _combine_kernel(*refs, cap, final):
    if final:
        post_ref, yg_ref, x_ref, mod_ref, fw_ref, o_ref = refs
    else:
        post_ref, yg_ref, x_ref, mod_ref, o_ref = refs
    d = D_MODEL
    post = post_ref[0]
    tn = post.shape[0]
    slot = lax.broadcasted_iota(jnp.int32, (tn, cap), 1).astype(F32)
    acc = jnp.zeros((tn, d), F32)
    for e in range(N_EXPERTS):
        onehot = jnp.where(post[:, e:e + 1] == slot, 1.0, 0.0).astype(BF16)
        acc = acc + _dot(onehot, yg_ref[e])
    x2 = x_ref[0] + mod_ref[0][:, 5 * d:6 * d] * acc
    if final:
        ms = jnp.mean(x2 * x2, axis=-1, keepdims=True)
        o_ref[0] = x2 * lax.rsqrt(ms + EPS) * fw_ref[...]
    else:
        o_ref[0] = x2


def _combine(post_b, yg, xs, mod, cap, row0, n, mod_row_ctx, final_w=None):
    b, ts, d = xs.shape
    tn = min(TM, n)
    blk0 = row0 // tn
    final = final_w is not None
    in_specs = [pl.BlockSpec((1, tn, LANES), lambda i, j: (i, j, 0)),
                pl.BlockSpec((N_EXPERTS, cap, d), lambda i, j: (0, i, 0)),
                pl.BlockSpec((1, tn, d), lambda i, j: (i, j + blk0, 0)),
                pl.BlockSpec((1, 1, mod.shape[-1]), lambda i, j: (8 if mod_row_ctx else i, 0, 0))]
    args = [post_b, yg, xs, mod]
    if final:
        in_specs.append(pl.BlockSpec(final_w.shape, lambda i, j: (0, 0)))
        args.append(final_w)
        out_spec = pl.BlockSpec((1, tn, d), lambda i, j: (i, j, 0))
        out_shape = jax.ShapeDtypeStruct((b, n, d), F32)
        aliases = {}
    else:
        out_spec = pl.BlockSpec((1, tn, d), lambda i, j: (i, j + blk0, 0))
        out_shape = jax.ShapeDtypeStruct((b, ts, d), F32)
        aliases = {2: 0}
    return pl.pallas_call(
        functools.partial(_combine_kernel, cap=cap, final=final),
        grid=(b, n // tn),
        in_specs=in_specs,
        out_specs=out_spec,
        out_shape=out_shape,
        input_output_aliases=aliases,
        compiler_params=_cparams(2),
        name="expert_combine_final" if final else "expert_combine",
    )(*args)


def _rope_tables(n_lat, lc):
    n_rows = n_lat // GRID_W
    rows = jnp.repeat(jnp.arange(n_rows, dtype=F32), GRID_W)
    cols = jnp.tile(jnp.arange(GRID_W, dtype=F32), n_rows)
    n_freq = A_HEAD_DIM // 4
    inv_freq = ROPE_THETA ** (-jnp.arange(n_freq, dtype=F32) / n_freq)
    ang = jnp.concatenate([rows[:, None] * inv_freq, cols[:, None] * inv_freq], axis=-1)
    c, s = jnp.cos(ang), jnp.sin(ang)
    cos = jnp.concatenate([c, c, c, c], axis=-1)
    sin = jnp.concatenate([-s, s, -s, s], axis=-1)
    cos = jnp.concatenate([jnp.ones((lc, LANES), F32), cos], axis=0)
    sin = jnp.concatenate([jnp.zeros((lc, LANES), F32), sin], axis=0)
    return cos, sin


def _route_layout(aff, row0, n, b):
    a = aff[:, row0:row0 + n, :N_EXPERTS]
    a = jnp.transpose(a, (1, 0, 2)).reshape(n, b * N_EXPERTS)
    return jnp.pad(a, ((0, 0), (0, LANES - b * N_EXPERTS)))


def _sample_layout(post, n, b):
    p = post[:, :b * N_EXPERTS].reshape(n, b, N_EXPERTS)
    p = jnp.transpose(p, (1, 0, 2))
    return jnp.pad(p, ((0, 0), (0, 0), (0, LANES - N_EXPERTS)), constant_values=-1.0)


def kernel(x, c, ctx, c_ctx, w_ada, b_ada, norm1_w, norm2_w, w_in, mlstm_conv_w, mlstm_gate_b, mlstm_norm_w,
           diff_lambda, diff_subln_w, gqa_qnorm_w, gqa_knorm_w, w_branch_a, w_branch_b, w_branch_c, w_out,
           w_router, w_exp_gate, w_exp_up, w_exp_down, final_norm_w):
    b, n_lat, d = x.shape
    lc = ctx.shape[1]
    depth = w_ada.shape[0]
    assert d == D_MODEL and b * N_EXPERTS <= LANES and b <= 8
    assert lc % TM == 0 and n_lat % TM == 0 and lc % TQ == 0

    xs = jnp.concatenate([ctx, x], axis=1)
    cvec = jnp.zeros((16, d), F32).at[:b].set(c).at[8].set(c_ctx)
    mods = _ada(cvec, w_ada, b_ada)
    cos, sin = _rope_tables(n_lat, lc)
    gidx = jnp.arange(LANES) // B_HEAD_DIM
    gmat = jnp.where(gidx[:, None] == gidx[None, :], 1.0 / B_HEAD_DIM, 0.0).astype(BF16)
    cap_lat = EC_CAPACITY_FACTOR * n_lat // N_EXPERTS
    cap_ctx = EC_CAPACITY_FACTOR * lc // N_EXPERTS
    out = None

    for layer in range(depth):
        with_ctx = layer < depth - 1
        mod = mods[layer].reshape(16, 1, 6 * d)
        wl = w_in[layer]
        bq0 = 3 * MIX_W
        w_bq = wl[:, bq0:bq0 + MIX_W].reshape(d, B_KV_HEADS, B_GROUP, B_HEAD_DIM).transpose(0, 2, 1, 3).reshape(d, MIX_W)
        w_main = jnp.concatenate([wl[:, :bq0], w_bq, wl[:, bq0 + MIX_W:MAIN_COLS]], axis=1).astype(BF16)
        w_bb = w_branch_b[layer].reshape(B_KV_HEADS, B_GROUP, B_HEAD_DIM, d).transpose(1, 0, 2, 3).reshape(MIX_W, d)
        w_gates = wl[:, GATE_COL0:GATE_COL0 + N_GATES]
        wg = jnp.pad(w_gates, ((0, 0), (0, LANES - N_GATES))).astype(BF16)
        wgt = w_gates.T.astype(BF16)
        gb = jnp.pad(mlstm_gate_b[layer], (0, LANES - N_GATES)).reshape(1, LANES)
        gbt = mlstm_gate_b[layer].reshape(N_GATES, 1)
        n1 = norm1_w[layer].reshape(1, d)
        n2 = norm2_w[layer].reshape(1, d)
        qnw = jnp.tile(gqa_qnorm_w[layer], LANES // B_HEAD_DIM).reshape(1, LANES)
        knw = jnp.tile(gqa_knorm_w[layer], LANES // B_HEAD_DIM).reshape(1, LANES)

        qa, ka, va, qb, kb, vb, qkc, vc, oc, g, gt = _inproj(
            xs, mod, n1, w_main, wg, wgt, gb, gbt, cos, sin, qnw, knw, gmat, lc // TM)

        lam_init = 0.8 - 0.6 * math.exp(-0.3 * layer)
        oa = _diff_attn(diff_lambda[layer], diff_subln_w[layer].reshape(1, 2 * A_HEAD_DIM),
                        qa, ka, va, lam_init, lc, with_ctx)
        ob = _gqa_attn(qb, kb, vb, lc, with_ctx)
        ocm = _mlstm(qkc, vc, oc, g, gt, mlstm_conv_w[layer], mlstm_norm_w[layer], lc)

        wm = wl[:, MERGE_COL0:].astype(BF16)
        wr = jnp.pad(w_router[layer], ((0, 0), (0, LANES - N_EXPERTS)))
        xs, h2, aff = _merge(xs, mod, n1, n2, oa, ob, ocm, wm,
                             w_branch_a[layer].astype(BF16), w_bb.astype(BF16),
                             w_branch_c[layer].astype(BF16), w_out[layer].astype(BF16), wr, lc, with_ctx)

        groups = [(lc, n_lat, cap_lat)]
        if with_ctx:
            groups.append((0, lc, cap_ctx))
        xes, gss, posts = [], [], []
        for row0, n, cap in groups:
            post, posr, affr = _route(_route_layout(aff, row0, n, b), cap)
            xe, gs = _gather(posr, affr, h2, cap, row0, n, b)
            xes.append(xe)
            gss.append(gs)
            posts.append(_sample_layout(post, n, b))
        ygs = _ffn(xes, gss, w_exp_gate, w_exp_up, w_exp_down, layer)
        for gi, (row0, n, cap) in enumerate(groups):
            is_last = (layer == depth - 1) and gi == 0
            res = _combine(posts[gi], ygs[gi], xs, mod, cap, row0, n, mod_row_ctx=(row0 == 0),
                           final_w=final_norm_w.reshape(1, d) if is_last else None)
            if is_last:
                out = res
            else:
                xs = res
    return out
```

```python
import functools
import math

import jax
import jax.numpy as jnp
from jax import lax
from jax.experimental import pallas as pl
from jax.experimental.pallas import tpu as pltpu

F32 = jnp.float32
BF16 = jnp.bfloat16

D_MODEL = 1024
DEPTH = 2
GRID_W = 64
ROPE_THETA = 10000.0
EPS = 1e-6
NEG_BIG = -1e30
MIX_W = D_MODEL // 2
A_HEAD_DIM = 64
A_HEADS = MIX_W // (2 * A_HEAD_DIM)
B_HEAD_DIM = 64
B_Q_HEADS = MIX_W // B_HEAD_DIM
B_KV_HEADS = 2
B_GROUP = B_Q_HEADS // B_KV_HEADS
C_HEAD_DIM = 128
C_HEADS = MIX_W // C_HEAD_DIM
C_CONV = 3
C_CHUNK = 128
N_BRANCH = 3
N_EXPERTS = 16
EXPERT_FF = 2 * D_MODEL
EC_CAPACITY_FACTOR = 2

LANES = 128
KV_B = B_KV_HEADS * B_HEAD_DIM
N_GATES = 4 * C_HEADS
MAIN_COLS = 8 * MIX_W + 2 * KV_B
GATE_COL0 = MAIN_COLS
MERGE_COL0 = MAIN_COLS + N_GATES
TM = 512
SEG = 256
TQ = 256
FF_TILE = 512
FFN_ROWS = 512
GATHER_EXPERTS = 4
VMEM_LIMIT = 56 * 1024 * 1024
SCORE_SCALE = (A_HEAD_DIM ** -0.5) * math.log2(math.e)


def _cparams(n_axes, vmem=VMEM_LIMIT):
    return pltpu.CompilerParams(dimension_semantics=("arbitrary",) * n_axes, vmem_limit_bytes=vmem)


def _dot(a, b):
    return jnp.dot(a, b, preferred_element_type=F32)


def _dot_nt(a, b):
    return lax.dot_general(a, b, (((1,), (1,)), ((), ())), preferred_element_type=F32)


def _split3(x):
    a = x.astype(BF16)
    r = x - a.astype(F32)
    b = r.astype(BF16)
    c = (r - b.astype(F32)).astype(BF16)
    return a, b, c


def _sigmoid(x):
    return 1.0 / (1.0 + jnp.exp(-x))


def _silu(x):
    return x * _sigmoid(x)


def _log_sigmoid(x):
    return jnp.minimum(x, 0.0) - jnp.log(1.0 + jnp.exp(-jnp.abs(x)))


def _norm_mod(x, nw, shift, scale):
    ms = jnp.mean(x * x, axis=-1, keepdims=True)
    return (x * lax.rsqrt(ms + EPS) * nw) * (1.0 + scale) + shift


def _ada_kernel(c_ref, w_ref, b_ref, o_ref):
    s = _silu(c_ref[...])
    s1, s2, _ = _split3(s)
    w = w_ref[0]
    w1, w2, _ = _split3(w)
    o_ref[0] = _dot(s1, w1) + _dot(s1, w2) + _dot(s2, w1) + b_ref[0]


def _ada(cvec, w_ada, b_ada):
    depth, d, n = w_ada.shape
    tn = 1536
    return pl.pallas_call(
        _ada_kernel,
        grid=(depth, n // tn),
        in_specs=[pl.BlockSpec((16, d), lambda l, j: (0, 0)),
                  pl.BlockSpec((1, d, tn), lambda l, j: (l, 0, j)),
                  pl.BlockSpec((1, 1, tn), lambda l, j: (l, 0, j))],
        out_specs=pl.BlockSpec((1, 16, tn), lambda l, j: (l, 0, j)),
        out_shape=jax.ShapeDtypeStruct((depth, 16, n), F32),
        compiler_params=_cparams(2),
        name="ada_mod",
    )(cvec, w_ada, b_ada.reshape(depth, 1, n))


def _inproj_kernel(x_ref, mod0_ref, mod1_ref, nw_ref, w_ref, wg_ref, wgt_ref, gb_ref, gbt_ref,
                   cos0_ref, cos1_ref, sin0_ref, sin1_ref, qnw_ref, knw_ref, gm_ref,
                   qa_ref, ka_ref, va_ref, qb_ref, kb_ref, vb_ref, qkc_ref, vc_ref, oc_ref, g_ref, gt_ref):
    d = D_MODEL
    tm = x_ref.shape[0]
    hs = []
    for half, mod_ref in enumerate((mod0_ref, mod1_ref)):
        mod = mod_ref[0]
        x = x_ref[half * SEG:(half + 1) * SEG, :]
        hs.append(_norm_mod(x, nw_ref[...], mod[:, 0:d], mod[:, d:2 * d]).astype(BF16))
    h = jnp.concatenate(hs, axis=0)
    cos = jnp.concatenate([cos0_ref[...], cos1_ref[...]], axis=0)
    sin = jnp.concatenate([sin0_ref[...], sin1_ref[...]], axis=0)
    lane = lax.broadcasted_iota(jnp.int32, (tm, LANES), 1)
    gm = gm_ref[...]

    def rope(p):
        return p * cos + pltpu.roll(p, LANES // 2, 1) * sin

    def head_norm(p, w):
        sq = p * p
        hi = sq.astype(BF16)
        lo = (sq - hi.astype(F32)).astype(BF16)
        ms = _dot(hi, gm) + _dot(lo, gm)
        return p * lax.rsqrt(ms + EPS) * w

    def proj(c0):
        return _dot(h, w_ref[:, c0:c0 + 2 * LANES])

    def halves(p):
        return p[:, :LANES], p[:, LANES:]

    for j in range(2):
        for half, p in enumerate(halves(proj(j * 256))):
            c = j * 256 + half * LANES
            qa_ref[:, c:c + LANES] = (rope(p) * SCORE_SCALE).astype(BF16)
    for j in range(2):
        for half, p in enumerate(halves(proj(512 + j * 256))):
            c = j * 256 + half * LANES
            ka_ref[:, c:c + LANES] = rope(p).astype(BF16)
    for j in range(2):
        va_ref[:, j * 256:(j + 1) * 256] = proj(1024 + j * 256).astype(BF16)
    qnw = qnw_ref[...]
    for j in range(2):
        for half, p in enumerate(halves(proj(1536 + j * 256))):
            c = j * 256 + half * LANES
            qb_ref[:, c:c + LANES] = (rope(head_norm(p, qnw)) * SCORE_SCALE).astype(BF16)
    pk, pv = halves(proj(2048))
    kb_ref[...] = rope(head_norm(pk, knw_ref[...])).astype(BF16)
    vb_ref[...] = pv.astype(BF16)
    for j in range(4):
        qkc_ref[:, j * 256:(j + 1) * 256] = proj(2304 + j * 256)
    for j in range(2):
        vc_ref[:, j * 256:(j + 1) * 256] = proj(3328 + j * 256).astype(BF16)
    for j in range(2):
        oc_ref[:, j * 256:(j + 1) * 256] = _sigmoid(proj(3840 + j * 256)).astype(BF16)
    g = _dot(h, wg_ref[...]) + gb_ref[...]
    is_f = ((lane % 8) >= 4) & (lane < N_GATES)
    g_ref[...] = jnp.where(is_f, _log_sigmoid(g), g)
    gt = _dot_nt(wgt_ref[...], h) + gbt_ref[...]
    row = lax.broadcasted_iota(jnp.int32, (N_GATES, tm), 0)
    gt_ref[...] = jnp.where((row % 8) >= 4, _log_sigmoid(gt), gt)


def _seg_rows(k, half, segs_per_sample, n_ctx_segs):
    seg = 2 * k + half
    sample = seg // segs_per_sample
    within = seg % segs_per_sample
    return jnp.where(within < n_ctx_segs, 8, sample), within


def _mod_specs(width, segs_per_sample, n_ctx_segs):
    return [pl.BlockSpec((1, 1, width), lambda k, h=half: (_seg_rows(k, h, segs_per_sample, n_ctx_segs)[0], 0, 0))
            for half in range(2)]


def _mod_spec_one_seg(width, segs_per_sample, n_ctx_segs):
    return pl.BlockSpec((1, 1, width), lambda k: (jnp.where(k % segs_per_sample < n_ctx_segs, 8, k // segs_per_sample), 0, 0))


def _inproj(x2, mod, nw, w_main, wg, wgt, gb, gbt, cos, sin, qnw, knw, gmat, ts, lc):
    rows, d = x2.shape
    sps = ts // SEG
    ncs = lc // SEG
    tok = lambda width: pl.BlockSpec((TM, width), lambda k: (k, 0))
    full = lambda a: pl.BlockSpec(a.shape, lambda k: (0,) * a.ndim)
    table = [pl.BlockSpec((SEG, LANES), lambda k, h=half: (_seg_rows(k, h, sps, ncs)[1], 0)) for half in range(2)]
    outs = [(MIX_W, BF16), (MIX_W, BF16), (MIX_W, BF16), (MIX_W, BF16), (KV_B, BF16), (KV_B, BF16),
            (2 * MIX_W, F32), (MIX_W, BF16), (MIX_W, BF16), (LANES, F32)]
    out_shape = [jax.ShapeDtypeStruct((rows, w), dt) for w, dt in outs]
    out_specs = [tok(w) for w, _ in outs]
    out_shape.append(jax.ShapeDtypeStruct((N_GATES, rows), F32))
    out_specs.append(pl.BlockSpec((N_GATES, TM), lambda k: (0, k)))
    return pl.pallas_call(
        _inproj_kernel,
        grid=(rows // TM,),
        in_specs=[tok(d)] + _mod_specs(mod.shape[-1], sps, ncs)
        + [full(nw), full(w_main), full(wg), full(wgt), full(gb), full(gbt)]
        + table + table + [full(qnw), full(knw), full(gmat)],
        out_specs=out_specs,
        out_shape=out_shape,
        compiler_params=_cparams(1),
        name="in_proj",
    )(x2, mod, mod, nw, w_main, wg, wgt, gb, gbt, cos, cos, sin, sin, qnw, knw, gmat)


def _first_head_lanes(shape):
    lane = lax.broadcasted_iota(jnp.int32, shape, 1)
    return (lane // (LANES // 4)) % 2 == 0


def _softmax_numerators(sb, eb, rows, lk):
    maxes = [jnp.max(sb[r:r + 8, :lk], axis=-1, keepdims=True) for r in range(0, rows, 8)]
    for rb in range(rows // 16):
        parts = [jnp.exp2(sb[rb * 16 + sub * 8:rb * 16 + sub * 8 + 8, :lk] - maxes[2 * rb + sub]) for sub in range(2)]
        eb[rb * 16:(rb + 1) * 16, :lk] = jnp.concatenate(parts, axis=0).astype(BF16)


def _with_ones_column(v):
    lane = lax.broadcasted_iota(jnp.int32, v.shape, 1)
    ones = jnp.where(lane == 0, 1.0, 0.0).astype(v.dtype)
    return jnp.concatenate([v, ones], axis=1)


def _attention_units(n_units, scores, finish, values, s_s, e_s, rows, lk):
    scores(0, s_s.at[0])
    for u in range(n_units):
        if u + 1 < n_units:
            scores(u + 1, s_s.at[(u + 1) % 2])
        _softmax_numerators(s_s.at[u % 2], e_s.at[u % 2], rows, lk)
        finish(u, _dot(e_s[u % 2, :, :lk], values(u)))


def _diff_attn_kernel(lam_ref, sub_ref, q_ref, k_ref, v_ref, o_ref, s_s, e_s, *, lam_init, n_ctx_blocks, with_ctx, lc):
    qi = pl.program_id(1)
    lv = lam_ref[...]
    lam = (jnp.exp(jnp.sum(lv[0:1] * lv[1:2], axis=-1, keepdims=True))
           - jnp.exp(jnp.sum(lv[2:3] * lv[3:4], axis=-1, keepdims=True)) + lam_init)
    tq = q_ref.shape[1]
    w = 2 * A_HEAD_DIM

    def body(lk):
        low = _first_head_lanes((tq, w))

        def scores(h, sb):
            cols = slice(h * w, (h + 1) * w)
            q = q_ref[0, :, cols]
            zero = jnp.zeros_like(q)
            qs = jnp.concatenate([jnp.where(low, q, zero), jnp.where(low, zero, q)], axis=0)
            sb[:, :lk] = _dot_nt(qs, k_ref[0, :lk, cols])

        def values(h):
            return _with_ones_column(v_ref[0, :lk, h * w:(h + 1) * w])

        def finish(h, pv):
            o = (pv[:tq, :w] * (1.0 / pv[:tq, w:w + 1])
                 - pv[tq:, :w] * (lam / pv[tq:, w:w + 1]))
            ms = jnp.mean(o * o, axis=-1, keepdims=True)
            o_ref[0, :, h * w:(h + 1) * w] = (o * lax.rsqrt(ms + EPS) * sub_ref[...] * (1.0 - lam_init)).astype(BF16)

        _attention_units(A_HEADS, scores, finish, values, s_s, e_s, 2 * tq, lk)

    @pl.when(qi < n_ctx_blocks)
    def _():
        if with_ctx:
            body(lc)
        else:
            o_ref[...] = jnp.zeros_like(o_ref)

    @pl.when(qi >= n_ctx_blocks)
    def _():
        body(k_ref.shape[1])


def _diff_attn(lam_vecs, sub_w, qa, ka, va, lam_init, lc, with_ctx):
    b, ts, _ = qa.shape
    n_ctx_blocks = lc // TQ
    nq = ts // TQ
    kern = functools.partial(_diff_attn_kernel, lam_init=lam_init, n_ctx_blocks=n_ctx_blocks, with_ctx=with_ctx, lc=lc)
    return pl.pallas_call(
        kern,
        grid=(b, nq),
        in_specs=[pl.BlockSpec(lam_vecs.shape, lambda i, j: (0, 0)),
                  pl.BlockSpec(sub_w.shape, lambda i, j: (0, 0)),
                  pl.BlockSpec((1, TQ, MIX_W), lambda i, j: (i, j, 0)),
                  pl.BlockSpec((1, ts, MIX_W), lambda i, j: (i, 0, 0)),
                  pl.BlockSpec((1, ts, MIX_W), lambda i, j: (i, 0, 0))],
        out_specs=pl.BlockSpec((1, TQ, MIX_W), lambda i, j: (i, j, 0)),
        out_shape=jax.ShapeDtypeStruct((b, ts, MIX_W), BF16),
        scratch_shapes=[pltpu.VMEM((2, 2 * TQ, ts), F32), pltpu.VMEM((2, 2 * TQ, ts), BF16)],
        compiler_params=_cparams(2),
        name="diff_attn",
    )(lam_vecs, sub_w, qa, ka, va)


def _gqa_kernel(q_ref, k_ref, v_ref, o_ref, s_s, e_s, *, n_ctx_blocks, with_ctx, lc):
    qi = pl.program_id(1)
    tq = q_ref.shape[1]

    def body(lk):
        low = _first_head_lanes((tq, LANES))
        out_low = lax.broadcasted_iota(jnp.int32, (tq, LANES), 1) < B_HEAD_DIM
        k = k_ref[0, :lk, :]
        v1 = _with_ones_column(v_ref[0, :lk, :])
        outs = {}
        pairs = B_GROUP // 2

        def scores(u, sb):
            g, pair = divmod(u, pairs)
            parts = []
            for j in (2 * pair, 2 * pair + 1):
                x = q_ref[0, :, j * LANES:(j + 1) * LANES]
                zero = jnp.zeros_like(x)
                parts.append(jnp.where(low, x, zero) if g == 0 else jnp.where(low, zero, x))
            sb[:, :lk] = _dot_nt(jnp.concatenate(parts, axis=0), k)

        def finish(u, pv):
            g, pair = divmod(u, pairs)
            o = pv[:, :LANES] * (1.0 / pv[:, LANES:LANES + 1])
            outs[(g, 2 * pair)] = o[:tq]
            outs[(g, 2 * pair + 1)] = o[tq:]

        _attention_units(B_KV_HEADS * pairs, scores, finish, lambda u: v1, s_s, e_s, 2 * tq, lk)
        for j in range(B_GROUP):
            o_ref[0, :, j * LANES:(j + 1) * LANES] = jnp.where(out_low, outs[(0, j)], outs[(1, j)]).astype(BF16)

    @pl.when(qi < n_ctx_blocks)
    def _():
        if with_ctx:
            body(lc)
        else:
            o_ref[...] = jnp.zeros_like(o_ref)

    @pl.when(qi >= n_ctx_blocks)
    def _():
        body(k_ref.shape[1])


def _gqa_attn(qb, kb, vb, lc, with_ctx):
    b, ts, _ = qb.shape
    n_ctx_blocks = lc // TQ
    nq = ts // TQ
    kern = functools.partial(_gqa_kernel, n_ctx_blocks=n_ctx_blocks, with_ctx=with_ctx, lc=lc)
    return pl.pallas_call(
        kern,
        grid=(b, nq),
        in_specs=[pl.BlockSpec((1, TQ, MIX_W), lambda i, j: (i, j, 0)),
                  pl.BlockSpec((1, ts, KV_B), lambda i, j: (i, 0, 0)),
                  pl.BlockSpec((1, ts, KV_B), lambda i, j: (i, 0, 0))],
        out_specs=pl.BlockSpec((1, TQ, MIX_W), lambda i, j: (i, j, 0)),
        out_shape=jax.ShapeDtypeStruct((b, ts, MIX_W), BF16),
        scratch_shapes=[pltpu.VMEM((2, 2 * TQ, ts), F32), pltpu.VMEM((2, 2 * TQ, ts), BF16)],
        compiler_params=_cparams(2),
        name="gqa_attn",
    )(qb, kb, vb)


def _mlstm_kernel(q_ref, k_ref, v_ref, o_ref, g_ref, gt_ref, cwq_ref, cwk_ref, nw_ref, out_ref,
                  q_s, kt_s, bc_s, ac_s, rows_s, hacc_s, st_s, *, lc):
    ts = q_ref.shape[1]
    hp = q_ref.shape[2] // C_HEAD_DIM
    head0 = pl.program_id(1) * hp
    nc = ts // C_CHUNK
    ncc = lc // C_CHUNK
    ch = C_CHUNK

    row = lax.broadcasted_iota(jnp.int32, (ts, LANES), 0)
    prev_ok = (row != 0) & (row != lc)
    next_ok = (row != lc - 1) & (row != ts - 1)

    def conv(x, w):
        xp = jnp.where(prev_ok, pltpu.roll(x, 1, 0), 0.0)
        xn = jnp.where(next_ok, pltpu.roll(x, ts - 1, 0), 0.0)
        return _silu(xp * w[0:1] + x * w[1:2] + xn * w[2:3])

    for j in range(hp):
        cols = slice(j * LANES, (j + 1) * LANES)
        q_s[:, cols] = conv(q_ref[0, :, cols], cwq_ref[0, :, cols]).astype(BF16)
        y = conv(k_ref[0, :, cols], cwk_ref[0, :, cols]) * (C_HEAD_DIM ** -0.5)
        for c in range(nc):
            kt_s[c, cols, :] = y[c * ch:(c + 1) * ch, :].T.astype(BF16)

    ri = lax.broadcasted_iota(jnp.int32, (ch, ch), 0)
    ci = lax.broadcasted_iota(jnp.int32, (ch, ch), 1)
    lower = jnp.where(ci <= ri, 1.0, 0.0).astype(BF16)
    upper = jnp.where(ci >= ri, 1.0, 0.0).astype(BF16)
    lane = ci
    rowi = lax.broadcasted_iota(jnp.int32, (N_GATES, ch), 0)
    for c in range(nc):
        rs = slice(c * ch, (c + 1) * ch)
        g = g_ref[0, rs, :]
        g1, g2, g3 = _split3(g)
        pre = _dot(lower, g1) + _dot(lower, g2) + _dot(lower, g3)
        suf = _dot(upper, g1) + _dot(upper, g2) + _dot(upper, g3)
        gt = gt_ref[:, rs]
        t1, t2, t3 = _split3(gt)
        pre_t = _dot(t1, upper) + _dot(t2, upper) + _dot(t3, upper)
        suf_t = _dot(t1, lower) + _dot(t2, lower) + _dot(t3, lower)
        for j in range(hp):
            for direction in range(2):
                idx = j * 2 + direction
                li = head0 + j + 8 * direction
                lf = li + 4
                cum, cum_t = (pre, pre_t) if direction == 0 else (suf, suf_t)
                b_col = jnp.sum(jnp.where(lane == lf, cum, 0.0), axis=-1, keepdims=True)
                i_col = jnp.sum(jnp.where(lane == li, g, 0.0), axis=-1, keepdims=True)
                bc_s[idx, rs, :] = jnp.broadcast_to(b_col, (ch, LANES))
                ac_s[idx, rs, :] = jnp.broadcast_to(i_col - b_col, (ch, LANES))
                b_row = jnp.sum(jnp.where(rowi == lf, cum_t, 0.0), axis=0, keepdims=True)
                i_row = jnp.sum(jnp.where(rowi == li, gt, 0.0), axis=0, keepdims=True)
                rows_s[c, 2 * idx:2 * idx + 1, :] = b_row
                rows_s[c, 2 * idx + 1:2 * idx + 2, :] = i_row - b_row

    hacc_s[...] = jnp.zeros_like(hacc_s)
    st_s[...] = jnp.zeros_like(st_s)
    tri_f = ci <= ri
    tri_b = ci >= ri

    def chain(c, j, direction, m):
        idx = j * 2 + direction
        c0 = pl.multiple_of(c * ch, ch)
        cols = slice(j * LANES, (j + 1) * LANES)
        q = q_s[pl.ds(c0, ch), cols]
        kt = kt_s[c, cols, :]
        v = v_ref[0, pl.ds(c0, ch), cols]
        bc = bc_s[idx, pl.ds(c0, ch), :]
        ac = ac_s[idx, pl.ds(c0, ch), :]
        rows = rows_s[c]
        b_row = rows[2 * idx:2 * idx + 1, :]
        ib_row = rows[2 * idx + 1:2 * idx + 2, :]
        tri = tri_f if direction == 0 else tri_b
        log_d = jnp.where(tri, bc + ib_row, NEG_BIG)
        m_intra = jnp.max(log_d, axis=-1, keepdims=True)
        log_inter = bc + m
        m_t = jnp.maximum(log_inter, m_intra)
        dm = jnp.exp(log_d - m_t)
        w_inter = jnp.exp(log_inter - m_t)
        s = _dot(q, kt) * dm
        st = st_s[idx]
        inter = _dot(q, st.astype(BF16))
        num = _dot(s.astype(BF16), v) + w_inter * inter[:, :LANES]
        den = jnp.sum(s, axis=-1, keepdims=True) + w_inter * inter[:, LANES:LANES + 1]
        hout = num / jnp.maximum(jnp.abs(den), jnp.exp(-m_t))
        hacc_s[pl.ds(c0, ch), cols] = hacc_s[pl.ds(c0, ch), cols] + hout
        total = b_row[:, ch - 1:ch] if direction == 0 else b_row[:, 0:1]
        m_new = jnp.maximum(total + m, jnp.max(total + ib_row, axis=-1, keepdims=True))
        w = jnp.exp(total + ac - m_new)
        decay = jnp.exp(total + m - m_new)
        wv = jnp.concatenate([w * v.astype(F32), jnp.where(lane == 0, w, 0.0)], axis=1).astype(BF16)
        st_s[idx] = decay * st + _dot(kt, wv)
        return m_new

    def step(i, ms):
        c_f = i
        c_b = jnp.where(i < ncc, ncc - 1 - i, nc + ncc - 1 - i)
        out = []
        for j in range(hp):
            out.append(chain(c_f, j, 0, ms[j * 2]))
            out.append(chain(c_b, j, 1, ms[j * 2 + 1]))
        return tuple(out)

    lax.fori_loop(0, nc, step, tuple(jnp.zeros((1, 1), F32) for _ in range(2 * hp)), unroll=2)

    for j in range(hp):
        cols = slice(j * LANES, (j + 1) * LANES)
        x = hacc_s[:, cols]
        ms = jnp.mean(x * x, axis=-1, keepdims=True)
        y = x * lax.rsqrt(ms + EPS) * nw_ref[0, :, cols]
        out_ref[0, :, cols] = (o_ref[0, :, cols].astype(F32) * y).astype(BF16)


def _mlstm(qkc, vc, oc, g, gt, conv_w, norm_w, lc, heads_per_step=2):
    b, ts, _ = vc.shape
    wq = heads_per_step * C_HEAD_DIM
    nhp = C_HEADS // heads_per_step
    nc = ts // C_CHUNK
    cw = conv_w.reshape(C_CONV, 2 * nhp, wq).transpose(1, 0, 2)
    nw = norm_w.reshape(1, nhp, wq).transpose(1, 0, 2)
    kern = functools.partial(_mlstm_kernel, lc=lc)
    tokw = lambda off: pl.BlockSpec((1, ts, wq), lambda i, p: (i, 0, p + off))
    return pl.pallas_call(
        kern,
        grid=(b, nhp),
        in_specs=[tokw(0), tokw(nhp), tokw(0), tokw(0),
                  pl.BlockSpec((1, ts, LANES), lambda i, p: (i, 0, 0)),
                  pl.BlockSpec((N_GATES, ts), lambda i, p: (0, i)),
                  pl.BlockSpec((1, C_CONV, wq), lambda i, p: (p, 0, 0)),
                  pl.BlockSpec((1, C_CONV, wq), lambda i, p: (p + nhp, 0, 0)),
                  pl.BlockSpec((1, 1, wq), lambda i, p: (p, 0, 0))],
        out_specs=tokw(0),
        out_shape=jax.ShapeDtypeStruct((b, ts, MIX_W), BF16),
        scratch_shapes=[pltpu.VMEM((ts, wq), BF16),
                        pltpu.VMEM((nc, wq, C_CHUNK), BF16),
                        pltpu.VMEM((2 * heads_per_step, ts, LANES), F32),
                        pltpu.VMEM((2 * heads_per_step, ts, LANES), F32),
                        pltpu.VMEM((nc, 4 * heads_per_step, C_CHUNK), F32),
                        pltpu.VMEM((ts, wq), F32),
                        pltpu.VMEM((2 * heads_per_step, C_HEAD_DIM, 2 * LANES), F32)],
        compiler_params=_cparams(2),
        name="mlstm",
    )(qkc, qkc, vc, oc, g, gt, cw, cw, nw)


def _merge_kernel(x_ref, mod_ref, n1_ref, n2_ref, oa_ref, ob_ref, oc_ref, wm_ref, wa_ref, wb_ref, wc_ref,
                  wo_ref, wr_ref, xo_ref, h2_ref, aff_ref):
    d = D_MODEL
    mod = mod_ref[0]
    x = x_ref[...]
    h = _norm_mod(x, n1_ref[...], mod[:, 0:d], mod[:, d:2 * d]).astype(BF16)
    merged = (_sigmoid(_dot(h, wm_ref[:, 0:d])) * _dot(oa_ref[...], wa_ref[...])
              + _sigmoid(_dot(h, wm_ref[:, d:2 * d])) * _dot(ob_ref[...], wb_ref[...])
              + _sigmoid(_dot(h, wm_ref[:, 2 * d:3 * d])) * _dot(oc_ref[...], wc_ref[...]))
    y = _dot(merged.astype(BF16), wo_ref[...])
    x1 = x + mod[:, 2 * d:3 * d] * y
    xo_ref[...] = x1
    h2 = _norm_mod(x1, n2_ref[...], mod[:, 3 * d:4 * d], mod[:, 4 * d:5 * d])
    h2b = h2.astype(BF16)
    h2_ref[...] = h2b
    h2l = (h2 - h2b.astype(F32)).astype(BF16)
    wr = wr_ref[...]
    wrh = wr.astype(BF16)
    wrl = (wr - wrh.astype(F32)).astype(BF16)
    logits = _dot(h2b, wrh) + _dot(h2b, wrl) + _dot(h2l, wrh)
    lane = lax.broadcasted_iota(jnp.int32, logits.shape, 1)
    valid = lane < N_EXPERTS
    logits = jnp.where(valid, logits, NEG_BIG)
    e = jnp.where(valid, jnp.exp(logits - jnp.max(logits, axis=-1, keepdims=True)), 0.0)
    aff_ref[...] = e / jnp.sum(e, axis=-1, keepdims=True)


def _merge(x2, mod, n1, n2, oa, ob, oc, wm, wa, wb, wc, wo, wr, ts, lc):
    rows, d = x2.shape
    tok = lambda width: pl.BlockSpec((SEG, width), lambda k: (k, 0))
    full = lambda a: pl.BlockSpec(a.shape, lambda k: (0,) * a.ndim)
    return pl.pallas_call(
        _merge_kernel,
        grid=(rows // SEG,),
        in_specs=[tok(d), _mod_spec_one_seg(mod.shape[-1], ts // SEG, lc // SEG),
                  full(n1), full(n2), tok(MIX_W), tok(MIX_W), tok(MIX_W),
                  full(wm), full(wa), full(wb), full(wc), full(wo), full(wr)],
        out_specs=[tok(d), tok(d), tok(LANES)],
        out_shape=[jax.ShapeDtypeStruct((rows, d), F32),
                   jax.ShapeDtypeStruct((rows, d), BF16),
                   jax.ShapeDtypeStruct((rows, LANES), F32)],
        input_output_aliases={0: 0},
        compiler_params=_cparams(1),
        name="merge_out",
    )(x2, mod, n1, n2, oa, ob, oc, wm, wa, wb, wc, wo, wr)


def _route_kernel(aff_ref, post_ref, posr_ref, affr_ref, *, cap):
    n = aff_ref.shape[0]
    ch = LANES
    aff = aff_ref[...]

    def step(i, thr_bits):
        cand = thr_bits | jnp.left_shift(jnp.int32(1), 30 - i)
        cnt = jnp.sum((aff >= pltpu.bitcast(cand, F32)).astype(jnp.int32), axis=0, keepdims=True)
        return jnp.where(cnt >= cap, cand, thr_bits)

    thr = pltpu.bitcast(lax.fori_loop(0, 31, step, jnp.zeros((1, LANES), jnp.int32)), F32)
    gt = aff > thr
    eq = aff == thr
    need = cap - jnp.sum(gt.astype(jnp.int32), axis=0, keepdims=True)

    ri = lax.broadcasted_iota(jnp.int32, (ch, ch), 0)
    ci = lax.broadcasted_iota(jnp.int32, (ch, ch), 1)
    strict_lower = jnp.where(ci < ri, 1.0, 0.0).astype(BF16)

    def excl_cumsum(mask_f):
        carry = jnp.zeros((1, LANES), F32)
        blocks = []
        for c in range(n // ch):
            blk = mask_f[c * ch:(c + 1) * ch, :]
            blocks.append(_dot(strict_lower, blk.astype(BF16)) + carry)
            carry = carry + jnp.sum(blk, axis=0, keepdims=True)
        return jnp.concatenate(blocks, axis=0)

    eq_rank = excl_cumsum(jnp.where(eq, 1.0, 0.0))
    sel = gt | (eq & (eq_rank < need.astype(F32)))
    pos = excl_cumsum(jnp.where(sel, 1.0, 0.0))
    post = jnp.where(sel, pos, -1.0)
    post_ref[...] = post
    for c in range(n // ch):
        posr_ref[:, c * ch:(c + 1) * ch] = post[c * ch:(c + 1) * ch, :].T
        affr_ref[:, c * ch:(c + 1) * ch] = aff[c * ch:(c + 1) * ch, :].T


def _route(aff_t, cap):
    n = aff_t.shape[0]
    return pl.pallas_call(
        functools.partial(_route_kernel, cap=cap),
        out_shape=[jax.ShapeDtypeStruct((n, LANES), F32),
                   jax.ShapeDtypeStruct((LANES, n), F32),
                   jax.ShapeDtypeStruct((LANES, n), F32)],
        compiler_params=pltpu.CompilerParams(vmem_limit_bytes=VMEM_LIMIT),
        name="route",
    )(aff_t)


def _gather_kernel(posr_ref, affr_ref, h_ref, xe_ref, gs_ref, *, cap, row0, n):
    ne = posr_ref.shape[0]
    slot = lax.broadcasted_iota(jnp.int32, (cap, n), 0).astype(F32)
    onehots = []
    for t in range(ne):
        pf = jnp.where(posr_ref[t] == slot, 1.0, 0.0)
        gs = jnp.sum(pf * affr_ref[t], axis=-1, keepdims=True)
        gs_ref[t] = jnp.broadcast_to(gs, (cap, LANES))
        onehots.append(pf.astype(BF16))
    xe = _dot(jnp.concatenate(onehots, axis=0), h_ref[0, row0:row0 + n, :])
    for t in range(ne):
        xe_ref[t] = xe[t * cap:(t + 1) * cap].astype(BF16)


def _gather(posr, affr, h2, cap, row0, n, b):
    ts, d = h2.shape[1:]
    be = posr.shape[0]
    posr3 = posr.reshape(be, 1, n)
    affr3 = affr.reshape(be, 1, n)
    e = N_EXPERTS
    ge = GATHER_EXPERTS
    return pl.pallas_call(
        functools.partial(_gather_kernel, cap=cap, row0=row0, n=n),
        grid=(b, e // ge),
        in_specs=[pl.BlockSpec((ge, 1, n), lambda i, j: (i * (e // ge) + j, 0, 0)),
                  pl.BlockSpec((ge, 1, n), lambda i, j: (i * (e // ge) + j, 0, 0)),
                  pl.BlockSpec((1, ts, d), lambda i, j: (i, 0, 0))],
        out_specs=[pl.BlockSpec((ge, cap, d), lambda i, j: (j, i, 0)),
                   pl.BlockSpec((ge, cap, LANES), lambda i, j: (j, i, 0))],
        out_shape=[jax.ShapeDtypeStruct((e, b * cap, d), BF16),
                   jax.ShapeDtypeStruct((e, b * cap, LANES), F32)],
        compiler_params=_cparams(2),
        name="expert_gather",
    )(posr3, affr3, h2)


def _ffn_kernel(*refs, n_groups):
    xe_refs = refs[0:n_groups]
    gs_refs = refs[n_groups:2 * n_groups]
    wg_ref, wu_ref, wd_ref = refs[2 * n_groups:2 * n_groups + 3]
    out_refs = refs[2 * n_groups + 3:3 * n_groups + 3]
    acc_refs = refs[3 * n_groups + 3:4 * n_groups + 3]
    f = pl.program_id(1)
    nf = pl.num_programs(1)
    tiles = lambda ref, n: [ref[0, 0, :, c:c + 2 * LANES].astype(BF16) for c in range(0, n, 2 * LANES)]
    wgb = tiles(wg_ref, wg_ref.shape[3])
    wub = tiles(wu_ref, wu_ref.shape[3])
    wdb = tiles(wd_ref, wd_ref.shape[3])
    for xe_ref, gs_ref, out_ref, acc_ref in zip(xe_refs, gs_refs, out_refs, acc_refs):
        rows = xe_ref.shape[1]
        rb = min(FFN_ROWS, rows)

        @pl.when(f == 0)
        def _():
            acc_ref[...] = jnp.zeros_like(acc_ref)

        for r0 in range(0, rows, rb):
            xb = xe_ref[0, r0:r0 + rb, :]
            a = jnp.concatenate([_dot(xb, w) for w in wgb], axis=1)
            u = jnp.concatenate([_dot(xb, w) for w in wub], axis=1)
            hm = (_silu(a) * u).astype(BF16)
            y = jnp.concatenate([_dot(hm, w) for w in wdb], axis=1)
            acc_ref[r0:r0 + rb, :] = acc_ref[r0:r0 + rb, :] + y

        @pl.when(f == nf - 1)
        def _():
            gate = gs_ref[0][:, 0:1]
            out_ref[0] = (acc_ref[...] * gate).astype(BF16)


def _ffn(xes, gss, w_gate, w_up, w_down, layer):
    n_groups = len(xes)
    _, e, d, ff = w_gate.shape
    nf = ff // FF_TILE
    in_specs = [pl.BlockSpec((1,) + x.shape[1:], lambda i, f: (i, 0, 0)) for x in xes]
    in_specs += [pl.BlockSpec((1,) + g.shape[1:], lambda i, f: (i, 0, 0)) for g in gss]
    in_specs += [pl.BlockSpec((1, 1, d, FF_TILE), lambda i, f: (layer, i, 0, f)),
                 pl.BlockSpec((1, 1, d, FF_TILE), lambda i, f: (layer, i, 0, f)),
                 pl.BlockSpec((1, 1, FF_TILE, d), lambda i, f: (layer, i, f, 0))]
    out = pl.pallas_call(
        functools.partial(_ffn_kernel, n_groups=n_groups),
        grid=(e, nf),
        in_specs=in_specs,
        out_specs=[pl.BlockSpec((1,) + x.shape[1:], lambda i, f: (i, 0, 0)) for x in xes],
        out_shape=[jax.ShapeDtypeStruct(x.shape, BF16) for x in xes],
        scratch_shapes=[pltpu.VMEM(x.shape[1:], F32) for x in xes],
        compiler_params=_cparams(2),
        name="expert_ffn",
    )(*xes, *gss, w_gate, w_up, w_down)
    return list(out)


def _combine_kernel(*refs, cap, final):
    if final:
        post_ref, yg_ref, x_ref, mod_ref, fw_ref, o_ref = refs
    else:
        post_ref, yg_ref, x_ref, mod_ref, o_ref = refs
    d = D_MODEL
    post = post_ref[0]
    tn = post.shape[0]
    slot = lax.broadcasted_iota(jnp.int32, (tn, cap), 1).astype(F32)
    acc = jnp.zeros((tn, d), F32)
    for e in range(N_EXPERTS):
        onehot = jnp.where(post[:, e:e + 1] == slot, 1.0, 0.0).astype(BF16)
        acc = acc + _dot(onehot, yg_ref[e])
    x2 = x_ref[0] + mod_ref[0][:, 5 * d:6 * d] * acc
    if final:
        ms = jnp.mean(x2 * x2, axis=-1, keepdims=True)
        o_ref[0] = x2 * lax.rsqrt(ms + EPS) * fw_ref[...]
    else:
        o_ref[0] = x2


def _combine(post_b, yg, xs, mod, cap, row0, n, mod_row_ctx, final_w=None):
    b, ts, d = xs.shape
    tn = min(SEG, n)
    blk0 = row0 // tn
    final = final_w is not None
    in_specs = [pl.BlockSpec((1, tn, LANES), lambda i, j: (i, j, 0)),
                pl.BlockSpec((N_EXPERTS, cap, d), lambda i, j: (0, i, 0)),
                pl.BlockSpec((1, tn, d), lambda i, j: (i, j + blk0, 0)),
                pl.BlockSpec((1, 1, mod.shape[-1]), lambda i, j: (8 if mod_row_ctx else i, 0, 0))]
    args = [post_b, yg, xs, mod]
    if final:
        in_specs.append(pl.BlockSpec(final_w.shape, lambda i, j: (0, 0)))
        args.append(final_w)
        out_spec = pl.BlockSpec((1, tn, d), lambda i, j: (i, j, 0))
        out_shape = jax.ShapeDtypeStruct((b, n, d), F32)
        aliases = {}
    else:
        out_spec = pl.BlockSpec((1, tn, d), lambda i, j: (i, j + blk0, 0))
        out_shape = jax.ShapeDtypeStruct((b, ts, d), F32)
        aliases = {2: 0}
    return pl.pallas_call(
        functools.partial(_combine_kernel, cap=cap, final=final),
        grid=(b, n // tn),
        in_specs=in_specs,
        out_specs=out_spec,
        out_shape=out_shape,
        input_output_aliases=aliases,
        compiler_params=_cparams(2),
        name="expert_combine_final" if final else "expert_combine",
    )(*args)


def _rope_tables(n_lat, lc):
    n_rows = n_lat // GRID_W
    rows = jnp.repeat(jnp.arange(n_rows, dtype=F32), GRID_W)
    cols = jnp.tile(jnp.arange(GRID_W, dtype=F32), n_rows)
    n_freq = A_HEAD_DIM // 4
    inv_freq = ROPE_THETA ** (-jnp.arange(n_freq, dtype=F32) / n_freq)
    ang = jnp.concatenate([rows[:, None] * inv_freq, cols[:, None] * inv_freq], axis=-1)
    c, s = jnp.cos(ang), jnp.sin(ang)
    cos = jnp.concatenate([c, c, c, c], axis=-1)
    sin = jnp.concatenate([-s, -s, s, s], axis=-1)
    cos = jnp.concatenate([jnp.ones((lc, LANES), F32), cos], axis=0)
    sin = jnp.concatenate([jnp.zeros((lc, LANES), F32), sin], axis=0)
    return cos, sin


def _pair_rope_layout(w):
    d, n = w.shape
    q = LANES // 4
    return w.reshape(d, n // LANES, 2, 2, q).transpose(0, 1, 3, 2, 4).reshape(d, n)


def _route_layout(aff, row0, n, b):
    a = aff[:, row0:row0 + n, :N_EXPERTS]
    a = jnp.transpose(a, (1, 0, 2)).reshape(n, b * N_EXPERTS)
    return jnp.pad(a, ((0, 0), (0, LANES - b * N_EXPERTS)))


def _sample_layout(post, n, b):
    p = post[:, :b * N_EXPERTS].reshape(n, b, N_EXPERTS)
    p = jnp.transpose(p, (1, 0, 2))
    return jnp.pad(p, ((0, 0), (0, 0), (0, LANES - N_EXPERTS)), constant_values=-1.0)


def kernel(x, c, ctx, c_ctx, w_ada, b_ada, norm1_w, norm2_w, w_in, mlstm_conv_w, mlstm_gate_b, mlstm_norm_w,
           diff_lambda, diff_subln_w, gqa_qnorm_w, gqa_knorm_w, w_branch_a, w_branch_b, w_branch_c, w_out,
           w_router, w_exp_gate, w_exp_up, w_exp_down, final_norm_w):
    b, n_lat, d = x.shape
    lc = ctx.shape[1]
    depth = w_ada.shape[0]
    assert d == D_MODEL and b * N_EXPERTS <= LANES and b <= 8
    ts = lc + n_lat
    assert lc % SEG == 0 and n_lat % SEG == 0 and lc % TQ == 0 and (b * ts) % TM == 0

    xs = jnp.concatenate([ctx, x], axis=1)
    cvec = jnp.zeros((16, d), F32).at[:b].set(c).at[8].set(c_ctx)
    mods = _ada(cvec, w_ada, b_ada)
    cos, sin = _rope_tables(n_lat, lc)
    gidx = (jnp.arange(LANES) // (LANES // 4)) % 2
    gmat = jnp.where(gidx[:, None] == gidx[None, :], 1.0 / B_HEAD_DIM, 0.0).astype(BF16)
    cap_lat = EC_CAPACITY_FACTOR * n_lat // N_EXPERTS
    cap_ctx = EC_CAPACITY_FACTOR * lc // N_EXPERTS
    out = None

    for layer in range(depth):
        with_ctx = layer < depth - 1
        mod = mods[layer].reshape(16, 1, 6 * d)
        wl = w_in[layer]
        bq0 = 3 * MIX_W
        w_bq = wl[:, bq0:bq0 + MIX_W].reshape(d, B_KV_HEADS, B_GROUP, B_HEAD_DIM).transpose(0, 2, 1, 3).reshape(d, MIX_W)
        kb0 = bq0 + MIX_W
        w_main = jnp.concatenate([_pair_rope_layout(wl[:, :2 * MIX_W]), wl[:, 2 * MIX_W:bq0], _pair_rope_layout(w_bq),
                                  _pair_rope_layout(wl[:, kb0:kb0 + KV_B]), wl[:, kb0 + KV_B:MAIN_COLS]], axis=1).astype(BF16)
        w_bb = w_branch_b[layer].reshape(B_KV_HEADS, B_GROUP, B_HEAD_DIM, d).transpose(1, 0, 2, 3).reshape(MIX_W, d)
        w_gates = wl[:, GATE_COL0:GATE_COL0 + N_GATES]
        wg = jnp.pad(w_gates, ((0, 0), (0, LANES - N_GATES))).astype(BF16)
        wgt = w_gates.T.astype(BF16)
        gb = jnp.pad(mlstm_gate_b[layer], (0, LANES - N_GATES)).reshape(1, LANES)
        gbt = mlstm_gate_b[layer].reshape(N_GATES, 1)
        n1 = norm1_w[layer].reshape(1, d)
        n2 = norm2_w[layer].reshape(1, d)
        qnw = _pair_rope_layout(jnp.tile(gqa_qnorm_w[layer], LANES // B_HEAD_DIM).reshape(1, LANES))
        knw = _pair_rope_layout(jnp.tile(gqa_knorm_w[layer], LANES // B_HEAD_DIM).reshape(1, LANES))

        flat = _inproj(xs.reshape(b * ts, d), mod, n1, w_main, wg, wgt, gb, gbt, cos, sin, qnw, knw, gmat, ts, lc)
        qa, ka, va, qb, kb, vb, qkc, vc, oc, g = [a.reshape(b, ts, a.shape[-1]) for a in flat[:-1]]
        gt = flat[-1]

        lam_init = 0.8 - 0.6 * math.exp(-0.3 * layer)
        oa = _diff_attn(diff_lambda[layer], diff_subln_w[layer].reshape(1, 2 * A_HEAD_DIM),
                        qa, ka, va, lam_init, lc, with_ctx)
        ob = _gqa_attn(qb, kb, vb, lc, with_ctx)
        ocm = _mlstm(qkc, vc, oc, g, gt, mlstm_conv_w[layer], mlstm_norm_w[layer], lc)

        wm = wl[:, MERGE_COL0:].astype(BF16)
        wr = jnp.pad(w_router[layer], ((0, 0), (0, LANES - N_EXPERTS)))
        rows2 = lambda a: a.reshape(b * ts, a.shape[-1])
        xs, h2, aff = _merge(rows2(xs), mod, n1, n2, rows2(oa), rows2(ob), rows2(ocm), wm,
                             w_branch_a[layer].astype(BF16), w_bb.astype(BF16),
                             w_branch_c[layer].astype(BF16), w_out[layer].astype(BF16), wr, ts, lc)
        xs, h2, aff = [a.reshape(b, ts, a.shape[-1]) for a in (xs, h2, aff)]

        groups = [(lc, n_lat, cap_lat)]
        if with_ctx:
            groups.append((0, lc, cap_ctx))
        xes, gss, posts = [], [], []
        for row0, n, cap in groups:
            post, posr, affr = _route(_route_layout(aff, row0, n, b), cap)
            xe, gs = _gather(posr, affr, h2, cap, row0, n, b)
            xes.append(xe)
            gss.append(gs)
            posts.append(_sample_layout(post, n, b))
        ygs = _ffn(xes, gss, w_exp_gate, w_exp_up, w_exp_down, layer)
        for gi, (row0, n, cap) in enumerate(groups):
            is_last = (layer == depth - 1) and gi == 0
            res = _combine(posts[gi], ygs[gi], xs, mod, cap, row0, n, mod_row_ctx=(row0 == 0),
                           final_w=final_norm_w.reshape(1, d) if is_last else None)
            if is_last:
                out = res
            else:
                xs = res
    return out
```

```python
import functools
import math

import jax
import jax.numpy as jnp
from jax import lax
from jax.experimental import pallas as pl
from jax.experimental.pallas import tpu as pltpu

F32 = jnp.float32
BF16 = jnp.bfloat16

D_MODEL = 1024
DEPTH = 2
GRID_W = 64
ROPE_THETA = 10000.0
EPS = 1e-6
NEG_BIG = -1e30
MIX_W = D_MODEL // 2
A_HEAD_DIM = 64
A_HEADS = MIX_W // (2 * A_HEAD_DIM)
B_HEAD_DIM = 64
B_Q_HEADS = MIX_W // B_HEAD_DIM
B_KV_HEADS = 2
B_GROUP = B_Q_HEADS // B_KV_HEADS
C_HEAD_DIM = 128
C_HEADS = MIX_W // C_HEAD_DIM
C_CONV = 3
C_CHUNK = 128
N_BRANCH = 3
N_EXPERTS = 16
EXPERT_FF = 2 * D_MODEL
EC_CAPACITY_FACTOR = 2

LANES = 128
KV_B = B_KV_HEADS * B_HEAD_DIM
N_GATES = 4 * C_HEADS
MAIN_COLS = 8 * MIX_W + 2 * KV_B
GATE_COL0 = MAIN_COLS
MERGE_COL0 = MAIN_COLS + N_GATES
TM = 512
SEG = 256
TQ = 256
FF_TILE = 512
FFN_ROWS = 512
GATHER_EXPERTS = 4
SMALL_KEYS = 512
VMEM_LIMIT = 56 * 1024 * 1024
SCORE_SCALE = (A_HEAD_DIM ** -0.5) * math.log2(math.e)


def _cparams(n_axes, vmem=VMEM_LIMIT):
    return pltpu.CompilerParams(dimension_semantics=("arbitrary",) * n_axes, vmem_limit_bytes=vmem)


def _dot(a, b):
    return jnp.dot(a, b, preferred_element_type=F32)


def _dot_nt(a, b):
    return lax.dot_general(a, b, (((1,), (1,)), ((), ())), preferred_element_type=F32)


def _split3(x):
    a = x.astype(BF16)
    r = x - a.astype(F32)
    b = r.astype(BF16)
    c = (r - b.astype(F32)).astype(BF16)
    return a, b, c


def _sigmoid(x):
    return 1.0 / (1.0 + jnp.exp(-x))


def _silu(x):
    return x * _sigmoid(x)


def _log_sigmoid(x):
    return jnp.minimum(x, 0.0) - jnp.log(1.0 + jnp.exp(-jnp.abs(x)))


def _norm_mod(x, nw, shift, scale):
    ms = jnp.mean(x * x, axis=-1, keepdims=True)
    return (x * lax.rsqrt(ms + EPS) * nw) * (1.0 + scale) + shift


def _ada_kernel(c_ref, w_ref, b_ref, o_ref):
    s = _silu(c_ref[...])
    s1, s2, _ = _split3(s)
    w = w_ref[0]
    w1, w2, _ = _split3(w)
    o_ref[0] = _dot(s1, w1) + _dot(s1, w2) + _dot(s2, w1) + b_ref[0]


def _ada(cvec, w_ada, b_ada):
    depth, d, n = w_ada.shape
    tn = 1536
    return pl.pallas_call(
        _ada_kernel,
        grid=(depth, n // tn),
        in_specs=[pl.BlockSpec((16, d), lambda l, j: (0, 0)),
                  pl.BlockSpec((1, d, tn), lambda l, j: (l, 0, j)),
                  pl.BlockSpec((1, 1, tn), lambda l, j: (l, 0, j))],
        out_specs=pl.BlockSpec((1, 16, tn), lambda l, j: (l, 0, j)),
        out_shape=jax.ShapeDtypeStruct((depth, 16, n), F32),
        compiler_params=_cparams(2),
        name="ada_mod",
    )(cvec, w_ada, b_ada.reshape(depth, 1, n))


def _inproj_kernel(x_ref, mod0_ref, mod1_ref, nw_ref, w_ref, wg_ref, wgt_ref, gb_ref, gbt_ref,
                   cos0_ref, cos1_ref, sin0_ref, sin1_ref, qnw_ref, knw_ref, gm_ref,
                   qa_ref, ka_ref, va_ref, qb_ref, kb_ref, vb_ref, qkc_ref, vc_ref, oc_ref, g_ref, gt_ref):
    d = D_MODEL
    tm = x_ref.shape[0]
    hs = []
    for half, mod_ref in enumerate((mod0_ref, mod1_ref)):
        mod = mod_ref[0]
        x = x_ref[half * SEG:(half + 1) * SEG, :]
        hs.append(_norm_mod(x, nw_ref[...], mod[:, 0:d], mod[:, d:2 * d]).astype(BF16))
    h = jnp.concatenate(hs, axis=0)
    cos = jnp.concatenate([cos0_ref[...], cos1_ref[...]], axis=0)
    sin = jnp.concatenate([sin0_ref[...], sin1_ref[...]], axis=0)
    lane = lax.broadcasted_iota(jnp.int32, (tm, LANES), 1)
    gm = gm_ref[...]

    def rope(p):
        return p * cos + pltpu.roll(p, LANES // 2, 1) * sin

    def head_norm(p, w):
        sq = p * p
        hi = sq.astype(BF16)
        lo = (sq - hi.astype(F32)).astype(BF16)
        ms = _dot(hi, gm) + _dot(lo, gm)
        return p * lax.rsqrt(ms + EPS) * w

    def proj(c0):
        return _dot(h, w_ref[:, c0:c0 + 2 * LANES])

    def halves(p):
        return p[:, :LANES], p[:, LANES:]

    for j in range(2):
        for half, p in enumerate(halves(proj(j * 256))):
            c = j * 256 + half * LANES
            qa_ref[:, c:c + LANES] = (rope(p) * SCORE_SCALE).astype(BF16)
    for j in range(2):
        for half, p in enumerate(halves(proj(512 + j * 256))):
            c = j * 256 + half * LANES
            ka_ref[:, c:c + LANES] = rope(p).astype(BF16)
    for j in range(2):
        va_ref[:, j * 256:(j + 1) * 256] = proj(1024 + j * 256).astype(BF16)
    qnw = qnw_ref[...]
    for j in range(2):
        for half, p in enumerate(halves(proj(1536 + j * 256))):
            c = j * 256 + half * LANES
            qb_ref[:, c:c + LANES] = (rope(head_norm(p, qnw)) * SCORE_SCALE).astype(BF16)
    pk, pv = halves(proj(2048))
    kb_ref[...] = rope(head_norm(pk, knw_ref[...])).astype(BF16)
    vb_ref[...] = pv.astype(BF16)
    for j in range(4):
        qkc_ref[:, j * 256:(j + 1) * 256] = proj(2304 + j * 256)
    for j in range(2):
        vc_ref[:, j * 256:(j + 1) * 256] = proj(3328 + j * 256).astype(BF16)
    for j in range(2):
        oc_ref[:, j * 256:(j + 1) * 256] = _sigmoid(proj(3840 + j * 256)).astype(BF16)
    g = _dot(h, wg_ref[...]) + gb_ref[...]
    is_f = ((lane % 8) >= 4) & (lane < N_GATES)
    g_ref[...] = jnp.where(is_f, _log_sigmoid(g), g)
    gt = _dot_nt(wgt_ref[...], h) + gbt_ref[...]
    row = lax.broadcasted_iota(jnp.int32, (N_GATES, tm), 0)
    gt_ref[...] = jnp.where((row % 8) >= 4, _log_sigmoid(gt), gt)


def _seg_rows(k, half, segs_per_sample, n_ctx_segs):
    seg = 2 * k + half
    sample = seg // segs_per_sample
    within = seg % segs_per_sample
    return jnp.where(within < n_ctx_segs, 8, sample), within


def _mod_specs(width, segs_per_sample, n_ctx_segs):
    return [pl.BlockSpec((1, 1, width), lambda k, h=half: (_seg_rows(k, h, segs_per_sample, n_ctx_segs)[0], 0, 0))
            for half in range(2)]


def _segment_of_step(segs_per_sample, n_ctx_segs, with_ctx):
    if with_ctx:
        return lambda k: k
    n_lat = segs_per_sample - n_ctx_segs
    return lambda k: (k // n_lat) * segs_per_sample + n_ctx_segs + k % n_lat


def _mod_spec_one_seg(width, segs_per_sample, n_ctx_segs, seg_of):
    def index(k):
        seg = seg_of(k)
        return jnp.where(seg % segs_per_sample < n_ctx_segs, 8, seg // segs_per_sample), 0, 0
    return pl.BlockSpec((1, 1, width), index)


def _inproj(x2, mod, nw, w_main, wg, wgt, gb, gbt, cos, sin, qnw, knw, gmat, ts, lc):
    rows, d = x2.shape
    sps = ts // SEG
    ncs = lc // SEG
    tok = lambda width: pl.BlockSpec((TM, width), lambda k: (k, 0))
    full = lambda a: pl.BlockSpec(a.shape, lambda k: (0,) * a.ndim)
    table = [pl.BlockSpec((SEG, LANES), lambda k, h=half: (_seg_rows(k, h, sps, ncs)[1], 0)) for half in range(2)]
    outs = [(MIX_W, BF16), (MIX_W, BF16), (MIX_W, BF16), (MIX_W, BF16), (KV_B, BF16), (KV_B, BF16),
            (2 * MIX_W, F32), (MIX_W, BF16), (MIX_W, BF16), (LANES, F32)]
    out_shape = [jax.ShapeDtypeStruct((rows, w), dt) for w, dt in outs]
    out_specs = [tok(w) for w, _ in outs]
    out_shape.append(jax.ShapeDtypeStruct((N_GATES, rows), F32))
    out_specs.append(pl.BlockSpec((N_GATES, TM), lambda k: (0, k)))
    return pl.pallas_call(
        _inproj_kernel,
        grid=(rows // TM,),
        in_specs=[tok(d)] + _mod_specs(mod.shape[-1], sps, ncs)
        + [full(nw), full(w_main), full(wg), full(wgt), full(gb), full(gbt)]
        + table + table + [full(qnw), full(knw), full(gmat)],
        out_specs=out_specs,
        out_shape=out_shape,
        compiler_params=_cparams(1),
        name="in_proj",
    )(x2, mod, mod, nw, w_main, wg, wgt, gb, gbt, cos, cos, sin, sin, qnw, knw, gmat)


def _first_head_lanes(shape):
    lane = lax.broadcasted_iota(jnp.int32, shape, 1)
    return (lane // (LANES // 4)) % 2 == 0


def _softmax_numerators(sb, eb, rows, lk):
    maxes = [jnp.max(sb[r:r + 8, :lk], axis=-1, keepdims=True) for r in range(0, rows, 8)]
    for rb in range(rows // 16):
        parts = [jnp.exp2(sb[rb * 16 + sub * 8:rb * 16 + sub * 8 + 8, :lk] - maxes[2 * rb + sub]) for sub in range(2)]
        eb[rb * 16:(rb + 1) * 16, :lk] = jnp.concatenate(parts, axis=0).astype(BF16)


def _with_ones_column(v):
    lane = lax.broadcasted_iota(jnp.int32, v.shape, 1)
    ones = jnp.where(lane == 0, 1.0, 0.0).astype(v.dtype)
    return jnp.concatenate([v, ones], axis=1)


def _attention_units(n_units, scores, finish, values, s_s, e_s, rows, lk):
    if lk <= SMALL_KEYS:
        for u in range(n_units):
            s = scores(u)
            e = jnp.exp2(s - jnp.max(s, axis=-1, keepdims=True))
            finish(u, _dot(e.astype(BF16), values(u)))
        return
    s_s[0, :, :lk] = scores(0)
    for u in range(n_units):
        if u + 1 < n_units:
            s_s[(u + 1) % 2, :, :lk] = scores(u + 1)
        _softmax_numerators(s_s.at[u % 2], e_s.at[u % 2], rows, lk)
        finish(u, _dot(e_s[u % 2, :, :lk], values(u)))


def _diff_attn_kernel(lam_ref, sub_ref, q_ref, k_ref, v_ref, o_ref, s_s, e_s, *, lam_init, n_ctx_blocks, with_ctx, lc):
    qi = pl.program_id(1)
    lv = lam_ref[...]
    lam = (jnp.exp(jnp.sum(lv[0:1] * lv[1:2], axis=-1, keepdims=True))
           - jnp.exp(jnp.sum(lv[2:3] * lv[3:4], axis=-1, keepdims=True)) + lam_init)
    tq = q_ref.shape[1]
    w = 2 * A_HEAD_DIM

    def body(lk):
        low = _first_head_lanes((tq, w))

        def scores(h):
            cols = slice(h * w, (h + 1) * w)
            q = q_ref[0, :, cols]
            zero = jnp.zeros_like(q)
            qs = jnp.concatenate([jnp.where(low, q, zero), jnp.where(low, zero, q)], axis=0)
            return _dot_nt(qs, k_ref[0, :lk, cols])

        def values(h):
            return _with_ones_column(v_ref[0, :lk, h * w:(h + 1) * w])

        def finish(h, pv):
            o = (pv[:tq, :w] * (1.0 / pv[:tq, w:w + 1])
                 - pv[tq:, :w] * (lam / pv[tq:, w:w + 1]))
            ms = jnp.mean(o * o, axis=-1, keepdims=True)
            o_ref[0, :, h * w:(h + 1) * w] = (o * lax.rsqrt(ms + EPS) * sub_ref[...] * (1.0 - lam_init)).astype(BF16)

        _attention_units(A_HEADS, scores, finish, values, s_s, e_s, 2 * tq, lk)

    @pl.when(qi < n_ctx_blocks)
    def _():
        if with_ctx:
            body(lc)
        else:
            o_ref[...] = jnp.zeros_like(o_ref)

    @pl.when(qi >= n_ctx_blocks)
    def _():
        body(k_ref.shape[1])


def _diff_attn(lam_vecs, sub_w, qa, ka, va, lam_init, lc, with_ctx):
    b, ts, _ = qa.shape
    n_ctx_blocks = lc // TQ
    nq = ts // TQ
    kern = functools.partial(_diff_attn_kernel, lam_init=lam_init, n_ctx_blocks=n_ctx_blocks, with_ctx=with_ctx, lc=lc)
    return pl.pallas_call(
        kern,
        grid=(b, nq),
        in_specs=[pl.BlockSpec(lam_vecs.shape, lambda i, j: (0, 0)),
                  pl.BlockSpec(sub_w.shape, lambda i, j: (0, 0)),
                  pl.BlockSpec((1, TQ, MIX_W), lambda i, j: (i, j, 0)),
                  pl.BlockSpec((1, ts, MIX_W), lambda i, j: (i, 0, 0)),
                  pl.BlockSpec((1, ts, MIX_W), lambda i, j: (i, 0, 0))],
        out_specs=pl.BlockSpec((1, TQ, MIX_W), lambda i, j: (i, j, 0)),
        out_shape=jax.ShapeDtypeStruct((b, ts, MIX_W), BF16),
        scratch_shapes=[pltpu.VMEM((2, 2 * TQ, ts), F32), pltpu.VMEM((2, 2 * TQ, ts), BF16)],
        compiler_params=_cparams(2),
        name="diff_attn",
    )(lam_vecs, sub_w, qa, ka, va)


def _gqa_kernel(q_ref, k_ref, v_ref, o_ref, s_s, e_s, *, n_ctx_blocks, with_ctx, lc):
    qi = pl.program_id(1)
    tq = q_ref.shape[1]

    def body(lk):
        low = _first_head_lanes((tq, LANES))
        out_low = lax.broadcasted_iota(jnp.int32, (tq, LANES), 1) < B_HEAD_DIM
        k = k_ref[0, :lk, :]
        v1 = _with_ones_column(v_ref[0, :lk, :])
        outs = {}
        pairs = B_GROUP // 2

        def scores(u):
            g, pair = divmod(u, pairs)
            parts = []
            for j in (2 * pair, 2 * pair + 1):
                x = q_ref[0, :, j * LANES:(j + 1) * LANES]
                zero = jnp.zeros_like(x)
                parts.append(jnp.where(low, x, zero) if g == 0 else jnp.where(low, zero, x))
            return _dot_nt(jnp.concatenate(parts, axis=0), k)

        def finish(u, pv):
            g, pair = divmod(u, pairs)
            o = pv[:, :LANES] * (1.0 / pv[:, LANES:LANES + 1])
            outs[(g, 2 * pair)] = o[:tq]
            outs[(g, 2 * pair + 1)] = o[tq:]

        _attention_units(B_KV_HEADS * pairs, scores, finish, lambda u: v1, s_s, e_s, 2 * tq, lk)
        for j in range(B_GROUP):
            o_ref[0, :, j * LANES:(j + 1) * LANES] = jnp.where(out_low, outs[(0, j)], outs[(1, j)]).astype(BF16)

    @pl.when(qi < n_ctx_blocks)
    def _():
        if with_ctx:
            body(lc)
        else:
            o_ref[...] = jnp.zeros_like(o_ref)

    @pl.when(qi >= n_ctx_blocks)
    def _():
        body(k_ref.shape[1])


def _gqa_attn(qb, kb, vb, lc, with_ctx):
    b, ts, _ = qb.shape
    n_ctx_blocks = lc // TQ
    nq = ts // TQ
    kern = functools.partial(_gqa_kernel, n_ctx_blocks=n_ctx_blocks, with_ctx=with_ctx, lc=lc)
    return pl.pallas_call(
        kern,
        grid=(b, nq),
        in_specs=[pl.BlockSpec((1, TQ, MIX_W), lambda i, j: (i, j, 0)),
                  pl.BlockSpec((1, ts, KV_B), lambda i, j: (i, 0, 0)),
                  pl.BlockSpec((1, ts, KV_B), lambda i, j: (i, 0, 0))],
        out_specs=pl.BlockSpec((1, TQ, MIX_W), lambda i, j: (i, j, 0)),
        out_shape=jax.ShapeDtypeStruct((b, ts, MIX_W), BF16),
        scratch_shapes=[pltpu.VMEM((2, 2 * TQ, ts), F32), pltpu.VMEM((2, 2 * TQ, ts), BF16)],
        compiler_params=_cparams(2),
        name="gqa_attn",
    )(qb, kb, vb)


def _mlstm_kernel(q_ref, k_ref, v_ref, o_ref, g_ref, gt_ref, cwq_ref, cwk_ref, nw_ref, out_ref,
                  q_s, kt_s, bc_s, ac_s, rows_s, hacc_s, st_s, *, lc):
    ts = q_ref.shape[1]
    hp = q_ref.shape[2] // C_HEAD_DIM
    head0 = pl.program_id(1) * hp
    nc = ts // C_CHUNK
    ncc = lc // C_CHUNK
    ch = C_CHUNK

    row = lax.broadcasted_iota(jnp.int32, (ts, LANES), 0)
    prev_ok = (row != 0) & (row != lc)
    next_ok = (row != lc - 1) & (row != ts - 1)

    def conv(x, w):
        xp = jnp.where(prev_ok, pltpu.roll(x, 1, 0), 0.0)
        xn = jnp.where(next_ok, pltpu.roll(x, ts - 1, 0), 0.0)
        return _silu(xp * w[0:1] + x * w[1:2] + xn * w[2:3])

    for j in range(hp):
        cols = slice(j * LANES, (j + 1) * LANES)
        q_s[:, cols] = conv(q_ref[0, :, cols], cwq_ref[0, :, cols]).astype(BF16)
        y = conv(k_ref[0, :, cols], cwk_ref[0, :, cols]) * (C_HEAD_DIM ** -0.5)
        for c in range(nc):
            kt_s[c, cols, :] = y[c * ch:(c + 1) * ch, :].T.astype(BF16)

    ri = lax.broadcasted_iota(jnp.int32, (ch, ch), 0)
    ci = lax.broadcasted_iota(jnp.int32, (ch, ch), 1)
    lower = jnp.where(ci <= ri, 1.0, 0.0).astype(BF16)
    upper = jnp.where(ci >= ri, 1.0, 0.0).astype(BF16)
    lane = ci
    rowi = lax.broadcasted_iota(jnp.int32, (N_GATES, ch), 0)
    for c in range(nc):
        rs = slice(c * ch, (c + 1) * ch)
        g = g_ref[0, rs, :]
        g1, g2, g3 = _split3(g)
        pre = _dot(lower, g1) + _dot(lower, g2) + _dot(lower, g3)
        suf = _dot(upper, g1) + _dot(upper, g2) + _dot(upper, g3)
        gt = gt_ref[:, rs]
        t1, t2, t3 = _split3(gt)
        pre_t = _dot(t1, upper) + _dot(t2, upper) + _dot(t3, upper)
        suf_t = _dot(t1, lower) + _dot(t2, lower) + _dot(t3, lower)
        for j in range(hp):
            for direction in range(2):
                idx = j * 2 + direction
                li = head0 + j + 8 * direction
                lf = li + 4
                cum, cum_t = (pre, pre_t) if direction == 0 else (suf, suf_t)
                b_col = jnp.sum(jnp.where(lane == lf, cum, 0.0), axis=-1, keepdims=True)
                i_col = jnp.sum(jnp.where(lane == li, g, 0.0), axis=-1, keepdims=True)
                bc_s[idx, rs, :] = jnp.broadcast_to(b_col, (ch, LANES))
                ac_s[idx, rs, :] = jnp.broadcast_to(i_col - b_col, (ch, LANES))
                b_row = jnp.sum(jnp.where(rowi == lf, cum_t, 0.0), axis=0, keepdims=True)
                i_row = jnp.sum(jnp.where(rowi == li, gt, 0.0), axis=0, keepdims=True)
                rows_s[c, 2 * idx:2 * idx + 1, :] = b_row
                rows_s[c, 2 * idx + 1:2 * idx + 2, :] = i_row - b_row

    hacc_s[...] = jnp.zeros_like(hacc_s)
    st_s[...] = jnp.zeros_like(st_s)
    tri_f = ci <= ri
    tri_b = ci >= ri

    def chain(c, j, direction, m):
        idx = j * 2 + direction
        c0 = pl.multiple_of(c * ch, ch)
        cols = slice(j * LANES, (j + 1) * LANES)
        q = q_s[pl.ds(c0, ch), cols]
        kt = kt_s[c, cols, :]
        v = v_ref[0, pl.ds(c0, ch), cols]
        bc = bc_s[idx, pl.ds(c0, ch), :]
        ac = ac_s[idx, pl.ds(c0, ch), :]
        rows = rows_s[c]
        b_row = rows[2 * idx:2 * idx + 1, :]
        ib_row = rows[2 * idx + 1:2 * idx + 2, :]
        tri = tri_f if direction == 0 else tri_b
        log_d = jnp.where(tri, bc + ib_row, NEG_BIG)
        m_intra = jnp.max(log_d, axis=-1, keepdims=True)
        log_inter = bc + m
        m_t = jnp.maximum(log_inter, m_intra)
        dm = jnp.exp(log_d - m_t)
        w_inter = jnp.exp(log_inter - m_t)
        s = _dot(q, kt) * dm
        st = st_s[idx]
        inter = _dot(q, st.astype(BF16))
        num = _dot(s.astype(BF16), v) + w_inter * inter[:, :LANES]
        den = jnp.sum(s, axis=-1, keepdims=True) + w_inter * inter[:, LANES:LANES + 1]
        hout = num / jnp.maximum(jnp.abs(den), jnp.exp(-m_t))
        hacc_s[pl.ds(c0, ch), cols] = hacc_s[pl.ds(c0, ch), cols] + hout
        total = b_row[:, ch - 1:ch] if direction == 0 else b_row[:, 0:1]
        m_new = jnp.maximum(total + m, jnp.max(total + ib_row, axis=-1, keepdims=True))
        w = jnp.exp(total + ac - m_new)
        decay = jnp.exp(total + m - m_new)
        wv = jnp.concatenate([w * v.astype(F32), jnp.where(lane == 0, w, 0.0)], axis=1).astype(BF16)
        st_s[idx] = decay * st + _dot(kt, wv)
        return m_new

    def step(i, ms):
        c_f = i
        c_b = jnp.where(i < ncc, ncc - 1 - i, nc + ncc - 1 - i)
        out = []
        for j in range(hp):
            out.append(chain(c_f, j, 0, ms[j * 2]))
            out.append(chain(c_b, j, 1, ms[j * 2 + 1]))
        return tuple(out)

    lax.fori_loop(0, nc, step, tuple(jnp.zeros((1, 1), F32) for _ in range(2 * hp)), unroll=2)

    for j in range(hp):
        cols = slice(j * LANES, (j + 1) * LANES)
        x = hacc_s[:, cols]
        ms = jnp.mean(x * x, axis=-1, keepdims=True)
        y = x * lax.rsqrt(ms + EPS) * nw_ref[0, :, cols]
        out_ref[0, :, cols] = (o_ref[0, :, cols].astype(F32) * y).astype(BF16)


def _mlstm(qkc, vc, oc, g, gt, conv_w, norm_w, lc, heads_per_step=2):
    b, ts, _ = vc.shape
    wq = heads_per_step * C_HEAD_DIM
    nhp = C_HEADS // heads_per_step
    nc = ts // C_CHUNK
    cw = conv_w.reshape(C_CONV, 2 * nhp, wq).transpose(1, 0, 2)
    nw = norm_w.reshape(1, nhp, wq).transpose(1, 0, 2)
    kern = functools.partial(_mlstm_kernel, lc=lc)
    tokw = lambda off: pl.BlockSpec((1, ts, wq), lambda i, p: (i, 0, p + off))
    return pl.pallas_call(
        kern,
        grid=(b, nhp),
        in_specs=[tokw(0), tokw(nhp), tokw(0), tokw(0),
                  pl.BlockSpec((1, ts, LANES), lambda i, p: (i, 0, 0)),
                  pl.BlockSpec((N_GATES, ts), lambda i, p: (0, i)),
                  pl.BlockSpec((1, C_CONV, wq), lambda i, p: (p, 0, 0)),
                  pl.BlockSpec((1, C_CONV, wq), lambda i, p: (p + nhp, 0, 0)),
                  pl.BlockSpec((1, 1, wq), lambda i, p: (p, 0, 0))],
        out_specs=tokw(0),
        out_shape=jax.ShapeDtypeStruct((b, ts, MIX_W), BF16),
        scratch_shapes=[pltpu.VMEM((ts, wq), BF16),
                        pltpu.VMEM((nc, wq, C_CHUNK), BF16),
                        pltpu.VMEM((2 * heads_per_step, ts, LANES), F32),
                        pltpu.VMEM((2 * heads_per_step, ts, LANES), F32),
                        pltpu.VMEM((nc, 4 * heads_per_step, C_CHUNK), F32),
                        pltpu.VMEM((ts, wq), F32),
                        pltpu.VMEM((2 * heads_per_step, C_HEAD_DIM, 2 * LANES), F32)],
        compiler_params=_cparams(2),
        name="mlstm",
    )(qkc, qkc, vc, oc, g, gt, cw, cw, nw)


def _merge_kernel(x_ref, mod_ref, n1_ref, n2_ref, oa_ref, ob_ref, oc_ref, wm_ref, wa_ref, wb_ref, wc_ref,
                  wo_ref, wr_ref, xo_ref, h2_ref, aff_ref):
    d = D_MODEL
    mod = mod_ref[0]
    x = x_ref[...]
    h = _norm_mod(x, n1_ref[...], mod[:, 0:d], mod[:, d:2 * d]).astype(BF16)
    merged = (_sigmoid(_dot(h, wm_ref[:, 0:d])) * _dot(oa_ref[...], wa_ref[...])
              + _sigmoid(_dot(h, wm_ref[:, d:2 * d])) * _dot(ob_ref[...], wb_ref[...])
              + _sigmoid(_dot(h, wm_ref[:, 2 * d:3 * d])) * _dot(oc_ref[...], wc_ref[...]))
    y = _dot(merged.astype(BF16), wo_ref[...])
    x1 = x + mod[:, 2 * d:3 * d] * y
    xo_ref[...] = x1
    h2 = _norm_mod(x1, n2_ref[...], mod[:, 3 * d:4 * d], mod[:, 4 * d:5 * d])
    h2b = h2.astype(BF16)
    h2_ref[...] = h2b
    h2l = (h2 - h2b.astype(F32)).astype(BF16)
    wr = wr_ref[...]
    wrh = wr.astype(BF16)
    wrl = (wr - wrh.astype(F32)).astype(BF16)
    logits = _dot(h2b, wrh) + _dot(h2b, wrl) + _dot(h2l, wrh)
    lane = lax.broadcasted_iota(jnp.int32, logits.shape, 1)
    valid = lane < N_EXPERTS
    logits = jnp.where(valid, logits, NEG_BIG)
    e = jnp.where(valid, jnp.exp(logits - jnp.max(logits, axis=-1, keepdims=True)), 0.0)
    aff_ref[...] = e / jnp.sum(e, axis=-1, keepdims=True)


def _merge(x2, mod, n1, n2, oa, ob, oc, wm, wa, wb, wc, wo, wr, ts, lc, with_ctx):
    rows, d = x2.shape
    sps, ncs = ts // SEG, lc // SEG
    seg_of = _segment_of_step(sps, ncs, with_ctx)
    n_steps = rows // SEG if with_ctx else (rows // ts) * (sps - ncs)
    tok = lambda width: pl.BlockSpec((SEG, width), lambda k: (seg_of(k), 0))
    full = lambda a: pl.BlockSpec(a.shape, lambda k: (0,) * a.ndim)
    return pl.pallas_call(
        _merge_kernel,
        grid=(n_steps,),
        in_specs=[tok(d), _mod_spec_one_seg(mod.shape[-1], sps, ncs, seg_of),
                  full(n1), full(n2), tok(MIX_W), tok(MIX_W), tok(MIX_W),
                  full(wm), full(wa), full(wb), full(wc), full(wo), full(wr)],
        out_specs=[tok(d), tok(d), tok(LANES)],
        out_shape=[jax.ShapeDtypeStruct((rows, d), F32),
                   jax.ShapeDtypeStruct((rows, d), BF16),
                   jax.ShapeDtypeStruct((rows, LANES), F32)],
        input_output_aliases={0: 0},
        compiler_params=_cparams(1),
        name="merge_out",
    )(x2, mod, n1, n2, oa, ob, oc, wm, wa, wb, wc, wo, wr)


def _route_kernel(aff_ref, post_ref, posr_ref, affr_ref, *, cap):
    n = aff_ref.shape[0]
    ch = LANES
    aff = aff_ref[...]

    def step(i, thr_bits):
        cand = thr_bits | jnp.left_shift(jnp.int32(1), 30 - i)
        cnt = jnp.sum((aff >= pltpu.bitcast(cand, F32)).astype(jnp.int32), axis=0, keepdims=True)
        return jnp.where(cnt >= cap, cand, thr_bits)

    thr = pltpu.bitcast(lax.fori_loop(0, 31, step, jnp.zeros((1, LANES), jnp.int32)), F32)
    gt = aff > thr
    eq = aff == thr
    need = cap - jnp.sum(gt.astype(jnp.int32), axis=0, keepdims=True)

    ri = lax.broadcasted_iota(jnp.int32, (ch, ch), 0)
    ci = lax.broadcasted_iota(jnp.int32, (ch, ch), 1)
    strict_lower = jnp.where(ci < ri, 1.0, 0.0).astype(BF16)

    def excl_cumsum(mask_f):
        carry = jnp.zeros((1, LANES), F32)
        blocks = []
        for c in range(n // ch):
            blk = mask_f[c * ch:(c + 1) * ch, :]
            blocks.append(_dot(strict_lower, blk.astype(BF16)) + carry)
            carry = carry + jnp.sum(blk, axis=0, keepdims=True)
        return jnp.concatenate(blocks, axis=0)

    eq_rank = excl_cumsum(jnp.where(eq, 1.0, 0.0))
    sel = gt | (eq & (eq_rank < need.astype(F32)))
    pos = excl_cumsum(jnp.where(sel, 1.0, 0.0))
    post = jnp.where(sel, pos, -1.0)
    post_ref[...] = post
    for c in range(n // ch):
        posr_ref[:, c * ch:(c + 1) * ch] = post[c * ch:(c + 1) * ch, :].T
        affr_ref[:, c * ch:(c + 1) * ch] = aff[c * ch:(c + 1) * ch, :].T


def _route(aff_t, cap):
    n = aff_t.shape[0]
    return pl.pallas_call(
        functools.partial(_route_kernel, cap=cap),
        out_shape=[jax.ShapeDtypeStruct((n, LANES), F32),
                   jax.ShapeDtypeStruct((LANES, n), F32),
                   jax.ShapeDtypeStruct((LANES, n), F32)],
        compiler_params=pltpu.CompilerParams(vmem_limit_bytes=VMEM_LIMIT),
        name="route",
    )(aff_t)


def _gather_kernel(posr_ref, affr_ref, h_ref, xe_ref, gs_ref, *, cap, row0, n):
    ne = posr_ref.shape[0]
    slot = lax.broadcasted_iota(jnp.int32, (cap, n), 0).astype(F32)
    onehots = []
    for t in range(ne):
        pf = jnp.where(posr_ref[t] == slot, 1.0, 0.0)
        gs = jnp.sum(pf * affr_ref[t], axis=-1, keepdims=True)
        gs_ref[t] = jnp.broadcast_to(gs, (cap, LANES))
        onehots.append(pf.astype(BF16))
    xe = _dot(jnp.concatenate(onehots, axis=0), h_ref[0, row0:row0 + n, :])
    for t in range(ne):
        xe_ref[t] = xe[t * cap:(t + 1) * cap].astype(BF16)


def _gather(posr, affr, h2, cap, row0, n, b):
    ts, d = h2.shape[1:]
    be = posr.shape[0]
    posr3 = posr.reshape(be, 1, n)
    affr3 = affr.reshape(be, 1, n)
    e = N_EXPERTS
    ge = GATHER_EXPERTS
    return pl.pallas_call(
        functools.partial(_gather_kernel, cap=cap, row0=row0, n=n),
        grid=(b, e // ge),
        in_specs=[pl.BlockSpec((ge, 1, n), lambda i, j: (i * (e // ge) + j, 0, 0)),
                  pl.BlockSpec((ge, 1, n), lambda i, j: (i * (e // ge) + j, 0, 0)),
                  pl.BlockSpec((1, ts, d), lambda i, j: (i, 0, 0))],
        out_specs=[pl.BlockSpec((ge, cap, d), lambda i, j: (j, i, 0)),
                   pl.BlockSpec((ge, cap, LANES), lambda i, j: (j, i, 0))],
        out_shape=[jax.ShapeDtypeStruct((e, b * cap, d), BF16),
                   jax.ShapeDtypeStruct((e, b * cap, LANES), F32)],
        compiler_params=_cparams(2),
        name="expert_gather",
    )(posr3, affr3, h2)


def _ffn_kernel(*refs, n_groups):
    xe_refs = refs[0:n_groups]
    gs_refs = refs[n_groups:2 * n_groups]
    wg_ref, wu_ref, wd_ref = refs[2 * n_groups:2 * n_groups + 3]
    out_refs = refs[2 * n_groups + 3:3 * n_groups + 3]
    acc_refs = refs[3 * n_groups + 3:4 * n_groups + 3]
    f = pl.program_id(1)
    nf = pl.num_programs(1)
    wgb = wg_ref[0, 0].astype(BF16)
    wub = wu_ref[0, 0].astype(BF16)
    wdb = wd_ref[0, 0].astype(BF16)
    for xe_ref, gs_ref, out_ref, acc_ref in zip(xe_refs, gs_refs, out_refs, acc_refs):
        rows = xe_ref.shape[1]
        rb = min(FFN_ROWS, rows)

        @pl.when(f == 0)
        def _():
            acc_ref[...] = jnp.zeros_like(acc_ref)

        for r0 in range(0, rows, rb):
            xb = xe_ref[0, r0:r0 + rb, :]
            a = _dot(xb, wgb)
            u = _dot(xb, wub)
            y = _dot((_silu(a) * u).astype(BF16), wdb)
            acc_ref[r0:r0 + rb, :] = acc_ref[r0:r0 + rb, :] + y

        @pl.when(f == nf - 1)
        def _():
            gate = gs_ref[0][:, 0:1]
            out_ref[0] = (acc_ref[...] * gate).astype(BF16)


def _ffn(xes, gss, w_gate, w_up, w_down, layer):
    n_groups = len(xes)
    _, e, d, ff = w_gate.shape
    nf = ff // FF_TILE
    in_specs = [pl.BlockSpec((1,) + x.shape[1:], lambda i, f: (i, 0, 0)) for x in xes]
    in_specs += [pl.BlockSpec((1,) + g.shape[1:], lambda i, f: (i, 0, 0)) for g in gss]
    in_specs += [pl.BlockSpec((1, 1, d, FF_TILE), lambda i, f: (layer, i, 0, f)),
                 pl.BlockSpec((1, 1, d, FF_TILE), lambda i, f: (layer, i, 0, f)),
                 pl.BlockSpec((1, 1, FF_TILE, d), lambda i, f: (layer, i, f, 0))]
    out = pl.pallas_call(
        functools.partial(_ffn_kernel, n_groups=n_groups),
        grid=(e, nf),
        in_specs=in_specs,
        out_specs=[pl.BlockSpec((1,) + x.shape[1:], lambda i, f: (i, 0, 0)) for x in xes],
        out_shape=[jax.ShapeDtypeStruct(x.shape, BF16) for x in xes],
        scratch_shapes=[pltpu.VMEM(x.shape[1:], F32) for x in xes],
        compiler_params=_cparams(2),
        name="expert_ffn",
    )(*xes, *gss, w_gate, w_up, w_down)
    return list(out)


def _combine_kernel(*refs, cap, final):
    if final:
        post_ref, yg_ref, x_ref, mod_ref, fw_ref, o_ref = refs
    else:
        post_ref, yg_ref, x_ref, mod_ref, o_ref = refs
    d = D_MODEL
    post = post_ref[0]
    tn = post.shape[0]
    slot = lax.broadcasted_iota(jnp.int32, (tn, cap), 1).astype(F32)
    acc = jnp.zeros((tn, d), F32)
    for e in range(N_EXPERTS):
        onehot = jnp.where(post[:, e:e + 1] == slot, 1.0, 0.0).astype(BF16)
        acc = acc + _dot(onehot, yg_ref[e])
    x2 = x_ref[0] + mod_ref[0][:, 5 * d:6 * d] * acc
    if final:
        ms = jnp.mean(x2 * x2, axis=-1, keepdims=True)
        o_ref[0] = x2 * lax.rsqrt(ms + EPS) * fw_ref[...]
    else:
        o_ref[0] = x2


def _combine(post_b, yg, xs, mod, cap, row0, n, mod_row_ctx, final_w=None):
    b, ts, d = xs.shape
    tn = min(SEG, n)
    blk0 = row0 // tn
    final = final_w is not None
    in_specs = [pl.BlockSpec((1, tn, LANES), lambda i, j: (i, j, 0)),
                pl.BlockSpec((N_EXPERTS, cap, d), lambda i, j: (0, i, 0)),
                pl.BlockSpec((1, tn, d), lambda i, j: (i, j + blk0, 0)),
                pl.BlockSpec((1, 1, mod.shape[-1]), lambda i, j: (8 if mod_row_ctx else i, 0, 0))]
    args = [post_b, yg, xs, mod]
    if final:
        in_specs.append(pl.BlockSpec(final_w.shape, lambda i, j: (0, 0)))
        args.append(final_w)
        out_spec = pl.BlockSpec((1, tn, d), lambda i, j: (i, j, 0))
        out_shape = jax.ShapeDtypeStruct((b, n, d), F32)
        aliases = {}
    else:
        out_spec = pl.BlockSpec((1, tn, d), lambda i, j: (i, j + blk0, 0))
        out_shape = jax.ShapeDtypeStruct((b, ts, d), F32)
        aliases = {2: 0}
    return pl.pallas_call(
        functools.partial(_combine_kernel, cap=cap, final=final),
        grid=(b, n // tn),
        in_specs=in_specs,
        out_specs=out_spec,
        out_shape=out_shape,
        input_output_aliases=aliases,
        compiler_params=_cparams(2),
        name="expert_combine_final" if final else "expert_combine",
    )(*args)


def _rope_tables(n_lat, lc):
    n_rows = n_lat // GRID_W
    rows = jnp.repeat(jnp.arange(n_rows, dtype=F32), GRID_W)
    cols = jnp.tile(jnp.arange(GRID_W, dtype=F32), n_rows)
    n_freq = A_HEAD_DIM // 4
    inv_freq = ROPE_THETA ** (-jnp.arange(n_freq, dtype=F32) / n_freq)
    ang = jnp.concatenate([rows[:, None] * inv_freq, cols[:, None] * inv_freq], axis=-1)
    c, s = jnp.cos(ang), jnp.sin(ang)
    cos = jnp.concatenate([c, c, c, c], axis=-1)
    sin = jnp.concatenate([-s, -s, s, s], axis=-1)
    cos = jnp.concatenate([jnp.ones((lc, LANES), F32), cos], axis=0)
    sin = jnp.concatenate([jnp.zeros((lc, LANES), F32), sin], axis=0)
    return cos, sin


def _pair_rope_layout(w):
    d, n = w.shape
    q = LANES // 4
    return w.reshape(d, n // LANES, 2, 2, q).transpose(0, 1, 3, 2, 4).reshape(d, n)


def _route_layout(aff, row0, n, b):
    a = aff[:, row0:row0 + n, :N_EXPERTS]
    a = jnp.transpose(a, (1, 0, 2)).reshape(n, b * N_EXPERTS)
    return jnp.pad(a, ((0, 0), (0, LANES - b * N_EXPERTS)))


def _sample_layout(post, n, b):
    p = post[:, :b * N_EXPERTS].reshape(n, b, N_EXPERTS)
    p = jnp.transpose(p, (1, 0, 2))
    return jnp.pad(p, ((0, 0), (0, 0), (0, LANES - N_EXPERTS)), constant_values=-1.0)


def kernel(x, c, ctx, c_ctx, w_ada, b_ada, norm1_w, norm2_w, w_in, mlstm_conv_w, mlstm_gate_b, mlstm_norm_w,
           diff_lambda, diff_subln_w, gqa_qnorm_w, gqa_knorm_w, w_branch_a, w_branch_b, w_branch_c, w_out,
           w_router, w_exp_gate, w_exp_up, w_exp_down, final_norm_w):
    b, n_lat, d = x.shape
    lc = ctx.shape[1]
    depth = w_ada.shape[0]
    assert d == D_MODEL and b * N_EXPERTS <= LANES and b <= 8
    ts = lc + n_lat
    assert lc % SEG == 0 and n_lat % SEG == 0 and lc % TQ == 0 and (b * ts) % TM == 0

    xs = jnp.concatenate([ctx, x], axis=1)
    cvec = jnp.zeros((16, d), F32).at[:b].set(c).at[8].set(c_ctx)
    mods = _ada(cvec, w_ada, b_ada)
    cos, sin = _rope_tables(n_lat, lc)
    gidx = (jnp.arange(LANES) // (LANES // 4)) % 2
    gmat = jnp.where(gidx[:, None] == gidx[None, :], 1.0 / B_HEAD_DIM, 0.0).astype(BF16)
    cap_lat = EC_CAPACITY_FACTOR * n_lat // N_EXPERTS
    cap_ctx = EC_CAPACITY_FACTOR * lc // N_EXPERTS
    out = None

    for layer in range(depth):
        with_ctx = layer < depth - 1
        mod = mods[layer].reshape(16, 1, 6 * d)
        wl = w_in[layer]
        bq0 = 3 * MIX_W
        w_bq = wl[:, bq0:bq0 + MIX_W].reshape(d, B_KV_HEADS, B_GROUP, B_HEAD_DIM).transpose(0, 2, 1, 3).reshape(d, MIX_W)
        kb0 = bq0 + MIX_W
        w_main = jnp.concatenate([_pair_rope_layout(wl[:, :2 * MIX_W]), wl[:, 2 * MIX_W:bq0], _pair_rope_layout(w_bq),
                                  _pair_rope_layout(wl[:, kb0:kb0 + KV_B]), wl[:, kb0 + KV_B:MAIN_COLS]], axis=1).astype(BF16)
        w_bb = w_branch_b[layer].reshape(B_KV_HEADS, B_GROUP, B_HEAD_DIM, d).transpose(1, 0, 2, 3).reshape(MIX_W, d)
        w_gates = wl[:, GATE_COL0:GATE_COL0 + N_GATES]
        wg = jnp.pad(w_gates, ((0, 0), (0, LANES - N_GATES))).astype(BF16)
        wgt = w_gates.T.astype(BF16)
        gb = jnp.pad(mlstm_gate_b[layer], (0, LANES - N_GATES)).reshape(1, LANES)
        gbt = mlstm_gate_b[layer].reshape(N_GATES, 1)
        n1 = norm1_w[layer].reshape(1, d)
        n2 = norm2_w[layer].reshape(1, d)
        qnw = _pair_rope_layout(jnp.tile(gqa_qnorm_w[layer], LANES // B_HEAD_DIM).reshape(1, LANES))
        knw = _pair_rope_layout(jnp.tile(gqa_knorm_w[layer], LANES // B_HEAD_DIM).reshape(1, LANES))

        flat = _inproj(xs.reshape(b * ts, d), mod, n1, w_main, wg, wgt, gb, gbt, cos, sin, qnw, knw, gmat, ts, lc)
        qa, ka, va, qb, kb, vb, qkc, vc, oc, g = [a.reshape(b, ts, a.shape[-1]) for a in flat[:-1]]
        gt = flat[-1]

        lam_init = 0.8 - 0.6 * math.exp(-0.3 * layer)
        oa = _diff_attn(diff_lambda[layer], diff_subln_w[layer].reshape(1, 2 * A_HEAD_DIM),
                        qa, ka, va, lam_init, lc, with_ctx)
        ob = _gqa_attn(qb, kb, vb, lc, with_ctx)
        ocm = _mlstm(qkc, vc, oc, g, gt, mlstm_conv_w[layer], mlstm_norm_w[layer], lc)

        wm = wl[:, MERGE_COL0:].astype(BF16)
        wr = jnp.pad(w_router[layer], ((0, 0), (0, LANES - N_EXPERTS)))
        rows2 = lambda a: a.reshape(b * ts, a.shape[-1])
        xs, h2, aff = _merge(rows2(xs), mod, n1, n2, rows2(oa), rows2(ob), rows2(ocm), wm,
                             w_branch_a[layer].astype(BF16), w_bb.astype(BF16),
                             w_branch_c[layer].astype(BF16), w_out[layer].astype(BF16), wr, ts, lc, with_ctx)
        xs, h2, aff = [a.reshape(b, ts, a.shape[-1]) for a in (xs, h2, aff)]

        groups = [(lc, n_lat, cap_lat)]
        if with_ctx:
            groups.append((0, lc, cap_ctx))
        xes, gss, posts = [], [], []
        for row0, n, cap in groups:
            post, posr, affr = _route(_route_layout(aff, row0, n, b), cap)
            xe, gs = _gather(posr, affr, h2, cap, row0, n, b)
            xes.append(xe)
            gss.append(gs)
            posts.append(_sample_layout(post, n, b))
        ygs = _ffn(xes, gss, w_exp_gate, w_exp_up, w_exp_down, layer)
        for gi, (row0, n, cap) in enumerate(groups):
            is_last = (layer == depth - 1) and gi == 0
            res = _combine(posts[gi], ygs[gi], xs, mod, cap, row0, n, mod_row_ctx=(row0 == 0),
                           final_w=final_norm_w.reshape(1, d) if is_last else None)
            if is_last:
                out = res
            else:
                xs = res
    return out
```

```python
import functools
import math

import jax
import jax.numpy as jnp
from jax import lax
from jax.experimental import pallas as pl
from jax.experimental.pallas import tpu as pltpu

F32 = jnp.float32
BF16 = jnp.bfloat16

D_MODEL = 1024
DEPTH = 2
GRID_W = 64
ROPE_THETA = 10000.0
EPS = 1e-6
NEG_BIG = -1e30
MIX_W = D_MODEL // 2
A_HEAD_DIM = 64
A_HEADS = MIX_W // (2 * A_HEAD_DIM)
B_HEAD_DIM = 64
B_Q_HEADS = MIX_W // B_HEAD_DIM
B_KV_HEADS = 2
B_GROUP = B_Q_HEADS // B_KV_HEADS
C_HEAD_DIM = 128
C_HEADS = MIX_W // C_HEAD_DIM
C_CONV = 3
C_CHUNK = 128
N_BRANCH = 3
N_EXPERTS = 16
EXPERT_FF = 2 * D_MODEL
EC_CAPACITY_FACTOR = 2

LANES = 128
KV_B = B_KV_HEADS * B_HEAD_DIM
N_GATES = 4 * C_HEADS
MAIN_COLS = 8 * MIX_W + 2 * KV_B
GATE_COL0 = MAIN_COLS
MERGE_COL0 = MAIN_COLS + N_GATES
TM = 512
SEG = 256
TQ = 256
FF_TILE = 512
FFN_ROWS = 512
GATHER_EXPERTS = 4
SMALL_KEYS = 512
VMEM_LIMIT = 56 * 1024 * 1024
SCORE_SCALE = (A_HEAD_DIM ** -0.5) * math.log2(math.e)


def _cparams(n_axes, vmem=VMEM_LIMIT):
    return pltpu.CompilerParams(dimension_semantics=("arbitrary",) * n_axes, vmem_limit_bytes=vmem)


def _dot(a, b):
    return jnp.dot(a, b, preferred_element_type=F32)


def _dot_nt(a, b):
    return lax.dot_general(a, b, (((1,), (1,)), ((), ())), preferred_element_type=F32)


def _split3(x):
    a = x.astype(BF16)
    r = x - a.astype(F32)
    b = r.astype(BF16)
    c = (r - b.astype(F32)).astype(BF16)
    return a, b, c


def _sigmoid(x):
    return 1.0 / (1.0 + jnp.exp(-x))


def _silu(x):
    return x * _sigmoid(x)


def _log_sigmoid(x):
    return jnp.minimum(x, 0.0) - jnp.log(1.0 + jnp.exp(-jnp.abs(x)))


def _norm_mod(x, nw, shift, scale):
    ms = jnp.mean(x * x, axis=-1, keepdims=True)
    return (x * lax.rsqrt(ms + EPS) * nw) * (1.0 + scale) + shift


def _ada_kernel(c_ref, w_ref, b_ref, o_ref):
    s = _silu(c_ref[...])
    s1, s2, _ = _split3(s)
    w = w_ref[0]
    w1, w2, _ = _split3(w)
    o_ref[0] = _dot(s1, w1) + _dot(s1, w2) + _dot(s2, w1) + b_ref[0]


def _ada(cvec, w_ada, b_ada):
    depth, d, n = w_ada.shape
    tn = 1536
    return pl.pallas_call(
        _ada_kernel,
        grid=(depth, n // tn),
        in_specs=[pl.BlockSpec((16, d), lambda l, j: (0, 0)),
                  pl.BlockSpec((1, d, tn), lambda l, j: (l, 0, j)),
                  pl.BlockSpec((1, 1, tn), lambda l, j: (l, 0, j))],
        out_specs=pl.BlockSpec((1, 16, tn), lambda l, j: (l, 0, j)),
        out_shape=jax.ShapeDtypeStruct((depth, 16, n), F32),
        compiler_params=_cparams(2),
        name="ada_mod",
    )(cvec, w_ada, b_ada.reshape(depth, 1, n))


def _inproj_kernel(x_ref, mod0_ref, mod1_ref, nw_ref, w_ref, wg_ref, wgt_ref, gb_ref, gbt_ref,
                   cos0_ref, cos1_ref, sin0_ref, sin1_ref, qnw_ref, knw_ref,
                   qa_ref, ka_ref, va_ref, qb_ref, kb_ref, vb_ref, qkc_ref, vc_ref, oc_ref, g_ref, gt_ref):
    d = D_MODEL
    tm = x_ref.shape[0]
    hs = []
    for half, mod_ref in enumerate((mod0_ref, mod1_ref)):
        mod = mod_ref[0]
        x = x_ref[half * SEG:(half + 1) * SEG, :]
        hs.append(_norm_mod(x, nw_ref[...], mod[:, 0:d], mod[:, d:2 * d]).astype(BF16))
    h = jnp.concatenate(hs, axis=0)
    cos = jnp.concatenate([cos0_ref[...], cos1_ref[...]], axis=0)
    sin = jnp.concatenate([sin0_ref[...], sin1_ref[...]], axis=0)
    lane = lax.broadcasted_iota(jnp.int32, (tm, LANES), 1)

    def rope(p):
        return p * cos + pltpu.roll(p, LANES // 2, 1) * sin

    first_head = _first_head_lanes((tm, LANES))

    def head_inv_rms(p):
        sq = p * p
        sa = jnp.sum(jnp.where(first_head, sq, 0.0), axis=-1, keepdims=True)
        sb = jnp.sum(jnp.where(first_head, 0.0, sq), axis=-1, keepdims=True)
        return lax.rsqrt(jnp.where(first_head, sa, sb) * (1.0 / B_HEAD_DIM) + EPS)

    def proj(c0):
        return _dot(h, w_ref[:, c0:c0 + 2 * LANES])

    def halves(p):
        return p[:, :LANES], p[:, LANES:]

    pk, pv = halves(proj(2048))
    vb_ref[...] = pv.astype(BF16)
    chunks = [p for j in range(2) for p in halves(proj(1536 + j * 256))] + [pk]
    for j in range(4):
        qkc_ref[:, j * 256:(j + 1) * 256] = proj(2304 + j * 256)
    qnw = qnw_ref[...]
    for i, p in enumerate(chunks[:-1]):
        qb_ref[:, i * LANES:(i + 1) * LANES] = (rope(p * head_inv_rms(p) * qnw) * SCORE_SCALE).astype(BF16)
    kb_ref[...] = rope(pk * head_inv_rms(pk) * knw_ref[...]).astype(BF16)
    for j in range(2):
        for half, p in enumerate(halves(proj(j * 256))):
            c = j * 256 + half * LANES
            qa_ref[:, c:c + LANES] = (rope(p) * SCORE_SCALE).astype(BF16)
    for j in range(2):
        for half, p in enumerate(halves(proj(512 + j * 256))):
            c = j * 256 + half * LANES
            ka_ref[:, c:c + LANES] = rope(p).astype(BF16)
    for j in range(2):
        va_ref[:, j * 256:(j + 1) * 256] = proj(1024 + j * 256).astype(BF16)
    for j in range(2):
        vc_ref[:, j * 256:(j + 1) * 256] = proj(3328 + j * 256).astype(BF16)
    for j in range(2):
        oc_ref[:, j * 256:(j + 1) * 256] = _sigmoid(proj(3840 + j * 256)).astype(BF16)
    g = _dot(h, wg_ref[...]) + gb_ref[...]
    is_f = ((lane % 8) >= 4) & (lane < N_GATES)
    g_ref[...] = jnp.where(is_f, _log_sigmoid(g), g)
    gt = _dot_nt(wgt_ref[...], h) + gbt_ref[...]
    row = lax.broadcasted_iota(jnp.int32, (N_GATES, tm), 0)
    gt_ref[...] = jnp.where((row % 8) >= 4, _log_sigmoid(gt), gt)


def _seg_rows(k, half, segs_per_sample, n_ctx_segs):
    seg = 2 * k + half
    sample = seg // segs_per_sample
    within = seg % segs_per_sample
    return jnp.where(within < n_ctx_segs, 8, sample), within


def _mod_specs(width, segs_per_sample, n_ctx_segs):
    return [pl.BlockSpec((1, 1, width), lambda k, h=half: (_seg_rows(k, h, segs_per_sample, n_ctx_segs)[0], 0, 0))
            for half in range(2)]


def _segment_of_step(segs_per_sample, n_ctx_segs, with_ctx):
    if with_ctx:
        return lambda k: k
    n_lat = segs_per_sample - n_ctx_segs
    return lambda k: (k // n_lat) * segs_per_sample + n_ctx_segs + k % n_lat


def _mod_spec_one_seg(width, segs_per_sample, n_ctx_segs, seg_of):
    def index(k):
        seg = seg_of(k)
        return jnp.where(seg % segs_per_sample < n_ctx_segs, 8, seg // segs_per_sample), 0, 0
    return pl.BlockSpec((1, 1, width), index)


def _inproj(x2, mod, nw, w_main, wg, wgt, gb, gbt, cos, sin, qnw, knw, ts, lc):
    rows, d = x2.shape
    sps = ts // SEG
    ncs = lc // SEG
    tok = lambda width: pl.BlockSpec((TM, width), lambda k: (k, 0))
    full = lambda a: pl.BlockSpec(a.shape, lambda k: (0,) * a.ndim)
    table = [pl.BlockSpec((SEG, LANES), lambda k, h=half: (_seg_rows(k, h, sps, ncs)[1], 0)) for half in range(2)]
    outs = [(MIX_W, BF16), (MIX_W, BF16), (MIX_W, BF16), (MIX_W, BF16), (KV_B, BF16), (KV_B, BF16),
            (2 * MIX_W, F32), (MIX_W, BF16), (MIX_W, BF16), (LANES, F32)]
    out_shape = [jax.ShapeDtypeStruct((rows, w), dt) for w, dt in outs]
    out_specs = [tok(w) for w, _ in outs]
    out_shape.append(jax.ShapeDtypeStruct((N_GATES, rows), F32))
    out_specs.append(pl.BlockSpec((N_GATES, TM), lambda k: (0, k)))
    return pl.pallas_call(
        _inproj_kernel,
        grid=(rows // TM,),
        in_specs=[tok(d)] + _mod_specs(mod.shape[-1], sps, ncs)
        + [full(nw), full(w_main), full(wg), full(wgt), full(gb), full(gbt)]
        + table + table + [full(qnw), full(knw)],
        out_specs=out_specs,
        out_shape=out_shape,
        compiler_params=_cparams(1),
        name="in_proj",
    )(x2, mod, mod, nw, w_main, wg, wgt, gb, gbt, cos, cos, sin, sin, qnw, knw)


def _first_head_lanes(shape):
    lane = lax.broadcasted_iota(jnp.int32, shape, 1)
    return (lane // (LANES // 4)) % 2 == 0


def _softmax_numerators(sb, eb, rows, lk):
    maxes = [jnp.max(sb[r:r + 8, :lk], axis=-1, keepdims=True) for r in range(0, rows, 8)]
    for rb in range(rows // 16):
        parts = [jnp.exp2(sb[rb * 16 + sub * 8:rb * 16 + sub * 8 + 8, :lk] - maxes[2 * rb + sub]) for sub in range(2)]
        eb[rb * 16:(rb + 1) * 16, :lk] = jnp.concatenate(parts, axis=0).astype(BF16)


def _with_ones_column(v):
    lane = lax.broadcasted_iota(jnp.int32, v.shape, 1)
    ones = jnp.where(lane == 0, 1.0, 0.0).astype(v.dtype)
    return jnp.concatenate([v, ones], axis=1)


def _attention_units(n_units, scores, finish, values, s_s, e_s, rows, lk):
    if lk <= SMALL_KEYS:
        for u in range(n_units):
            s = scores(u)
            e = jnp.exp2(s - jnp.max(s, axis=-1, keepdims=True))
            finish(u, _dot(e.astype(BF16), values(u)))
        return
    s_s[0, :, :lk] = scores(0)
    for u in range(n_units):
        if u + 1 < n_units:
            s_s[(u + 1) % 2, :, :lk] = scores(u + 1)
        _softmax_numerators(s_s.at[u % 2], e_s.at[u % 2], rows, lk)
        finish(u, _dot(e_s[u % 2, :, :lk], values(u)))


def _diff_attn_kernel(lam_ref, sub_ref, q_ref, k_ref, v_ref, o_ref, s_s, e_s, *, lam_init, n_ctx_blocks, with_ctx, lc):
    qi = pl.program_id(1)
    lv = lam_ref[...]
    lam = (jnp.exp(jnp.sum(lv[0:1] * lv[1:2], axis=-1, keepdims=True))
           - jnp.exp(jnp.sum(lv[2:3] * lv[3:4], axis=-1, keepdims=True)) + lam_init)
    tq = q_ref.shape[1]
    w = 2 * A_HEAD_DIM

    def body(lk):
        low = _first_head_lanes((tq, w))

        def scores(h):
            cols = slice(h * w, (h + 1) * w)
            q = q_ref[0, :, cols]
            zero = jnp.zeros_like(q)
            qs = jnp.concatenate([jnp.where(low, q, zero), jnp.where(low, zero, q)], axis=0)
            return _dot_nt(qs, k_ref[0, :lk, cols])

        def values(h):
            return _with_ones_column(v_ref[0, :lk, h * w:(h + 1) * w])

        def finish(h, pv):
            o = (pv[:tq, :w] * (1.0 / pv[:tq, w:w + 1])
                 - pv[tq:, :w] * (lam / pv[tq:, w:w + 1]))
            ms = jnp.mean(o * o, axis=-1, keepdims=True)
            o_ref[0, :, h * w:(h + 1) * w] = (o * lax.rsqrt(ms + EPS) * sub_ref[...] * (1.0 - lam_init)).astype(BF16)

        _attention_units(A_HEADS, scores, finish, values, s_s, e_s, 2 * tq, lk)

    @pl.when(qi < n_ctx_blocks)
    def _():
        if with_ctx:
            body(lc)
        else:
            o_ref[...] = jnp.zeros_like(o_ref)

    @pl.when(qi >= n_ctx_blocks)
    def _():
        body(k_ref.shape[1])


def _diff_attn(lam_vecs, sub_w, qa, ka, va, lam_init, lc, with_ctx):
    b, ts, _ = qa.shape
    n_ctx_blocks = lc // TQ
    nq = ts // TQ
    kern = functools.partial(_diff_attn_kernel, lam_init=lam_init, n_ctx_blocks=n_ctx_blocks, with_ctx=with_ctx, lc=lc)
    return pl.pallas_call(
        kern,
        grid=(b, nq),
        in_specs=[pl.BlockSpec(lam_vecs.shape, lambda i, j: (0, 0)),
                  pl.BlockSpec(sub_w.shape, lambda i, j: (0, 0)),
                  pl.BlockSpec((1, TQ, MIX_W), lambda i, j: (i, j, 0)),
                  pl.BlockSpec((1, ts, MIX_W), lambda i, j: (i, 0, 0)),
                  pl.BlockSpec((1, ts, MIX_W), lambda i, j: (i, 0, 0))],
        out_specs=pl.BlockSpec((1, TQ, MIX_W), lambda i, j: (i, j, 0)),
        out_shape=jax.ShapeDtypeStruct((b, ts, MIX_W), BF16),
        scratch_shapes=[pltpu.VMEM((2, 2 * TQ, ts), F32), pltpu.VMEM((2, 2 * TQ, ts), BF16)],
        compiler_params=_cparams(2),
        name="diff_attn",
    )(lam_vecs, sub_w, qa, ka, va)


def _gqa_kernel(q_ref, k_ref, v_ref, o_ref, s_s, e_s, *, n_ctx_blocks, with_ctx, lc):
    qi = pl.program_id(1)
    tq = q_ref.shape[1]

    def body(lk):
        low = _first_head_lanes((tq, LANES))
        out_low = lax.broadcasted_iota(jnp.int32, (tq, LANES), 1) < B_HEAD_DIM
        k = k_ref[0, :lk, :]
        v1 = _with_ones_column(v_ref[0, :lk, :])
        outs = {}
        pairs = B_GROUP // 2

        def scores(u):
            g, pair = divmod(u, pairs)
            parts = []
            for j in (2 * pair, 2 * pair + 1):
                x = q_ref[0, :, j * LANES:(j + 1) * LANES]
                zero = jnp.zeros_like(x)
                parts.append(jnp.where(low, x, zero) if g == 0 else jnp.where(low, zero, x))
            return _dot_nt(jnp.concatenate(parts, axis=0), k)

        def finish(u, pv):
            g, pair = divmod(u, pairs)
            o = pv[:, :LANES] * (1.0 / pv[:, LANES:LANES + 1])
            outs[(g, 2 * pair)] = o[:tq]
            outs[(g, 2 * pair + 1)] = o[tq:]

        _attention_units(B_KV_HEADS * pairs, scores, finish, lambda u: v1, s_s, e_s, 2 * tq, lk)
        for j in range(B_GROUP):
            o_ref[0, :, j * LANES:(j + 1) * LANES] = jnp.where(out_low, outs[(0, j)], outs[(1, j)]).astype(BF16)

    @pl.when(qi < n_ctx_blocks)
    def _():
        if with_ctx:
            body(lc)
        else:
            o_ref[...] = jnp.zeros_like(o_ref)

    @pl.when(qi >= n_ctx_blocks)
    def _():
        body(k_ref.shape[1])


def _gqa_attn(qb, kb, vb, lc, with_ctx):
    b, ts, _ = qb.shape
    n_ctx_blocks = lc // TQ
    nq = ts // TQ
    kern = functools.partial(_gqa_kernel, n_ctx_blocks=n_ctx_blocks, with_ctx=with_ctx, lc=lc)
    return pl.pallas_call(
        kern,
        grid=(b, nq),
        in_specs=[pl.BlockSpec((1, TQ, MIX_W), lambda i, j: (i, j, 0)),
                  pl.BlockSpec((1, ts, KV_B), lambda i, j: (i, 0, 0)),
                  pl.BlockSpec((1, ts, KV_B), lambda i, j: (i, 0, 0))],
        out_specs=pl.BlockSpec((1, TQ, MIX_W), lambda i, j: (i, j, 0)),
        out_shape=jax.ShapeDtypeStruct((b, ts, MIX_W), BF16),
        scratch_shapes=[pltpu.VMEM((2, 2 * TQ, ts), F32), pltpu.VMEM((2, 2 * TQ, ts), BF16)],
        compiler_params=_cparams(2),
        name="gqa_attn",
    )(qb, kb, vb)


def _mlstm_kernel(q_ref, k_ref, v_ref, o_ref, g_ref, gt_ref, cwq_ref, cwk_ref, nw_ref, out_ref,
                  q_s, kt_s, bc_s, ac_s, rows_s, hacc_s, st_s, *, lc):
    ts = q_ref.shape[1]
    hp = q_ref.shape[2] // C_HEAD_DIM
    head0 = pl.program_id(1) * hp
    nc = ts // C_CHUNK
    ncc = lc // C_CHUNK
    ch = C_CHUNK

    row = lax.broadcasted_iota(jnp.int32, (ts, LANES), 0)
    prev_ok = (row != 0) & (row != lc)
    next_ok = (row != lc - 1) & (row != ts - 1)

    def conv(x, w):
        xp = jnp.where(prev_ok, pltpu.roll(x, 1, 0), 0.0)
        xn = jnp.where(next_ok, pltpu.roll(x, ts - 1, 0), 0.0)
        return _silu(xp * w[0:1] + x * w[1:2] + xn * w[2:3])

    for j in range(hp):
        cols = slice(j * LANES, (j + 1) * LANES)
        q_s[:, cols] = conv(q_ref[0, :, cols], cwq_ref[0, :, cols]).astype(BF16)
        y = conv(k_ref[0, :, cols], cwk_ref[0, :, cols]) * (C_HEAD_DIM ** -0.5)
        for c in range(nc):
            kt_s[c, cols, :] = y[c * ch:(c + 1) * ch, :].T.astype(BF16)

    ri = lax.broadcasted_iota(jnp.int32, (ch, ch), 0)
    ci = lax.broadcasted_iota(jnp.int32, (ch, ch), 1)
    lower = jnp.where(ci <= ri, 1.0, 0.0).astype(BF16)
    upper = jnp.where(ci >= ri, 1.0, 0.0).astype(BF16)
    lane = ci
    rowi = lax.broadcasted_iota(jnp.int32, (N_GATES, ch), 0)
    for c in range(nc):
        rs = slice(c * ch, (c + 1) * ch)
        g = g_ref[0, rs, :]
        g1, g2, g3 = _split3(g)
        pre = _dot(lower, g1) + _dot(lower, g2) + _dot(lower, g3)
        suf = _dot(upper, g1) + _dot(upper, g2) + _dot(upper, g3)
        gt = gt_ref[:, rs]
        t1, t2, t3 = _split3(gt)
        pre_t = _dot(t1, upper) + _dot(t2, upper) + _dot(t3, upper)
        suf_t = _dot(t1, lower) + _dot(t2, lower) + _dot(t3, lower)
        for j in range(hp):
            for direction in range(2):
                idx = j * 2 + direction
                li = head0 + j + 8 * direction
                lf = li + 4
                cum, cum_t = (pre, pre_t) if direction == 0 else (suf, suf_t)
                b_col = jnp.sum(jnp.where(lane == lf, cum, 0.0), axis=-1, keepdims=True)
                i_col = jnp.sum(jnp.where(lane == li, g, 0.0), axis=-1, keepdims=True)
                bc_s[idx, rs, :] = jnp.broadcast_to(b_col, (ch, LANES))
                ac_s[idx, rs, :] = jnp.broadcast_to(i_col - b_col, (ch, LANES))
                b_row = jnp.sum(jnp.where(rowi == lf, cum_t, 0.0), axis=0, keepdims=True)
                i_row = jnp.sum(jnp.where(rowi == li, gt, 0.0), axis=0, keepdims=True)
                rows_s[c, 2 * idx:2 * idx + 1, :] = b_row
                rows_s[c, 2 * idx + 1:2 * idx + 2, :] = i_row - b_row

    hacc_s[...] = jnp.zeros_like(hacc_s)
    st_s[...] = jnp.zeros_like(st_s)
    tri_f = ci <= ri
    tri_b = ci >= ri

    def chain(c, j, direction, m):
        idx = j * 2 + direction
        c0 = pl.multiple_of(c * ch, ch)
        cols = slice(j * LANES, (j + 1) * LANES)
        q = q_s[pl.ds(c0, ch), cols]
        kt = kt_s[c, cols, :]
        v = v_ref[0, pl.ds(c0, ch), cols]
        bc = bc_s[idx, pl.ds(c0, ch), :]
        ac = ac_s[idx, pl.ds(c0, ch), :]
        rows = rows_s[c]
        b_row = rows[2 * idx:2 * idx + 1, :]
        ib_row = rows[2 * idx + 1:2 * idx + 2, :]
        tri = tri_f if direction == 0 else tri_b
        log_d = jnp.where(tri, bc + ib_row, NEG_BIG)
        m_intra = jnp.max(log_d, axis=-1, keepdims=True)
        log_inter = bc + m
        m_t = jnp.maximum(log_inter, m_intra)
        dm = jnp.exp(log_d - m_t)
        w_inter = jnp.exp(log_inter - m_t)
        s = _dot(q, kt) * dm
        st = st_s[idx]
        inter = _dot(q, st.astype(BF16))
        num = _dot(s.astype(BF16), v) + w_inter * inter[:, :LANES]
        den = jnp.sum(s, axis=-1, keepdims=True) + w_inter * inter[:, LANES:LANES + 1]
        hout = num / jnp.maximum(jnp.abs(den), jnp.exp(-m_t))
        hacc_s[pl.ds(c0, ch), cols] = hacc_s[pl.ds(c0, ch), cols] + hout
        total = b_row[:, ch - 1:ch] if direction == 0 else b_row[:, 0:1]
        m_new = jnp.maximum(total + m, jnp.max(total + ib_row, axis=-1, keepdims=True))
        w = jnp.exp(total + ac - m_new)
        decay = jnp.exp(total + m - m_new)
        wv = jnp.concatenate([w * v.astype(F32), jnp.where(lane == 0, w, 0.0)], axis=1).astype(BF16)
        st_s[idx] = decay * st + _dot(kt, wv)
        return m_new

    def step(i, ms):
        c_f = i
        c_b = jnp.where(i < ncc, ncc - 1 - i, nc + ncc - 1 - i)
        out = []
        for j in range(hp):
            out.append(chain(c_f, j, 0, ms[j * 2]))
            out.append(chain(c_b, j, 1, ms[j * 2 + 1]))
        return tuple(out)

    lax.fori_loop(0, nc, step, tuple(jnp.zeros((1, 1), F32) for _ in range(2 * hp)), unroll=2)

    for j in range(hp):
        cols = slice(j * LANES, (j + 1) * LANES)
        x = hacc_s[:, cols]
        ms = jnp.mean(x * x, axis=-1, keepdims=True)
        y = x * lax.rsqrt(ms + EPS) * nw_ref[0, :, cols]
        out_ref[0, :, cols] = (o_ref[0, :, cols].astype(F32) * y).astype(BF16)


def _mlstm(qkc, vc, oc, g, gt, conv_w, norm_w, lc, heads_per_step=2):
    b, ts, _ = vc.shape
    wq = heads_per_step * C_HEAD_DIM
    nhp = C_HEADS // heads_per_step
    nc = ts // C_CHUNK
    cw = conv_w.reshape(C_CONV, 2 * nhp, wq).transpose(1, 0, 2)
    nw = norm_w.reshape(1, nhp, wq).transpose(1, 0, 2)
    kern = functools.partial(_mlstm_kernel, lc=lc)
    tokw = lambda off: pl.BlockSpec((1, ts, wq), lambda i, p: (i, 0, p + off))
    return pl.pallas_call(
        kern,
        grid=(b, nhp),
        in_specs=[tokw(0), tokw(nhp), tokw(0), tokw(0),
                  pl.BlockSpec((1, ts, LANES), lambda i, p: (i, 0, 0)),
                  pl.BlockSpec((N_GATES, ts), lambda i, p: (0, i)),
                  pl.BlockSpec((1, C_CONV, wq), lambda i, p: (p, 0, 0)),
                  pl.BlockSpec((1, C_CONV, wq), lambda i, p: (p + nhp, 0, 0)),
                  pl.BlockSpec((1, 1, wq), lambda i, p: (p, 0, 0))],
        out_specs=tokw(0),
        out_shape=jax.ShapeDtypeStruct((b, ts, MIX_W), BF16),
        scratch_shapes=[pltpu.VMEM((ts, wq), BF16),
                        pltpu.VMEM((nc, wq, C_CHUNK), BF16),
                        pltpu.VMEM((2 * heads_per_step, ts, LANES), F32),
                        pltpu.VMEM((2 * heads_per_step, ts, LANES), F32),
                        pltpu.VMEM((nc, 4 * heads_per_step, C_CHUNK), F32),
                        pltpu.VMEM((ts, wq), F32),
                        pltpu.VMEM((2 * heads_per_step, C_HEAD_DIM, 2 * LANES), F32)],
        compiler_params=_cparams(2),
        name="mlstm",
    )(qkc, qkc, vc, oc, g, gt, cw, cw, nw)


def _merge_kernel(x_ref, mod_ref, n1_ref, n2_ref, oa_ref, ob_ref, oc_ref, wm_ref, wa_ref, wb_ref, wc_ref,
                  wo_ref, wr_ref, xo_ref, h2_ref, aff_ref):
    d = D_MODEL
    mod = mod_ref[0]
    x = x_ref[...]
    h = _norm_mod(x, n1_ref[...], mod[:, 0:d], mod[:, d:2 * d]).astype(BF16)
    merged = (_sigmoid(_dot(h, wm_ref[:, 0:d])) * _dot(oa_ref[...], wa_ref[...])
              + _sigmoid(_dot(h, wm_ref[:, d:2 * d])) * _dot(ob_ref[...], wb_ref[...])
              + _sigmoid(_dot(h, wm_ref[:, 2 * d:3 * d])) * _dot(oc_ref[...], wc_ref[...]))
    y = _dot(merged.astype(BF16), wo_ref[...])
    x1 = x + mod[:, 2 * d:3 * d] * y
    xo_ref[...] = x1
    h2 = _norm_mod(x1, n2_ref[...], mod[:, 3 * d:4 * d], mod[:, 4 * d:5 * d])
    h2b = h2.astype(BF16)
    h2_ref[...] = h2b
    h2l = (h2 - h2b.astype(F32)).astype(BF16)
    wr = wr_ref[...]
    wrh = wr.astype(BF16)
    wrl = (wr - wrh.astype(F32)).astype(BF16)
    logits = _dot(h2b, wrh) + _dot(h2b, wrl) + _dot(h2l, wrh)
    lane = lax.broadcasted_iota(jnp.int32, logits.shape, 1)
    valid = lane < N_EXPERTS
    logits = jnp.where(valid, logits, NEG_BIG)
    e = jnp.where(valid, jnp.exp(logits - jnp.max(logits, axis=-1, keepdims=True)), 0.0)
    aff_ref[...] = e / jnp.sum(e, axis=-1, keepdims=True)


def _merge(x2, mod, n1, n2, oa, ob, oc, wm, wa, wb, wc, wo, wr, ts, lc, with_ctx):
    rows, d = x2.shape
    sps, ncs = ts // SEG, lc // SEG
    seg_of = _segment_of_step(sps, ncs, with_ctx)
    n_steps = rows // SEG if with_ctx else (rows // ts) * (sps - ncs)
    tok = lambda width: pl.BlockSpec((SEG, width), lambda k: (seg_of(k), 0))
    full = lambda a: pl.BlockSpec(a.shape, lambda k: (0,) * a.ndim)
    return pl.pallas_call(
        _merge_kernel,
        grid=(n_steps,),
        in_specs=[tok(d), _mod_spec_one_seg(mod.shape[-1], sps, ncs, seg_of),
                  full(n1), full(n2), tok(MIX_W), tok(MIX_W), tok(MIX_W),
                  full(wm), full(wa), full(wb), full(wc), full(wo), full(wr)],
        out_specs=[tok(d), tok(d), tok(LANES)],
        out_shape=[jax.ShapeDtypeStruct((rows, d), F32),
                   jax.ShapeDtypeStruct((rows, d), BF16),
                   jax.ShapeDtypeStruct((rows, LANES), F32)],
        input_output_aliases={0: 0},
        compiler_params=_cparams(1),
        name="merge_out",
    )(x2, mod, n1, n2, oa, ob, oc, wm, wa, wb, wc, wo, wr)


def _route_kernel(aff_ref, post_ref, posr_ref, affr_ref, *, cap):
    n = aff_ref.shape[0]
    ch = LANES
    aff = aff_ref[...]

    def step(i, thr_bits):
        cand = thr_bits | jnp.left_shift(jnp.int32(1), 30 - i)
        cnt = jnp.sum((aff >= pltpu.bitcast(cand, F32)).astype(jnp.int32), axis=0, keepdims=True)
        return jnp.where(cnt >= cap, cand, thr_bits)

    thr = pltpu.bitcast(lax.fori_loop(0, 31, step, jnp.zeros((1, LANES), jnp.int32)), F32)
    gt = aff > thr
    eq = aff == thr
    need = cap - jnp.sum(gt.astype(jnp.int32), axis=0, keepdims=True)

    ri = lax.broadcasted_iota(jnp.int32, (ch, ch), 0)
    ci = lax.broadcasted_iota(jnp.int32, (ch, ch), 1)
    strict_lower = jnp.where(ci < ri, 1.0, 0.0).astype(BF16)

    def excl_cumsum(mask_f):
        carry = jnp.zeros((1, LANES), F32)
        blocks = []
        for c in range(n // ch):
            blk = mask_f[c * ch:(c + 1) * ch, :]
            blocks.append(_dot(strict_lower, blk.astype(BF16)) + carry)
            carry = carry + jnp.sum(blk, axis=0, keepdims=True)
        return jnp.concatenate(blocks, axis=0)

    eq_rank = excl_cumsum(jnp.where(eq, 1.0, 0.0))
    sel = gt | (eq & (eq_rank < need.astype(F32)))
    pos = excl_cumsum(jnp.where(sel, 1.0, 0.0))
    post = jnp.where(sel, pos, -1.0)
    post_ref[...] = post
    for c in range(n // ch):
        posr_ref[:, c * ch:(c + 1) * ch] = post[c * ch:(c + 1) * ch, :].T
        affr_ref[:, c * ch:(c + 1) * ch] = aff[c * ch:(c + 1) * ch, :].T


def _route(aff_t, cap):
    n = aff_t.shape[0]
    return pl.pallas_call(
        functools.partial(_route_kernel, cap=cap),
        out_shape=[jax.ShapeDtypeStruct((n, LANES), F32),
                   jax.ShapeDtypeStruct((LANES, n), F32),
                   jax.ShapeDtypeStruct((LANES, n), F32)],
        compiler_params=pltpu.CompilerParams(vmem_limit_bytes=VMEM_LIMIT),
        name="route",
    )(aff_t)


def _gather_kernel(posr_ref, affr_ref, h_ref, xe_ref, gs_ref, *, cap, row0, n):
    ne = posr_ref.shape[0]
    slot = lax.broadcasted_iota(jnp.int32, (cap, n), 0).astype(F32)
    onehots = []
    for t in range(ne):
        pf = jnp.where(posr_ref[t] == slot, 1.0, 0.0)
        gs = jnp.sum(pf * affr_ref[t], axis=-1, keepdims=True)
        gs_ref[t] = jnp.broadcast_to(gs, (cap, LANES))
        onehots.append(pf.astype(BF16))
    xe = _dot(jnp.concatenate(onehots, axis=0), h_ref[0, row0:row0 + n, :])
    for t in range(ne):
        xe_ref[t] = xe[t * cap:(t + 1) * cap].astype(BF16)


def _gather(posr, affr, h2, cap, row0, n, b):
    ts, d = h2.shape[1:]
    be = posr.shape[0]
    posr3 = posr.reshape(be, 1, n)
    affr3 = affr.reshape(be, 1, n)
    e = N_EXPERTS
    ge = GATHER_EXPERTS
    return pl.pallas_call(
        functools.partial(_gather_kernel, cap=cap, row0=row0, n=n),
        grid=(b, e // ge),
        in_specs=[pl.BlockSpec((ge, 1, n), lambda i, j: (i * (e // ge) + j, 0, 0)),
                  pl.BlockSpec((ge, 1, n), lambda i, j: (i * (e // ge) + j, 0, 0)),
                  pl.BlockSpec((1, ts, d), lambda i, j: (i, 0, 0))],
        out_specs=[pl.BlockSpec((ge, cap, d), lambda i, j: (j, i, 0)),
                   pl.BlockSpec((ge, cap, LANES), lambda i, j: (j, i, 0))],
        out_shape=[jax.ShapeDtypeStruct((e, b * cap, d), BF16),
                   jax.ShapeDtypeStruct((e, b * cap, LANES), F32)],
        compiler_params=_cparams(2),
        name="expert_gather",
    )(posr3, affr3, h2)


def _ffn_kernel(*refs, n_groups):
    xe_refs = refs[0:n_groups]
    gs_refs = refs[n_groups:2 * n_groups]
    wg_ref, wu_ref, wd_ref = refs[2 * n_groups:2 * n_groups + 3]
    out_refs = refs[2 * n_groups + 3:3 * n_groups + 3]
    acc_refs = refs[3 * n_groups + 3:4 * n_groups + 3]
    f = pl.program_id(1)
    nf = pl.num_programs(1)
    wgb = wg_ref[0, 0].astype(BF16)
    wub = wu_ref[0, 0].astype(BF16)
    wdb = wd_ref[0, 0].astype(BF16)
    for xe_ref, gs_ref, out_ref, acc_ref in zip(xe_refs, gs_refs, out_refs, acc_refs):
        rows = xe_ref.shape[1]
        rb = min(FFN_ROWS, rows)

        @pl.when(f == 0)
        def _():
            acc_ref[...] = jnp.zeros_like(acc_ref)

        for r0 in range(0, rows, rb):
            xb = xe_ref[0, r0:r0 + rb, :]
            a = _dot(xb, wgb)
            u = _dot(xb, wub)
            y = _dot((_silu(a) * u).astype(BF16), wdb)
            acc_ref[r0:r0 + rb, :] = acc_ref[r0:r0 + rb, :] + y

        @pl.when(f == nf - 1)
        def _():
            gate = gs_ref[0][:, 0:1]
            out_ref[0] = (acc_ref[...] * gate).astype(BF16)


def _ffn(xes, gss, w_gate, w_up, w_down, layer):
    n_groups = len(xes)
    _, e, d, ff = w_gate.shape
    nf = ff // FF_TILE
    in_specs = [pl.BlockSpec((1,) + x.shape[1:], lambda i, f: (i, 0, 0)) for x in xes]
    in_specs += [pl.BlockSpec((1,) + g.shape[1:], lambda i, f: (i, 0, 0)) for g in gss]
    in_specs += [pl.BlockSpec((1, 1, d, FF_TILE), lambda i, f: (layer, i, 0, f)),
                 pl.BlockSpec((1, 1, d, FF_TILE), lambda i, f: (layer, i, 0, f)),
                 pl.BlockSpec((1, 1, FF_TILE, d), lambda i, f: (layer, i, f, 0))]
    out = pl.pallas_call(
        functools.partial(_ffn_kernel, n_groups=n_groups),
        grid=(e, nf),
        in_specs=in_specs,
        out_specs=[pl.BlockSpec((1,) + x.shape[1:], lambda i, f: (i, 0, 0)) for x in xes],
        out_shape=[jax.ShapeDtypeStruct(x.shape, BF16) for x in xes],
        scratch_shapes=[pltpu.VMEM(x.shape[1:], F32) for x in xes],
        compiler_params=_cparams(2),
        name="expert_ffn",
    )(*xes, *gss, w_gate, w_up, w_down)
    return list(out)


def _combine_kernel(*refs, cap, final):
    if final:
        post_ref, yg_ref, x_ref, mod_ref, fw_ref, o_ref = refs
    else:
        post_ref, yg_ref, x_ref, mod_ref, o_ref = refs
    d = D_MODEL
    post = post_ref[0]
    tn = post.shape[0]
    slot = lax.broadcasted_iota(jnp.int32, (tn, cap), 1).astype(F32)
    acc = jnp.zeros((tn, d), F32)
    for e in range(N_EXPERTS):
        onehot = jnp.where(post[:, e:e + 1] == slot, 1.0, 0.0).astype(BF16)
        acc = acc + _dot(onehot, yg_ref[e])
    x2 = x_ref[0] + mod_ref[0][:, 5 * d:6 * d] * acc
    if final:
        ms = jnp.mean(x2 * x2, axis=-1, keepdims=True)
        o_ref[0] = x2 * lax.rsqrt(ms + EPS) * fw_ref[...]
    else:
        o_ref[0] = x2


def _combine(post_b, yg, xs, mod, cap, row0, n, mod_row_ctx, final_w=None):
    b, ts, d = xs.shape
    tn = min(SEG, n)
    blk0 = row0 // tn
    final = final_w is not None
    in_specs = [pl.BlockSpec((1, tn, LANES), lambda i, j: (i, j, 0)),
                pl.BlockSpec((N_EXPERTS, cap, d), lambda i, j: (0, i, 0)),
                pl.BlockSpec((1, tn, d), lambda i, j: (i, j + blk0, 0)),
                pl.BlockSpec((1, 1, mod.shape[-1]), lambda i, j: (8 if mod_row_ctx else i, 0, 0))]
    args = [post_b, yg, xs, mod]
    if final:
        in_specs.append(pl.BlockSpec(final_w.shape, lambda i, j: (0, 0)))
        args.append(final_w)
        out_spec = pl.BlockSpec((1, tn, d), lambda i, j: (i, j, 0))
        out_shape = jax.ShapeDtypeStruct((b, n, d), F32)
        aliases = {}
    else:
        out_spec = pl.BlockSpec((1, tn, d), lambda i, j: (i, j + blk0, 0))
        out_shape = jax.ShapeDtypeStruct((b, ts, d), F32)
        aliases = {2: 0}
    return pl.pallas_call(
        functools.partial(_combine_kernel, cap=cap, final=final),
        grid=(b, n // tn),
        in_specs=in_specs,
        out_specs=out_spec,
        out_shape=out_shape,
        input_output_aliases=aliases,
        compiler_params=_cparams(2),
        name="expert_combine_final" if final else "expert_combine",
    )(*args)


def _rope_tables(n_lat, lc):
    n_rows = n_lat // GRID_W
    rows = jnp.repeat(jnp.arange(n_rows, dtype=F32), GRID_W)
    cols = jnp.tile(jnp.arange(GRID_W, dtype=F32), n_rows)
    n_freq = A_HEAD_DIM // 4
    inv_freq = ROPE_THETA ** (-jnp.arange(n_freq, dtype=F32) / n_freq)
    ang = jnp.concatenate([rows[:, None] * inv_freq, cols[:, None] * inv_freq], axis=-1)
    c, s = jnp.cos(ang), jnp.sin(ang)
    cos = jnp.concatenate([c, c, c, c], axis=-1)
    sin = jnp.concatenate([-s, -s, s, s], axis=-1)
    cos = jnp.concatenate([jnp.ones((lc, LANES), F32), cos], axis=0)
    sin = jnp.concatenate([jnp.zeros((lc, LANES), F32), sin], axis=0)
    return cos, sin


def _pair_rope_layout(w):
    d, n = w.shape
    q = LANES // 4
    return w.reshape(d, n // LANES, 2, 2, q).transpose(0, 1, 3, 2, 4).reshape(d, n)


def _route_layout(aff, row0, n, b):
    a = aff[:, row0:row0 + n, :N_EXPERTS]
    a = jnp.transpose(a, (1, 0, 2)).reshape(n, b * N_EXPERTS)
    return jnp.pad(a, ((0, 0), (0, LANES - b * N_EXPERTS)))


def _sample_layout(post, n, b):
    p = post[:, :b * N_EXPERTS].reshape(n, b, N_EXPERTS)
    p = jnp.transpose(p, (1, 0, 2))
    return jnp.pad(p, ((0, 0), (0, 0), (0, LANES - N_EXPERTS)), constant_values=-1.0)


def kernel(x, c, ctx, c_ctx, w_ada, b_ada, norm1_w, norm2_w, w_in, mlstm_conv_w, mlstm_gate_b, mlstm_norm_w,
           diff_lambda, diff_subln_w, gqa_qnorm_w, gqa_knorm_w, w_branch_a, w_branch_b, w_branch_c, w_out,
           w_router, w_exp_gate, w_exp_up, w_exp_down, final_norm_w):
    b, n_lat, d = x.shape
    lc = ctx.shape[1]
    depth = w_ada.shape[0]
    assert d == D_MODEL and b * N_EXPERTS <= LANES and b <= 8
    ts = lc + n_lat
    assert lc % SEG == 0 and n_lat % SEG == 0 and lc % TQ == 0 and (b * ts) % TM == 0

    xs = jnp.concatenate([ctx, x], axis=1)
    cvec = jnp.zeros((16, d), F32).at[:b].set(c).at[8].set(c_ctx)
    mods = _ada(cvec, w_ada, b_ada)
    cos, sin = _rope_tables(n_lat, lc)
    cap_lat = EC_CAPACITY_FACTOR * n_lat // N_EXPERTS
    cap_ctx = EC_CAPACITY_FACTOR * lc // N_EXPERTS
    out = None

    for layer in range(depth):
        with_ctx = layer < depth - 1
        mod = mods[layer].reshape(16, 1, 6 * d)
        wl = w_in[layer]
        bq0 = 3 * MIX_W
        w_bq = wl[:, bq0:bq0 + MIX_W].reshape(d, B_KV_HEADS, B_GROUP, B_HEAD_DIM).transpose(0, 2, 1, 3).reshape(d, MIX_W)
        kb0 = bq0 + MIX_W
        w_main = jnp.concatenate([_pair_rope_layout(wl[:, :2 * MIX_W]), wl[:, 2 * MIX_W:bq0], _pair_rope_layout(w_bq),
                                  _pair_rope_layout(wl[:, kb0:kb0 + KV_B]), wl[:, kb0 + KV_B:MAIN_COLS]], axis=1).astype(BF16)
        w_bb = w_branch_b[layer].reshape(B_KV_HEADS, B_GROUP, B_HEAD_DIM, d).transpose(1, 0, 2, 3).reshape(MIX_W, d)
        w_gates = wl[:, GATE_COL0:GATE_COL0 + N_GATES]
        wg = jnp.pad(w_gates, ((0, 0), (0, LANES - N_GATES))).astype(BF16)
        wgt = w_gates.T.astype(BF16)
        gb = jnp.pad(mlstm_gate_b[layer], (0, LANES - N_GATES)).reshape(1, LANES)
        gbt = mlstm_gate_b[layer].reshape(N_GATES, 1)
        n1 = norm1_w[layer].reshape(1, d)
        n2 = norm2_w[layer].reshape(1, d)
        qnw = _pair_rope_layout(jnp.tile(gqa_qnorm_w[layer], LANES // B_HEAD_DIM).reshape(1, LANES))
        knw = _pair_rope_layout(jnp.tile(gqa_knorm_w[layer], LANES // B_HEAD_DIM).reshape(1, LANES))

        flat = _inproj(xs.reshape(b * ts, d), mod, n1, w_main, wg, wgt, gb, gbt, cos, sin, qnw, knw, ts, lc)
        qa, ka, va, qb, kb, vb, qkc, vc, oc, g = [a.reshape(b, ts, a.shape[-1]) for a in flat[:-1]]
        gt = flat[-1]

        lam_init = 0.8 - 0.6 * math.exp(-0.3 * layer)
        oa = _diff_attn(diff_lambda[layer], diff_subln_w[layer].reshape(1, 2 * A_HEAD_DIM),
                        qa, ka, va, lam_init, lc, with_ctx)
        ob = _gqa_attn(qb, kb, vb, lc, with_ctx)
        ocm = _mlstm(qkc, vc, oc, g, gt, mlstm_conv_w[layer], mlstm_norm_w[layer], lc)

        wm = wl[:, MERGE_COL0:].astype(BF16)
        wr = jnp.pad(w_router[layer], ((0, 0), (0, LANES - N_EXPERTS)))
        rows2 = lambda a: a.reshape(b * ts, a.shape[-1])
        xs, h2, aff = _merge(rows2(xs), mod, n1, n2, rows2(oa), rows2(ob), rows2(ocm), wm,
                             w_branch_a[layer].astype(BF16), w_bb.astype(BF16),
                             w_branch_c[layer].astype(BF16), w_out[layer].astype(BF16), wr, ts, lc, with_ctx)
        xs, h2, aff = [a.reshape(b, ts, a.shape[-1]) for a in (xs, h2, aff)]

        groups = [(lc, n_lat, cap_lat)]
        if with_ctx:
            groups.append((0, lc, cap_ctx))
        xes, gss, posts = [], [], []
        for row0, n, cap in groups:
            post, posr, affr = _route(_route_layout(aff, row0, n, b), cap)
            xe, gs = _gather(posr, affr, h2, cap, row0, n, b)
            xes.append(xe)
            gss.append(gs)
            posts.append(_sample_layout(post, n, b))
        ygs = _ffn(xes, gss, w_exp_gate, w_exp_up, w_exp_down, layer)
        for gi, (row0, n, cap) in enumerate(groups):
            is_last = (layer == depth - 1) and gi == 0
            res = _combine(posts[gi], ygs[gi], xs, mod, cap, row0, n, mod_row_ctx=(row0 == 0),
                           final_w=final_norm_w.reshape(1, d) if is_last else None)
            if is_last:
                out = res
            else:
                xs = res
    return out
```

```python
import functools
import math

import jax
import jax.numpy as jnp
from jax import lax
from jax.experimental import pallas as pl
from jax.experimental.pallas import tpu as pltpu

F32 = jnp.float32
BF16 = jnp.bfloat16

D_MODEL = 1024
DEPTH = 2
GRID_W = 64
ROPE_THETA = 10000.0
EPS = 1e-6
NEG_BIG = -1e30
MIX_W = D_MODEL // 2
A_HEAD_DIM = 64
A_HEADS = MIX_W // (2 * A_HEAD_DIM)
B_HEAD_DIM = 64
B_Q_HEADS = MIX_W // B_HEAD_DIM
B_KV_HEADS = 2
B_GROUP = B_Q_HEADS // B_KV_HEADS
C_HEAD_DIM = 128
C_HEADS = MIX_W // C_HEAD_DIM
C_CONV = 3
C_CHUNK = 128
N_BRANCH = 3
N_EXPERTS = 16
EXPERT_FF = 2 * D_MODEL
EC_CAPACITY_FACTOR = 2

LANES = 128
KV_B = B_KV_HEADS * B_HEAD_DIM
N_GATES = 4 * C_HEADS
MAIN_COLS = 8 * MIX_W + 2 * KV_B
GATE_COL0 = MAIN_COLS
MERGE_COL0 = MAIN_COLS + N_GATES
TM = 512
SEG = 256
TQ = 256
FF_TILE = 512
FFN_ROWS = 512
GATHER_EXPERTS = 4
SMALL_KEYS = 512
VMEM_LIMIT = 56 * 1024 * 1024
SCORE_SCALE = (A_HEAD_DIM ** -0.5) * math.log2(math.e)


def _cparams(n_axes, vmem=VMEM_LIMIT):
    return pltpu.CompilerParams(dimension_semantics=("arbitrary",) * n_axes, vmem_limit_bytes=vmem)


def _dot(a, b):
    return jnp.dot(a, b, preferred_element_type=F32)


def _dot_nt(a, b):
    return lax.dot_general(a, b, (((1,), (1,)), ((), ())), preferred_element_type=F32)


def _split3(x):
    a = x.astype(BF16)
    r = x - a.astype(F32)
    b = r.astype(BF16)
    c = (r - b.astype(F32)).astype(BF16)
    return a, b, c


def _sigmoid(x):
    return 1.0 / (1.0 + jnp.exp(-x))


def _silu(x):
    return x * _sigmoid(x)


def _log_sigmoid(x):
    return jnp.minimum(x, 0.0) - jnp.log(1.0 + jnp.exp(-jnp.abs(x)))


def _norm_mod(x, nw, shift, scale):
    ms = jnp.mean(x * x, axis=-1, keepdims=True)
    return (x * lax.rsqrt(ms + EPS) * nw) * (1.0 + scale) + shift


def _ada_kernel(c_ref, w_ref, b_ref, o_ref):
    s = _silu(c_ref[...])
    s1, s2, _ = _split3(s)
    w = w_ref[0]
    w1, w2, _ = _split3(w)
    o_ref[0] = _dot(s1, w1) + _dot(s1, w2) + _dot(s2, w1) + b_ref[0]


def _ada(cvec, w_ada, b_ada):
    depth, d, n = w_ada.shape
    tn = 1536
    return pl.pallas_call(
        _ada_kernel,
        grid=(depth, n // tn),
        in_specs=[pl.BlockSpec((16, d), lambda l, j: (0, 0)),
                  pl.BlockSpec((1, d, tn), lambda l, j: (l, 0, j)),
                  pl.BlockSpec((1, 1, tn), lambda l, j: (l, 0, j))],
        out_specs=pl.BlockSpec((1, 16, tn), lambda l, j: (l, 0, j)),
        out_shape=jax.ShapeDtypeStruct((depth, 16, n), F32),
        compiler_params=_cparams(2),
        name="ada_mod",
    )(cvec, w_ada, b_ada.reshape(depth, 1, n))


def _inproj_kernel(x_ref, mod0_ref, mod1_ref, nw_ref, w_ref, wg_ref, wgt_ref, gb_ref, gbt_ref,
                   cos0_ref, cos1_ref, sin0_ref, sin1_ref, qnw_ref, knw_ref,
                   qa_ref, ka_ref, va_ref, qb_ref, kb_ref, vb_ref, qkc_ref, vc_ref, oc_ref, g_ref, gt_ref):
    d = D_MODEL
    tm = x_ref.shape[0]
    hs = []
    for half, mod_ref in enumerate((mod0_ref, mod1_ref)):
        mod = mod_ref[0]
        x = x_ref[half * SEG:(half + 1) * SEG, :]
        hs.append(_norm_mod(x, nw_ref[...], mod[:, 0:d], mod[:, d:2 * d]).astype(BF16))
    h = jnp.concatenate(hs, axis=0)
    cos = jnp.concatenate([cos0_ref[...], cos1_ref[...]], axis=0)
    sin = jnp.concatenate([sin0_ref[...], sin1_ref[...]], axis=0)
    lane = lax.broadcasted_iota(jnp.int32, (tm, LANES), 1)

    def rope(p):
        return p * cos + pltpu.roll(p, LANES // 2, 1) * sin

    first_head = _first_head_lanes((tm, LANES))

    def head_inv_rms(p):
        sq = p * p
        sa = jnp.sum(jnp.where(first_head, sq, 0.0), axis=-1, keepdims=True)
        sb = jnp.sum(jnp.where(first_head, 0.0, sq), axis=-1, keepdims=True)
        return lax.rsqrt(jnp.where(first_head, sa, sb) * (1.0 / B_HEAD_DIM) + EPS)

    def proj(c0):
        return _dot(h, w_ref[:, c0:c0 + 2 * LANES])

    def halves(p):
        return p[:, :LANES], p[:, LANES:]

    pk, pv = halves(proj(2048))
    vb_ref[...] = pv.astype(BF16)
    chunks = [p for j in range(2) for p in halves(proj(1536 + j * 256))] + [pk]
    for j in range(4):
        qkc_ref[:, j * 256:(j + 1) * 256] = proj(2304 + j * 256)
    qnw = qnw_ref[...]
    for i, p in enumerate(chunks[:-1]):
        qb_ref[:, i * LANES:(i + 1) * LANES] = (rope(p * head_inv_rms(p) * qnw) * SCORE_SCALE).astype(BF16)
    kb_ref[...] = rope(pk * head_inv_rms(pk) * knw_ref[...]).astype(BF16)
    for j in range(2):
        for half, p in enumerate(halves(proj(j * 256))):
            c = j * 256 + half * LANES
            qa_ref[:, c:c + LANES] = (rope(p) * SCORE_SCALE).astype(BF16)
    for j in range(2):
        for half, p in enumerate(halves(proj(512 + j * 256))):
            c = j * 256 + half * LANES
            ka_ref[:, c:c + LANES] = rope(p).astype(BF16)
    for j in range(2):
        va_ref[:, j * 256:(j + 1) * 256] = proj(1024 + j * 256).astype(BF16)
    for j in range(2):
        vc_ref[:, j * 256:(j + 1) * 256] = proj(3328 + j * 256).astype(BF16)
    for j in range(2):
        oc_ref[:, j * 256:(j + 1) * 256] = _sigmoid(proj(3840 + j * 256)).astype(BF16)
    g = _dot(h, wg_ref[...]) + gb_ref[...]
    is_f = ((lane % 8) >= 4) & (lane < N_GATES)
    g_ref[...] = jnp.where(is_f, _log_sigmoid(g), g)
    gt = _dot_nt(wgt_ref[...], h) + gbt_ref[...]
    row = lax.broadcasted_iota(jnp.int32, (N_GATES, tm), 0)
    gt_ref[...] = jnp.where((row % 8) >= 4, _log_sigmoid(gt), gt)


def _seg_rows(k, half, segs_per_sample, n_ctx_segs):
    seg = 2 * k + half
    sample = seg // segs_per_sample
    within = seg % segs_per_sample
    return jnp.where(within < n_ctx_segs, 8, sample), within


def _mod_specs(width, segs_per_sample, n_ctx_segs):
    return [pl.BlockSpec((1, 1, width), lambda k, h=half: (_seg_rows(k, h, segs_per_sample, n_ctx_segs)[0], 0, 0))
            for half in range(2)]


def _segment_of_step(segs_per_sample, n_ctx_segs, with_ctx):
    if with_ctx:
        return lambda k: k
    n_lat = segs_per_sample - n_ctx_segs
    return lambda k: (k // n_lat) * segs_per_sample + n_ctx_segs + k % n_lat


def _mod_spec_one_seg(width, segs_per_sample, n_ctx_segs, seg_of):
    def index(k):
        seg = seg_of(k)
        return jnp.where(seg % segs_per_sample < n_ctx_segs, 8, seg // segs_per_sample), 0, 0
    return pl.BlockSpec((1, 1, width), index)


def _inproj(x2, mod, nw, w_main, wg, wgt, gb, gbt, cos, sin, qnw, knw, ts, lc):
    rows, d = x2.shape
    sps = ts // SEG
    ncs = lc // SEG
    tok = lambda width: pl.BlockSpec((TM, width), lambda k: (k, 0))
    full = lambda a: pl.BlockSpec(a.shape, lambda k: (0,) * a.ndim)
    table = [pl.BlockSpec((SEG, LANES), lambda k, h=half: (_seg_rows(k, h, sps, ncs)[1], 0)) for half in range(2)]
    outs = [(MIX_W, BF16), (MIX_W, BF16), (MIX_W, BF16), (MIX_W, BF16), (KV_B, BF16), (KV_B, BF16),
            (2 * MIX_W, F32), (MIX_W, BF16), (MIX_W, BF16), (LANES, F32)]
    out_shape = [jax.ShapeDtypeStruct((rows, w), dt) for w, dt in outs]
    out_specs = [tok(w) for w, _ in outs]
    out_shape.append(jax.ShapeDtypeStruct((N_GATES, rows), F32))
    out_specs.append(pl.BlockSpec((N_GATES, TM), lambda k: (0, k)))
    return pl.pallas_call(
        _inproj_kernel,
        grid=(rows // TM,),
        in_specs=[tok(d)] + _mod_specs(mod.shape[-1], sps, ncs)
        + [full(nw), full(w_main), full(wg), full(wgt), full(gb), full(gbt)]
        + table + table + [full(qnw), full(knw)],
        out_specs=out_specs,
        out_shape=out_shape,
        compiler_params=_cparams(1),
        name="in_proj",
    )(x2, mod, mod, nw, w_main, wg, wgt, gb, gbt, cos, cos, sin, sin, qnw, knw)


def _first_head_lanes(shape):
    lane = lax.broadcasted_iota(jnp.int32, shape, 1)
    return (lane // (LANES // 4)) % 2 == 0


def _softmax_numerators(sb, eb, rows, lk):
    maxes = [jnp.max(sb[r:r + 8, :lk], axis=-1, keepdims=True) for r in range(0, rows, 8)]
    for rb in range(rows // 16):
        parts = [jnp.exp2(sb[rb * 16 + sub * 8:rb * 16 + sub * 8 + 8, :lk] - maxes[2 * rb + sub]) for sub in range(2)]
        eb[rb * 16:(rb + 1) * 16, :lk] = jnp.concatenate(parts, axis=0).astype(BF16)


def _with_ones_column(v):
    lane = lax.broadcasted_iota(jnp.int32, v.shape, 1)
    ones = jnp.where(lane == 0, 1.0, 0.0).astype(v.dtype)
    return jnp.concatenate([v, ones], axis=1)


def _attention_units(n_units, scores, finish, values, s_s, e_s, rows, lk):
    if lk <= SMALL_KEYS:
        for u in range(n_units):
            s = scores(u)
            e = jnp.exp2(s - jnp.max(s, axis=-1, keepdims=True))
            finish(u, _dot(e.astype(BF16), values(u)))
        return
    s_s[0, :, :lk] = scores(0)
    for u in range(n_units):
        if u + 1 < n_units:
            s_s[(u + 1) % 2, :, :lk] = scores(u + 1)
        _softmax_numerators(s_s.at[u % 2], e_s.at[u % 2], rows, lk)
        finish(u, _dot(e_s[u % 2, :, :lk], values(u)))


def _diff_attn_kernel(lam_ref, sub_ref, q_ref, k_ref, v_ref, o_ref, s_s, e_s, *, lam_init, n_ctx_blocks, with_ctx, lc):
    qi = pl.program_id(1)
    lv = lam_ref[...]
    lam = (jnp.exp(jnp.sum(lv[0:1] * lv[1:2], axis=-1, keepdims=True))
           - jnp.exp(jnp.sum(lv[2:3] * lv[3:4], axis=-1, keepdims=True)) + lam_init)
    tq = q_ref.shape[1]
    w = 2 * A_HEAD_DIM

    def body(lk):
        low = _first_head_lanes((tq, w))

        def scores(h):
            cols = slice(h * w, (h + 1) * w)
            q = q_ref[0, :, cols]
            zero = jnp.zeros_like(q)
            qs = jnp.concatenate([jnp.where(low, q, zero), jnp.where(low, zero, q)], axis=0)
            return _dot_nt(qs, k_ref[0, :lk, cols])

        def values(h):
            return _with_ones_column(v_ref[0, :lk, h * w:(h + 1) * w])

        def finish(h, pv):
            o = (pv[:tq, :w] * (1.0 / pv[:tq, w:w + 1])
                 - pv[tq:, :w] * (lam / pv[tq:, w:w + 1]))
            ms = jnp.mean(o * o, axis=-1, keepdims=True)
            o_ref[0, :, h * w:(h + 1) * w] = (o * lax.rsqrt(ms + EPS) * sub_ref[...] * (1.0 - lam_init)).astype(BF16)

        _attention_units(A_HEADS, scores, finish, values, s_s, e_s, 2 * tq, lk)

    @pl.when(qi < n_ctx_blocks)
    def _():
        if with_ctx:
            body(lc)
        else:
            o_ref[...] = jnp.zeros_like(o_ref)

    @pl.when(qi >= n_ctx_blocks)
    def _():
        body(k_ref.shape[1])


def _diff_attn(lam_vecs, sub_w, qa, ka, va, lam_init, lc, with_ctx):
    b, ts, _ = qa.shape
    n_ctx_blocks = lc // TQ
    nq = ts // TQ
    kern = functools.partial(_diff_attn_kernel, lam_init=lam_init, n_ctx_blocks=n_ctx_blocks, with_ctx=with_ctx, lc=lc)
    return pl.pallas_call(
        kern,
        grid=(b, nq),
        in_specs=[pl.BlockSpec(lam_vecs.shape, lambda i, j: (0, 0)),
                  pl.BlockSpec(sub_w.shape, lambda i, j: (0, 0)),
                  pl.BlockSpec((1, TQ, MIX_W), lambda i, j: (i, j, 0)),
                  pl.BlockSpec((1, ts, MIX_W), lambda i, j: (i, 0, 0)),
                  pl.BlockSpec((1, ts, MIX_W), lambda i, j: (i, 0, 0))],
        out_specs=pl.BlockSpec((1, TQ, MIX_W), lambda i, j: (i, j, 0)),
        out_shape=jax.ShapeDtypeStruct((b, ts, MIX_W), BF16),
        scratch_shapes=[pltpu.VMEM((2, 2 * TQ, ts), F32), pltpu.VMEM((2, 2 * TQ, ts), BF16)],
        compiler_params=_cparams(2),
        name="diff_attn",
    )(lam_vecs, sub_w, qa, ka, va)


def _gqa_kernel(q_ref, k_ref, v_ref, o_ref, s_s, e_s, *, n_ctx_blocks, with_ctx, lc):
    qi = pl.program_id(1)
    tq = q_ref.shape[1]

    def body(lk):
        low = _first_head_lanes((tq, LANES))
        out_low = lax.broadcasted_iota(jnp.int32, (tq, LANES), 1) < B_HEAD_DIM
        k = k_ref[0, :lk, :]
        v1 = _with_ones_column(v_ref[0, :lk, :])
        outs = {}
        pairs = B_GROUP // 2

        def scores(u):
            g, pair = divmod(u, pairs)
            parts = []
            for j in (2 * pair, 2 * pair + 1):
                x = q_ref[0, :, j * LANES:(j + 1) * LANES]
                zero = jnp.zeros_like(x)
                parts.append(jnp.where(low, x, zero) if g == 0 else jnp.where(low, zero, x))
            return _dot_nt(jnp.concatenate(parts, axis=0), k)

        def finish(u, pv):
            g, pair = divmod(u, pairs)
            o = pv[:, :LANES] * (1.0 / pv[:, LANES:LANES + 1])
            outs[(g, 2 * pair)] = o[:tq]
            outs[(g, 2 * pair + 1)] = o[tq:]

        _attention_units(B_KV_HEADS * pairs, scores, finish, lambda u: v1, s_s, e_s, 2 * tq, lk)
        for j in range(B_GROUP):
            o_ref[0, :, j * LANES:(j + 1) * LANES] = jnp.where(out_low, outs[(0, j)], outs[(1, j)]).astype(BF16)

    @pl.when(qi < n_ctx_blocks)
    def _():
        if with_ctx:
            body(lc)
        else:
            o_ref[...] = jnp.zeros_like(o_ref)

    @pl.when(qi >= n_ctx_blocks)
    def _():
        body(k_ref.shape[1])


def _gqa_attn(qb, kb, vb, lc, with_ctx):
    b, ts, _ = qb.shape
    n_ctx_blocks = lc // TQ
    nq = ts // TQ
    kern = functools.partial(_gqa_kernel, n_ctx_blocks=n_ctx_blocks, with_ctx=with_ctx, lc=lc)
    return pl.pallas_call(
        kern,
        grid=(b, nq),
        in_specs=[pl.BlockSpec((1, TQ, MIX_W), lambda i, j: (i, j, 0)),
                  pl.BlockSpec((1, ts, KV_B), lambda i, j: (i, 0, 0)),
                  pl.BlockSpec((1, ts, KV_B), lambda i, j: (i, 0, 0))],
        out_specs=pl.BlockSpec((1, TQ, MIX_W), lambda i, j: (i, j, 0)),
        out_shape=jax.ShapeDtypeStruct((b, ts, MIX_W), BF16),
        scratch_shapes=[pltpu.VMEM((2, 2 * TQ, ts), F32), pltpu.VMEM((2, 2 * TQ, ts), BF16)],
        compiler_params=_cparams(2),
        name="gqa_attn",
    )(qb, kb, vb)


def _mlstm_kernel(q_ref, k_ref, v_ref, o_ref, g_ref, gt_ref, cwq_ref, cwk_ref, nw_ref, out_ref,
                  q_s, kt_s, bc_s, ac_s, rows_s, hacc_s, st_s, *, lc):
    ts = q_ref.shape[1]
    hp = q_ref.shape[2] // C_HEAD_DIM
    head0 = pl.program_id(1) * hp
    nc = ts // C_CHUNK
    ncc = lc // C_CHUNK
    ch = C_CHUNK

    row = lax.broadcasted_iota(jnp.int32, (ts, LANES), 0)
    prev_ok = (row != 0) & (row != lc)
    next_ok = (row != lc - 1) & (row != ts - 1)

    def conv(x, w):
        xp = jnp.where(prev_ok, pltpu.roll(x, 1, 0), 0.0)
        xn = jnp.where(next_ok, pltpu.roll(x, ts - 1, 0), 0.0)
        return _silu(xp * w[0:1] + x * w[1:2] + xn * w[2:3])

    for j in range(hp):
        cols = slice(j * LANES, (j + 1) * LANES)
        q_s[:, cols] = conv(q_ref[0, :, cols], cwq_ref[0, :, cols]).astype(BF16)
        y = conv(k_ref[0, :, cols], cwk_ref[0, :, cols]) * (C_HEAD_DIM ** -0.5)
        for c in range(nc):
            kt_s[c, cols, :] = y[c * ch:(c + 1) * ch, :].T.astype(BF16)

    ri = lax.broadcasted_iota(jnp.int32, (ch, ch), 0)
    ci = lax.broadcasted_iota(jnp.int32, (ch, ch), 1)
    lower = jnp.where(ci <= ri, 1.0, 0.0).astype(BF16)
    upper = jnp.where(ci >= ri, 1.0, 0.0).astype(BF16)
    lane = ci
    rowi = lax.broadcasted_iota(jnp.int32, (N_GATES, ch), 0)
    for c in range(nc):
        rs = slice(c * ch, (c + 1) * ch)
        g = g_ref[0, rs, :]
        g1, g2, g3 = _split3(g)
        pre = _dot(lower, g1) + _dot(lower, g2) + _dot(lower, g3)
        suf = _dot(upper, g1) + _dot(upper, g2) + _dot(upper, g3)
        gt = gt_ref[:, rs]
        t1, t2, t3 = _split3(gt)
        pre_t = _dot(t1, upper) + _dot(t2, upper) + _dot(t3, upper)
        suf_t = _dot(t1, lower) + _dot(t2, lower) + _dot(t3, lower)
        for j in range(hp):
            for direction in range(2):
                idx = j * 2 + direction
                li = head0 + j + 8 * direction
                lf = li + 4
                cum, cum_t = (pre, pre_t) if direction == 0 else (suf, suf_t)
                b_col = jnp.sum(jnp.where(lane == lf, cum, 0.0), axis=-1, keepdims=True)
                i_col = jnp.sum(jnp.where(lane == li, g, 0.0), axis=-1, keepdims=True)
                bc_s[idx, rs, :] = jnp.broadcast_to(b_col, (ch, LANES))
                ac_s[idx, rs, :] = jnp.broadcast_to(i_col - b_col, (ch, LANES))
                b_row = jnp.sum(jnp.where(rowi == lf, cum_t, 0.0), axis=0, keepdims=True)
                i_row = jnp.sum(jnp.where(rowi == li, gt, 0.0), axis=0, keepdims=True)
                rows_s[c, 2 * idx:2 * idx + 1, :] = b_row
                rows_s[c, 2 * idx + 1:2 * idx + 2, :] = i_row - b_row

    hacc_s[...] = jnp.zeros_like(hacc_s)
    st_s[...] = jnp.zeros_like(st_s)
    tri_f = ci <= ri
    tri_b = ci >= ri

    def chain(c, j, direction, m):
        idx = j * 2 + direction
        c0 = pl.multiple_of(c * ch, ch)
        cols = slice(j * LANES, (j + 1) * LANES)
        q = q_s[pl.ds(c0, ch), cols]
        kt = kt_s[c, cols, :]
        v = v_ref[0, pl.ds(c0, ch), cols]
        bc = bc_s[idx, pl.ds(c0, ch), :]
        ac = ac_s[idx, pl.ds(c0, ch), :]
        rows = rows_s[c]
        b_row = rows[2 * idx:2 * idx + 1, :]
        ib_row = rows[2 * idx + 1:2 * idx + 2, :]
        tri = tri_f if direction == 0 else tri_b
        log_d = jnp.where(tri, bc + ib_row, NEG_BIG)
        m_intra = jnp.max(log_d, axis=-1, keepdims=True)
        log_inter = bc + m
        m_t = jnp.maximum(log_inter, m_intra)
        dm = jnp.exp(log_d - m_t)
        w_inter = jnp.exp(log_inter - m_t)
        s = _dot(q, kt) * dm
        st = st_s[idx]
        inter = _dot(q, st.astype(BF16))
        num = _dot(s.astype(BF16), v) + w_inter * inter[:, :LANES]
        den = jnp.sum(s, axis=-1, keepdims=True) + w_inter * inter[:, LANES:LANES + 1]
        hout = num / jnp.maximum(jnp.abs(den), jnp.exp(-m_t))
        hacc_s[pl.ds(c0, ch), cols] = hacc_s[pl.ds(c0, ch), cols] + hout
        total = b_row[:, ch - 1:ch] if direction == 0 else b_row[:, 0:1]
        m_new = jnp.maximum(total + m, jnp.max(total + ib_row, axis=-1, keepdims=True))
        w = jnp.exp(total + ac - m_new)
        decay = jnp.exp(total + m - m_new)
        wv = jnp.concatenate([w * v.astype(F32), jnp.where(lane == 0, w, 0.0)], axis=1).astype(BF16)
        st_s[idx] = decay * st + _dot(kt, wv)
        return m_new

    def step(i, ms):
        c_f = i
        c_b = jnp.where(i < ncc, ncc - 1 - i, nc + ncc - 1 - i)
        out = []
        for j in range(hp):
            out.append(chain(c_f, j, 0, ms[j * 2]))
            out.append(chain(c_b, j, 1, ms[j * 2 + 1]))
        return tuple(out)

    lax.fori_loop(0, nc, step, tuple(jnp.zeros((1, 1), F32) for _ in range(2 * hp)), unroll=2)

    for j in range(hp):
        cols = slice(j * LANES, (j + 1) * LANES)
        x = hacc_s[:, cols]
        ms = jnp.mean(x * x, axis=-1, keepdims=True)
        y = x * lax.rsqrt(ms + EPS) * nw_ref[0, :, cols]
        out_ref[0, :, cols] = (o_ref[0, :, cols].astype(F32) * y).astype(BF16)


def _mlstm(qkc, vc, oc, g, gt, conv_w, norm_w, lc, heads_per_step=2):
    b, ts, _ = vc.shape
    wq = heads_per_step * C_HEAD_DIM
    nhp = C_HEADS // heads_per_step
    nc = ts // C_CHUNK
    cw = conv_w.reshape(C_CONV, 2 * nhp, wq).transpose(1, 0, 2)
    nw = norm_w.reshape(1, nhp, wq).transpose(1, 0, 2)
    kern = functools.partial(_mlstm_kernel, lc=lc)
    tokw = lambda off: pl.BlockSpec((1, ts, wq), lambda i, p: (i, 0, p + off))
    return pl.pallas_call(
        kern,
        grid=(b, nhp),
        in_specs=[tokw(0), tokw(nhp), tokw(0), tokw(0),
                  pl.BlockSpec((1, ts, LANES), lambda i, p: (i, 0, 0)),
                  pl.BlockSpec((N_GATES, ts), lambda i, p: (0, i)),
                  pl.BlockSpec((1, C_CONV, wq), lambda i, p: (p, 0, 0)),
                  pl.BlockSpec((1, C_CONV, wq), lambda i, p: (p + nhp, 0, 0)),
                  pl.BlockSpec((1, 1, wq), lambda i, p: (p, 0, 0))],
        out_specs=tokw(0),
        out_shape=jax.ShapeDtypeStruct((b, ts, MIX_W), BF16),
        scratch_shapes=[pltpu.VMEM((ts, wq), BF16),
                        pltpu.VMEM((nc, wq, C_CHUNK), BF16),
                        pltpu.VMEM((2 * heads_per_step, ts, LANES), F32),
                        pltpu.VMEM((2 * heads_per_step, ts, LANES), F32),
                        pltpu.VMEM((nc, 4 * heads_per_step, C_CHUNK), F32),
                        pltpu.VMEM((ts, wq), F32),
                        pltpu.VMEM((2 * heads_per_step, C_HEAD_DIM, 2 * LANES), F32)],
        compiler_params=_cparams(2),
        name="mlstm",
    )(qkc, qkc, vc, oc, g, gt, cw, cw, nw)


def _merge_kernel(x_ref, mod_ref, n1_ref, n2_ref, oa_ref, ob_ref, oc_ref, wm_ref, wa_ref, wb_ref, wc_ref,
                  wo_ref, wr_ref, xo_ref, h2_ref, aff_ref):
    d = D_MODEL
    mod = mod_ref[0]
    x = x_ref[...]
    h = _norm_mod(x, n1_ref[...], mod[:, 0:d], mod[:, d:2 * d]).astype(BF16)
    merged = (_sigmoid(_dot(h, wm_ref[:, 0:d])) * _dot(oa_ref[...], wa_ref[...])
              + _sigmoid(_dot(h, wm_ref[:, d:2 * d])) * _dot(ob_ref[...], wb_ref[...])
              + _sigmoid(_dot(h, wm_ref[:, 2 * d:3 * d])) * _dot(oc_ref[...], wc_ref[...]))
    y = _dot(merged.astype(BF16), wo_ref[...])
    x1 = x + mod[:, 2 * d:3 * d] * y
    xo_ref[...] = x1
    h2 = _norm_mod(x1, n2_ref[...], mod[:, 3 * d:4 * d], mod[:, 4 * d:5 * d])
    h2b = h2.astype(BF16)
    h2_ref[...] = h2b
    h2l = (h2 - h2b.astype(F32)).astype(BF16)
    wr = wr_ref[...]
    wrh = wr.astype(BF16)
    wrl = (wr - wrh.astype(F32)).astype(BF16)
    logits = _dot(h2b, wrh) + _dot(h2b, wrl) + _dot(h2l, wrh)
    lane = lax.broadcasted_iota(jnp.int32, logits.shape, 1)
    valid = lane < N_EXPERTS
    logits = jnp.where(valid, logits, NEG_BIG)
    e = jnp.where(valid, jnp.exp(logits - jnp.max(logits, axis=-1, keepdims=True)), 0.0)
    aff_ref[...] = e / jnp.sum(e, axis=-1, keepdims=True)


def _merge(x2, mod, n1, n2, oa, ob, oc, wm, wa, wb, wc, wo, wr, ts, lc, with_ctx):
    rows, d = x2.shape
    sps, ncs = ts // SEG, lc // SEG
    seg_of = _segment_of_step(sps, ncs, with_ctx)
    n_steps = rows // SEG if with_ctx else (rows // ts) * (sps - ncs)
    tok = lambda width: pl.BlockSpec((SEG, width), lambda k: (seg_of(k), 0))
    full = lambda a: pl.BlockSpec(a.shape, lambda k: (0,) * a.ndim)
    return pl.pallas_call(
        _merge_kernel,
        grid=(n_steps,),
        in_specs=[tok(d), _mod_spec_one_seg(mod.shape[-1], sps, ncs, seg_of),
                  full(n1), full(n2), tok(MIX_W), tok(MIX_W), tok(MIX_W),
                  full(wm), full(wa), full(wb), full(wc), full(wo), full(wr)],
        out_specs=[tok(d), tok(d), tok(LANES)],
        out_shape=[jax.ShapeDtypeStruct((rows, d), F32),
                   jax.ShapeDtypeStruct((rows, d), BF16),
                   jax.ShapeDtypeStruct((rows, LANES), F32)],
        input_output_aliases={0: 0},
        compiler_params=_cparams(1),
        name="merge_out",
    )(x2, mod, n1, n2, oa, ob, oc, wm, wa, wb, wc, wo, wr)


def _route_kernel(aff_ref, post_ref, posr_ref, affr_ref, *, cap, row0):
    n = post_ref.shape[0]
    ch = LANES
    aff = aff_ref[0, row0:row0 + n, :]
    for bi in range(1, aff_ref.shape[0]):
        aff = aff + pltpu.roll(aff_ref[bi, row0:row0 + n, :], bi * N_EXPERTS, 1)

    def step(i, thr_bits):
        cand = thr_bits | jnp.left_shift(jnp.int32(1), 30 - i)
        cnt = jnp.sum((aff >= pltpu.bitcast(cand, F32)).astype(jnp.int32), axis=0, keepdims=True)
        return jnp.where(cnt >= cap, cand, thr_bits)

    thr = pltpu.bitcast(lax.fori_loop(0, 31, step, jnp.zeros((1, LANES), jnp.int32)), F32)
    gt = aff > thr
    eq = aff == thr
    need = cap - jnp.sum(gt.astype(jnp.int32), axis=0, keepdims=True)

    ri = lax.broadcasted_iota(jnp.int32, (ch, ch), 0)
    ci = lax.broadcasted_iota(jnp.int32, (ch, ch), 1)
    strict_lower = jnp.where(ci < ri, 1.0, 0.0).astype(BF16)

    def excl_cumsum(mask_f):
        carry = jnp.zeros((1, LANES), F32)
        blocks = []
        for c in range(n // ch):
            blk = mask_f[c * ch:(c + 1) * ch, :]
            blocks.append(_dot(strict_lower, blk.astype(BF16)) + carry)
            carry = carry + jnp.sum(blk, axis=0, keepdims=True)
        return jnp.concatenate(blocks, axis=0)

    eq_rank = excl_cumsum(jnp.where(eq, 1.0, 0.0))
    sel = gt | (eq & (eq_rank < need.astype(F32)))
    pos = excl_cumsum(jnp.where(sel, 1.0, 0.0))
    post = jnp.where(sel, pos, -1.0)
    post_ref[...] = post
    for c in range(n // ch):
        posr_ref[:, c * ch:(c + 1) * ch] = post[c * ch:(c + 1) * ch, :].T
        affr_ref[:, c * ch:(c + 1) * ch] = aff[c * ch:(c + 1) * ch, :].T


def _route(aff, cap, row0, n):
    return pl.pallas_call(
        functools.partial(_route_kernel, cap=cap, row0=row0),
        out_shape=[jax.ShapeDtypeStruct((n, LANES), F32),
                   jax.ShapeDtypeStruct((LANES, n), F32),
                   jax.ShapeDtypeStruct((LANES, n), F32)],
        compiler_params=pltpu.CompilerParams(vmem_limit_bytes=VMEM_LIMIT),
        name="route",
    )(aff)


def _gather_kernel(posr_ref, affr_ref, h_ref, xe_ref, gs_ref, *, cap, row0, n):
    ne = posr_ref.shape[0]
    slot = lax.broadcasted_iota(jnp.int32, (cap, n), 0).astype(F32)
    onehots = []
    for t in range(ne):
        pf = jnp.where(posr_ref[t] == slot, 1.0, 0.0)
        gs = jnp.sum(pf * affr_ref[t], axis=-1, keepdims=True)
        gs_ref[t] = jnp.broadcast_to(gs, (cap, LANES))
        onehots.append(pf.astype(BF16))
    xe = _dot(jnp.concatenate(onehots, axis=0), h_ref[0, row0:row0 + n, :])
    for t in range(ne):
        xe_ref[t] = xe[t * cap:(t + 1) * cap].astype(BF16)


def _gather(posr, affr, h2, cap, row0, n, b):
    ts, d = h2.shape[1:]
    be = posr.shape[0]
    posr3 = posr.reshape(be, 1, n)
    affr3 = affr.reshape(be, 1, n)
    e = N_EXPERTS
    ge = GATHER_EXPERTS
    return pl.pallas_call(
        functools.partial(_gather_kernel, cap=cap, row0=row0, n=n),
        grid=(b, e // ge),
        in_specs=[pl.BlockSpec((ge, 1, n), lambda i, j: (i * (e // ge) + j, 0, 0)),
                  pl.BlockSpec((ge, 1, n), lambda i, j: (i * (e // ge) + j, 0, 0)),
                  pl.BlockSpec((1, ts, d), lambda i, j: (i, 0, 0))],
        out_specs=[pl.BlockSpec((ge, cap, d), lambda i, j: (j, i, 0)),
                   pl.BlockSpec((ge, cap, LANES), lambda i, j: (j, i, 0))],
        out_shape=[jax.ShapeDtypeStruct((e, b * cap, d), BF16),
                   jax.ShapeDtypeStruct((e, b * cap, LANES), F32)],
        compiler_params=_cparams(2),
        name="expert_gather",
    )(posr3, affr3, h2)


def _ffn_kernel(*refs, n_groups):
    xe_refs = refs[0:n_groups]
    gs_refs = refs[n_groups:2 * n_groups]
    wg_ref, wu_ref, wd_ref = refs[2 * n_groups:2 * n_groups + 3]
    out_refs = refs[2 * n_groups + 3:3 * n_groups + 3]
    acc_refs = refs[3 * n_groups + 3:4 * n_groups + 3]
    f = pl.program_id(1)
    nf = pl.num_programs(1)
    wgb = wg_ref[0, 0].astype(BF16)
    wub = wu_ref[0, 0].astype(BF16)
    wdb = wd_ref[0, 0].astype(BF16)
    @pl.when(f == 0)
    def _():
        for acc_ref in acc_refs:
            acc_ref[...] = jnp.zeros_like(acc_ref)

    total = sum(r.shape[1] for r in xe_refs)
    n_blocks = max(1, total // FFN_ROWS)
    assert total % (16 * n_blocks) == 0
    rb = total // n_blocks
    for blk in range(n_blocks):
        pieces, start = [], 0
        for gi, r in enumerate(xe_refs):
            lo, hi = max(blk * rb, start), min((blk + 1) * rb, start + r.shape[1])
            if lo < hi:
                pieces.append((gi, lo - start, hi - lo))
            start += r.shape[1]
        xb = jnp.concatenate([xe_refs[gi][0, r0:r0 + n, :] for gi, r0, n in pieces], axis=0)
        a = _dot(xb, wgb)
        u = _dot(xb, wub)
        y = _dot((_silu(a) * u).astype(BF16), wdb)
        off = 0
        for gi, r0, n in pieces:
            acc_refs[gi][r0:r0 + n, :] = acc_refs[gi][r0:r0 + n, :] + y[off:off + n]
            off += n

    @pl.when(f == nf - 1)
    def _():
        for gs_ref, out_ref, acc_ref in zip(gs_refs, out_refs, acc_refs):
            out_ref[0] = (acc_ref[...] * gs_ref[0][:, 0:1]).astype(BF16)


def _ffn(xes, gss, w_gate, w_up, w_down, layer):
    n_groups = len(xes)
    _, e, d, ff = w_gate.shape
    nf = ff // FF_TILE
    in_specs = [pl.BlockSpec((1,) + x.shape[1:], lambda i, f: (i, 0, 0)) for x in xes]
    in_specs += [pl.BlockSpec((1,) + g.shape[1:], lambda i, f: (i, 0, 0)) for g in gss]
    in_specs += [pl.BlockSpec((1, 1, d, FF_TILE), lambda i, f: (layer, i, 0, f)),
                 pl.BlockSpec((1, 1, d, FF_TILE), lambda i, f: (layer, i, 0, f)),
                 pl.BlockSpec((1, 1, FF_TILE, d), lambda i, f: (layer, i, f, 0))]
    out = pl.pallas_call(
        functools.partial(_ffn_kernel, n_groups=n_groups),
        grid=(e, nf),
        in_specs=in_specs,
        out_specs=[pl.BlockSpec((1,) + x.shape[1:], lambda i, f: (i, 0, 0)) for x in xes],
        out_shape=[jax.ShapeDtypeStruct(x.shape, BF16) for x in xes],
        scratch_shapes=[pltpu.VMEM(x.shape[1:], F32) for x in xes],
        compiler_params=_cparams(2),
        name="expert_ffn",
    )(*xes, *gss, w_gate, w_up, w_down)
    return list(out)


def _combine_kernel(*refs, cap, final):
    if final:
        post_ref, yg_ref, x_ref, mod_ref, fw_ref, o_ref = refs
    else:
        post_ref, yg_ref, x_ref, mod_ref, o_ref = refs
    d = D_MODEL
    sample = pl.program_id(0)
    post = pltpu.roll(post_ref[...], (LANES - sample * N_EXPERTS) % LANES, 1)
    tn = post.shape[0]
    slot = lax.broadcasted_iota(jnp.int32, (tn, cap), 1).astype(F32)
    acc = jnp.zeros((tn, d), F32)
    for e in range(N_EXPERTS):
        onehot = jnp.where(post[:, e:e + 1] == slot, 1.0, 0.0).astype(BF16)
        acc = acc + _dot(onehot, yg_ref[e])
    x2 = x_ref[0] + mod_ref[0][:, 5 * d:6 * d] * acc
    if final:
        ms = jnp.mean(x2 * x2, axis=-1, keepdims=True)
        o_ref[0] = x2 * lax.rsqrt(ms + EPS) * fw_ref[...]
    else:
        o_ref[0] = x2


def _combine(post, yg, xs, mod, cap, row0, n, mod_row_ctx, final_w=None):
    b, ts, d = xs.shape
    tn = min(SEG, n)
    blk0 = row0 // tn
    final = final_w is not None
    in_specs = [pl.BlockSpec((tn, LANES), lambda i, j: (j, 0)),
                pl.BlockSpec((N_EXPERTS, cap, d), lambda i, j: (0, i, 0)),
                pl.BlockSpec((1, tn, d), lambda i, j: (i, j + blk0, 0)),
                pl.BlockSpec((1, 1, mod.shape[-1]), lambda i, j: (8 if mod_row_ctx else i, 0, 0))]
    args = [post, yg, xs, mod]
    if final:
        in_specs.append(pl.BlockSpec(final_w.shape, lambda i, j: (0, 0)))
        args.append(final_w)
        out_spec = pl.BlockSpec((1, tn, d), lambda i, j: (i, j, 0))
        out_shape = jax.ShapeDtypeStruct((b, n, d), F32)
        aliases = {}
    else:
        out_spec = pl.BlockSpec((1, tn, d), lambda i, j: (i, j + blk0, 0))
        out_shape = jax.ShapeDtypeStruct((b, ts, d), F32)
        aliases = {2: 0}
    return pl.pallas_call(
        functools.partial(_combine_kernel, cap=cap, final=final),
        grid=(b, n // tn),
        in_specs=in_specs,
        out_specs=out_spec,
        out_shape=out_shape,
        input_output_aliases=aliases,
        compiler_params=_cparams(2),
        name="expert_combine_final" if final else "expert_combine",
    )(*args)


def _rope_tables(n_lat, lc):
    n_rows = n_lat // GRID_W
    rows = jnp.repeat(jnp.arange(n_rows, dtype=F32), GRID_W)
    cols = jnp.tile(jnp.arange(GRID_W, dtype=F32), n_rows)
    n_freq = A_HEAD_DIM // 4
    inv_freq = ROPE_THETA ** (-jnp.arange(n_freq, dtype=F32) / n_freq)
    ang = jnp.concatenate([rows[:, None] * inv_freq, cols[:, None] * inv_freq], axis=-1)
    c, s = jnp.cos(ang), jnp.sin(ang)
    cos = jnp.concatenate([c, c, c, c], axis=-1)
    sin = jnp.concatenate([-s, -s, s, s], axis=-1)
    cos = jnp.concatenate([jnp.ones((lc, LANES), F32), cos], axis=0)
    sin = jnp.concatenate([jnp.zeros((lc, LANES), F32), sin], axis=0)
    return cos, sin


def _pair_rope_layout(w):
    d, n = w.shape
    q = LANES // 4
    return w.reshape(d, n // LANES, 2, 2, q).transpose(0, 1, 3, 2, 4).reshape(d, n)


def kernel(x, c, ctx, c_ctx, w_ada, b_ada, norm1_w, norm2_w, w_in, mlstm_conv_w, mlstm_gate_b, mlstm_norm_w,
           diff_lambda, diff_subln_w, gqa_qnorm_w, gqa_knorm_w, w_branch_a, w_branch_b, w_branch_c, w_out,
           w_router, w_exp_gate, w_exp_up, w_exp_down, final_norm_w):
    b, n_lat, d = x.shape
    lc = ctx.shape[1]
    depth = w_ada.shape[0]
    assert d == D_MODEL and b * N_EXPERTS <= LANES and b <= 8
    ts = lc + n_lat
    assert lc % SEG == 0 and n_lat % SEG == 0 and lc % TQ == 0 and (b * ts) % TM == 0

    xs = jnp.concatenate([ctx, x], axis=1)
    cvec = jnp.zeros((16, d), F32).at[:b].set(c).at[8].set(c_ctx)
    mods = _ada(cvec, w_ada, b_ada)
    cos, sin = _rope_tables(n_lat, lc)
    cap_lat = EC_CAPACITY_FACTOR * n_lat // N_EXPERTS
    cap_ctx = EC_CAPACITY_FACTOR * lc // N_EXPERTS
    out = None

    for layer in range(depth):
        with_ctx = layer < depth - 1
        mod = mods[layer].reshape(16, 1, 6 * d)
        wl = w_in[layer]
        bq0 = 3 * MIX_W
        w_bq = wl[:, bq0:bq0 + MIX_W].reshape(d, B_KV_HEADS, B_GROUP, B_HEAD_DIM).transpose(0, 2, 1, 3).reshape(d, MIX_W)
        kb0 = bq0 + MIX_W
        w_main = jnp.concatenate([_pair_rope_layout(wl[:, :2 * MIX_W]), wl[:, 2 * MIX_W:bq0], _pair_rope_layout(w_bq),
                                  _pair_rope_layout(wl[:, kb0:kb0 + KV_B]), wl[:, kb0 + KV_B:MAIN_COLS]], axis=1).astype(BF16)
        w_bb = w_branch_b[layer].reshape(B_KV_HEADS, B_GROUP, B_HEAD_DIM, d).transpose(1, 0, 2, 3).reshape(MIX_W, d)
        w_gates = wl[:, GATE_COL0:GATE_COL0 + N_GATES]
        wg = jnp.pad(w_gates, ((0, 0), (0, LANES - N_GATES))).astype(BF16)
        wgt = w_gates.T.astype(BF16)
        gb = jnp.pad(mlstm_gate_b[layer], (0, LANES - N_GATES)).reshape(1, LANES)
        gbt = mlstm_gate_b[layer].reshape(N_GATES, 1)
        n1 = norm1_w[layer].reshape(1, d)
        n2 = norm2_w[layer].reshape(1, d)
        qnw = _pair_rope_layout(jnp.tile(gqa_qnorm_w[layer], LANES // B_HEAD_DIM).reshape(1, LANES))
        knw = _pair_rope_layout(jnp.tile(gqa_knorm_w[layer], LANES // B_HEAD_DIM).reshape(1, LANES))

        flat = _inproj(xs.reshape(b * ts, d), mod, n1, w_main, wg, wgt, gb, gbt, cos, sin, qnw, knw, ts, lc)
        qa, ka, va, qb, kb, vb, qkc, vc, oc, g = [a.reshape(b, ts, a.shape[-1]) for a in flat[:-1]]
        gt = flat[-1]

        lam_init = 0.8 - 0.6 * math.exp(-0.3 * layer)
        oa = _diff_attn(diff_lambda[layer], diff_subln_w[layer].reshape(1, 2 * A_HEAD_DIM),
                        qa, ka, va, lam_init, lc, with_ctx)
        ob = _gqa_attn(qb, kb, vb, lc, with_ctx)
        ocm = _mlstm(qkc, vc, oc, g, gt, mlstm_conv_w[layer], mlstm_norm_w[layer], lc)

        wm = wl[:, MERGE_COL0:].astype(BF16)
        wr = jnp.pad(w_router[layer], ((0, 0), (0, LANES - N_EXPERTS)))
        rows2 = lambda a: a.reshape(b * ts, a.shape[-1])
        xs, h2, aff = _merge(rows2(xs), mod, n1, n2, rows2(oa), rows2(ob), rows2(ocm), wm,
                             w_branch_a[layer].astype(BF16), w_bb.astype(BF16),
                             w_branch_c[layer].astype(BF16), w_out[layer].astype(BF16), wr, ts, lc, with_ctx)
        xs, h2, aff = [a.reshape(b, ts, a.shape[-1]) for a in (xs, h2, aff)]

        groups = [(lc, n_lat, cap_lat)]
        if with_ctx:
            groups.append((0, lc, cap_ctx))
        xes, gss, posts = [], [], []
        for row0, n, cap in groups:
            post, posr, affr = _route(aff, cap, row0, n)
            xe, gs = _gather(posr, affr, h2, cap, row0, n, b)
            xes.append(xe)
            gss.append(gs)
            posts.append(post)
        ygs = _ffn(xes, gss, w_exp_gate, w_exp_up, w_exp_down, layer)
        for gi, (row0, n, cap) in enumerate(groups):
            is_last = (layer == depth - 1) and gi == 0
            res = _combine(posts[gi], ygs[gi], xs, mod, cap, row0, n, mod_row_ctx=(row0 == 0),
                           final_w=final_norm_w.reshape(1, d) if is_last else None)
            if is_last:
                out = res
            else:
                xs = res
    return out
```

```python
import functools
import math

import jax
import jax.numpy as jnp
from jax import lax
from jax.experimental import pallas as pl
from jax.experimental.pallas import tpu as pltpu

F32 = jnp.float32
BF16 = jnp.bfloat16

D_MODEL = 1024
DEPTH = 2
GRID_W = 64
ROPE_THETA = 10000.0
EPS = 1e-6
NEG_BIG = -1e30
MIX_W = D_MODEL // 2
A_HEAD_DIM = 64
A_HEADS = MIX_W // (2 * A_HEAD_DIM)
B_HEAD_DIM = 64
B_Q_HEADS = MIX_W // B_HEAD_DIM
B_KV_HEADS = 2
B_GROUP = B_Q_HEADS // B_KV_HEADS
C_HEAD_DIM = 128
C_HEADS = MIX_W // C_HEAD_DIM
C_CONV = 3
C_CHUNK = 128
N_BRANCH = 3
N_EXPERTS = 16
EXPERT_FF = 2 * D_MODEL
EC_CAPACITY_FACTOR = 2

LANES = 128
KV_B = B_KV_HEADS * B_HEAD_DIM
N_GATES = 4 * C_HEADS
MAIN_COLS = 8 * MIX_W + 2 * KV_B
GATE_COL0 = MAIN_COLS
MERGE_COL0 = MAIN_COLS + N_GATES
TM = 512
SEG = 256
TQ = 256
FF_TILE = 512
FFN_ROWS = 512
GATHER_EXPERTS = 4
SMALL_KEYS = 512
VMEM_LIMIT = 56 * 1024 * 1024
SCORE_SCALE = (A_HEAD_DIM ** -0.5) * math.log2(math.e)


def _cparams(n_axes, vmem=VMEM_LIMIT):
    return pltpu.CompilerParams(dimension_semantics=("arbitrary",) * n_axes, vmem_limit_bytes=vmem)


def _dot(a, b):
    return jnp.dot(a, b, preferred_element_type=F32)


def _dot_nt(a, b):
    return lax.dot_general(a, b, (((1,), (1,)), ((), ())), preferred_element_type=F32)


def _split3(x):
    a = x.astype(BF16)
    r = x - a.astype(F32)
    b = r.astype(BF16)
    c = (r - b.astype(F32)).astype(BF16)
    return a, b, c


def _sigmoid(x):
    return 1.0 / (1.0 + jnp.exp(-x))


def _silu(x):
    return x * _sigmoid(x)


def _log_sigmoid(x):
    return jnp.minimum(x, 0.0) - jnp.log(1.0 + jnp.exp(-jnp.abs(x)))


def _norm_mod(x, nw, shift, scale):
    ms = jnp.mean(x * x, axis=-1, keepdims=True)
    return (x * lax.rsqrt(ms + EPS) * nw) * (1.0 + scale) + shift


def _ada_kernel(c_ref, w_ref, b_ref, o_ref):
    s = _silu(c_ref[...])
    s1, s2, _ = _split3(s)
    w = w_ref[0]
    w1, w2, _ = _split3(w)
    o_ref[0] = _dot(s1, w1) + _dot(s1, w2) + _dot(s2, w1) + b_ref[0]


def _ada(cvec, w_ada, b_ada):
    depth, d, n = w_ada.shape
    tn = 1536
    return pl.pallas_call(
        _ada_kernel,
        grid=(depth, n // tn),
        in_specs=[pl.BlockSpec((16, d), lambda l, j: (0, 0)),
                  pl.BlockSpec((1, d, tn), lambda l, j: (l, 0, j)),
                  pl.BlockSpec((1, 1, tn), lambda l, j: (l, 0, j))],
        out_specs=pl.BlockSpec((1, 16, tn), lambda l, j: (l, 0, j)),
        out_shape=jax.ShapeDtypeStruct((depth, 16, n), F32),
        compiler_params=_cparams(2),
        name="ada_mod",
    )(cvec, w_ada, b_ada.reshape(depth, 1, n))


def _segment_rows(x_refs, k, first, count, segs_per_sample, n_ctx_segs):
    if len(x_refs) == 1:
        return [x_refs[0][i * SEG:(i + 1) * SEG, :] for i in range(count)]
    out = []
    for i in range(count):
        is_ctx = (first(k) + i) % segs_per_sample < n_ctx_segs
        out.append(jnp.where(is_ctx, x_refs[2 * i][...], x_refs[2 * i + 1][...]))
    return out


def _inproj_kernel(*refs, n_x, sps, ncs):
    x_refs = refs[:n_x]
    (mod0_ref, mod1_ref, nw_ref, w_ref, wg_ref, wgt_ref, gb_ref, gbt_ref,
     cos0_ref, cos1_ref, sin0_ref, sin1_ref, qnw_ref, knw_ref,
     qa_ref, ka_ref, va_ref, qb_ref, kb_ref, vb_ref, qkc_ref, vc_ref, oc_ref, g_ref, gt_ref) = refs[n_x:]
    d = D_MODEL
    tm = g_ref.shape[0]
    xs = _segment_rows(x_refs, pl.program_id(0), lambda k: 2 * k, 2, sps, ncs)
    hs = []
    for x, mod_ref in zip(xs, (mod0_ref, mod1_ref)):
        mod = mod_ref[0]
        hs.append(_norm_mod(x, nw_ref[...], mod[:, 0:d], mod[:, d:2 * d]).astype(BF16))
    h = jnp.concatenate(hs, axis=0)
    cos = jnp.concatenate([cos0_ref[...], cos1_ref[...]], axis=0)
    sin = jnp.concatenate([sin0_ref[...], sin1_ref[...]], axis=0)
    lane = lax.broadcasted_iota(jnp.int32, (tm, LANES), 1)

    def rope(p):
        return p * cos + pltpu.roll(p, LANES // 2, 1) * sin

    first_head = _first_head_lanes((tm, LANES))

    def head_inv_rms(p):
        sq = p * p
        sa = jnp.sum(jnp.where(first_head, sq, 0.0), axis=-1, keepdims=True)
        sb = jnp.sum(jnp.where(first_head, 0.0, sq), axis=-1, keepdims=True)
        return lax.rsqrt(jnp.where(first_head, sa, sb) * (1.0 / B_HEAD_DIM) + EPS)

    def proj(c0):
        return _dot(h, w_ref[:, c0:c0 + 2 * LANES])

    def halves(p):
        return p[:, :LANES], p[:, LANES:]

    pk, pv = halves(proj(2048))
    vb_ref[...] = pv.astype(BF16)
    chunks = [p for j in range(2) for p in halves(proj(1536 + j * 256))] + [pk]
    for j in range(4):
        qkc_ref[:, j * 256:(j + 1) * 256] = proj(2304 + j * 256)
    qnw = qnw_ref[...]
    for i, p in enumerate(chunks[:-1]):
        qb_ref[:, i * LANES:(i + 1) * LANES] = (rope(p * head_inv_rms(p) * qnw) * SCORE_SCALE).astype(BF16)
    kb_ref[...] = rope(pk * head_inv_rms(pk) * knw_ref[...]).astype(BF16)
    for j in range(2):
        for half, p in enumerate(halves(proj(j * 256))):
            c = j * 256 + half * LANES
            qa_ref[:, c:c + LANES] = (rope(p) * SCORE_SCALE).astype(BF16)
    for j in range(2):
        for half, p in enumerate(halves(proj(512 + j * 256))):
            c = j * 256 + half * LANES
            ka_ref[:, c:c + LANES] = rope(p).astype(BF16)
    for j in range(2):
        va_ref[:, j * 256:(j + 1) * 256] = proj(1024 + j * 256).astype(BF16)
    for j in range(2):
        vc_ref[:, j * 256:(j + 1) * 256] = proj(3328 + j * 256).astype(BF16)
    for j in range(2):
        oc_ref[:, j * 256:(j + 1) * 256] = _sigmoid(proj(3840 + j * 256)).astype(BF16)
    g = _dot(h, wg_ref[...]) + gb_ref[...]
    is_f = ((lane % 8) >= 4) & (lane < N_GATES)
    g_ref[...] = jnp.where(is_f, _log_sigmoid(g), g)
    gt = _dot_nt(wgt_ref[...], h) + gbt_ref[...]
    row = lax.broadcasted_iota(jnp.int32, (N_GATES, tm), 0)
    gt_ref[...] = jnp.where((row % 8) >= 4, _log_sigmoid(gt), gt)


def _seg_rows(k, half, segs_per_sample, n_ctx_segs):
    seg = 2 * k + half
    sample = seg // segs_per_sample
    within = seg % segs_per_sample
    return jnp.where(within < n_ctx_segs, 8, sample), within


def _mod_specs(width, segs_per_sample, n_ctx_segs):
    return [pl.BlockSpec((1, 1, width), lambda k, h=half: (_seg_rows(k, h, segs_per_sample, n_ctx_segs)[0], 0, 0))
            for half in range(2)]


def _segment_of_step(segs_per_sample, n_ctx_segs, with_ctx):
    if with_ctx:
        return lambda k: k
    n_lat = segs_per_sample - n_ctx_segs
    return lambda k: (k // n_lat) * segs_per_sample + n_ctx_segs + k % n_lat


def _mod_spec_one_seg(width, segs_per_sample, n_ctx_segs, seg_of):
    def index(k):
        seg = seg_of(k)
        return jnp.where(seg % segs_per_sample < n_ctx_segs, 8, seg // segs_per_sample), 0, 0
    return pl.BlockSpec((1, 1, width), index)


def _split_x_specs(seg_of, d, sps, ncs):
    nls = sps - ncs

    def ctx_index(k):
        seg = seg_of(k)
        return (seg // sps) * ncs + jnp.minimum(seg % sps, ncs - 1), 0

    def lat_index(k):
        seg = seg_of(k)
        return (seg // sps) * nls + jnp.clip(seg % sps - ncs, 0, nls - 1), 0

    return [pl.BlockSpec((SEG, d), ctx_index), pl.BlockSpec((SEG, d), lat_index)]


def _inproj(x_parts, mod, nw, w_main, wg, wgt, gb, gbt, cos, sin, qnw, knw, ts, lc):
    d = x_parts[0].shape[-1]
    rows = sum(p.shape[0] for p in x_parts)
    sps = ts // SEG
    ncs = lc // SEG
    tok = lambda width: pl.BlockSpec((TM, width), lambda k: (k, 0))
    full = lambda a: pl.BlockSpec(a.shape, lambda k: (0,) * a.ndim)
    table = [pl.BlockSpec((SEG, LANES), lambda k, h=half: (_seg_rows(k, h, sps, ncs)[1], 0)) for half in range(2)]
    outs = [(MIX_W, BF16), (MIX_W, BF16), (MIX_W, BF16), (MIX_W, BF16), (KV_B, BF16), (KV_B, BF16),
            (2 * MIX_W, F32), (MIX_W, BF16), (MIX_W, BF16), (LANES, F32)]
    out_shape = [jax.ShapeDtypeStruct((rows, w), dt) for w, dt in outs]
    out_specs = [tok(w) for w, _ in outs]
    out_shape.append(jax.ShapeDtypeStruct((N_GATES, rows), F32))
    out_specs.append(pl.BlockSpec((N_GATES, TM), lambda k: (0, k)))
    if len(x_parts) == 1:
        x_specs, x_args = [tok(d)], x_parts
    else:
        x_specs = [sp for half in range(2) for sp in _split_x_specs(lambda k, h=half: 2 * k + h, d, sps, ncs)]
        x_args = list(x_parts) * 2
    return pl.pallas_call(
        functools.partial(_inproj_kernel, n_x=len(x_args), sps=sps, ncs=ncs),
        grid=(rows // TM,),
        in_specs=x_specs + _mod_specs(mod.shape[-1], sps, ncs)
        + [full(nw), full(w_main), full(wg), full(wgt), full(gb), full(gbt)]
        + table + table + [full(qnw), full(knw)],
        out_specs=out_specs,
        out_shape=out_shape,
        compiler_params=_cparams(1),
        name="in_proj",
    )(*x_args, mod, mod, nw, w_main, wg, wgt, gb, gbt, cos, cos, sin, sin, qnw, knw)


def _first_head_lanes(shape):
    lane = lax.broadcasted_iota(jnp.int32, shape, 1)
    return (lane // (LANES // 4)) % 2 == 0


def _softmax_numerators(sb, eb, rows, lk):
    maxes = [jnp.max(sb[r:r + 8, :lk], axis=-1, keepdims=True) for r in range(0, rows, 8)]
    for rb in range(rows // 16):
        parts = [jnp.exp2(sb[rb * 16 + sub * 8:rb * 16 + sub * 8 + 8, :lk] - maxes[2 * rb + sub]) for sub in range(2)]
        eb[rb * 16:(rb + 1) * 16, :lk] = jnp.concatenate(parts, axis=0).astype(BF16)


def _with_ones_column(v):
    lane = lax.broadcasted_iota(jnp.int32, v.shape, 1)
    ones = jnp.where(lane == 0, 1.0, 0.0).astype(v.dtype)
    return jnp.concatenate([v, ones], axis=1)


def _attention_units(n_units, scores, finish, values, s_s, e_s, rows, lk):
    if lk <= SMALL_KEYS:
        for u in range(n_units):
            s = scores(u)
            e = jnp.exp2(s - jnp.max(s, axis=-1, keepdims=True))
            finish(u, _dot(e.astype(BF16), values(u)))
        return
    s_s[0, :, :lk] = scores(0)
    for u in range(n_units):
        if u + 1 < n_units:
            s_s[(u + 1) % 2, :, :lk] = scores(u + 1)
        _softmax_numerators(s_s.at[u % 2], e_s.at[u % 2], rows, lk)
        finish(u, _dot(e_s[u % 2, :, :lk], values(u)))


def _diff_attn_kernel(lam_ref, sub_ref, q_ref, k_ref, v_ref, o_ref, s_s, e_s, *, lam_init, n_ctx_blocks, with_ctx, lc):
    qi = pl.program_id(1)
    lv = lam_ref[...]
    lam = (jnp.exp(jnp.sum(lv[0:1] * lv[1:2], axis=-1, keepdims=True))
           - jnp.exp(jnp.sum(lv[2:3] * lv[3:4], axis=-1, keepdims=True)) + lam_init)
    tq = q_ref.shape[1]
    w = 2 * A_HEAD_DIM

    def body(lk):
        low = _first_head_lanes((tq, w))

        def scores(h):
            cols = slice(h * w, (h + 1) * w)
            q = q_ref[0, :, cols]
            zero = jnp.zeros_like(q)
            qs = jnp.concatenate([jnp.where(low, q, zero), jnp.where(low, zero, q)], axis=0)
            return _dot_nt(qs, k_ref[0, :lk, cols])

        def values(h):
            return _with_ones_column(v_ref[0, :lk, h * w:(h + 1) * w])

        def finish(h, pv):
            o = (pv[:tq, :w] * (1.0 / pv[:tq, w:w + 1])
                 - pv[tq:, :w] * (lam / pv[tq:, w:w + 1]))
            ms = jnp.mean(o * o, axis=-1, keepdims=True)
            o_ref[0, :, h * w:(h + 1) * w] = (o * lax.rsqrt(ms + EPS) * sub_ref[...] * (1.0 - lam_init)).astype(BF16)

        _attention_units(A_HEADS, scores, finish, values, s_s, e_s, 2 * tq, lk)

    @pl.when(qi < n_ctx_blocks)
    def _():
        if with_ctx:
            body(lc)
        else:
            o_ref[...] = jnp.zeros_like(o_ref)

    @pl.when(qi >= n_ctx_blocks)
    def _():
        body(k_ref.shape[1])


def _diff_attn(lam_vecs, sub_w, qa, ka, va, lam_init, lc, with_ctx):
    b, ts, _ = qa.shape
    n_ctx_blocks = lc // TQ
    nq = ts // TQ
    kern = functools.partial(_diff_attn_kernel, lam_init=lam_init, n_ctx_blocks=n_ctx_blocks, with_ctx=with_ctx, lc=lc)
    return pl.pallas_call(
        kern,
        grid=(b, nq),
        in_specs=[pl.BlockSpec(lam_vecs.shape, lambda i, j: (0, 0)),
                  pl.BlockSpec(sub_w.shape, lambda i, j: (0, 0)),
                  pl.BlockSpec((1, TQ, MIX_W), lambda i, j: (i, j, 0)),
                  pl.BlockSpec((1, ts, MIX_W), lambda i, j: (i, 0, 0)),
                  pl.BlockSpec((1, ts, MIX_W), lambda i, j: (i, 0, 0))],
        out_specs=pl.BlockSpec((1, TQ, MIX_W), lambda i, j: (i, j, 0)),
        out_shape=jax.ShapeDtypeStruct((b, ts, MIX_W), BF16),
        scratch_shapes=[pltpu.VMEM((2, 2 * TQ, ts), F32), pltpu.VMEM((2, 2 * TQ, ts), BF16)],
        compiler_params=_cparams(2),
        name="diff_attn",
    )(lam_vecs, sub_w, qa, ka, va)


def _gqa_kernel(q_ref, k_ref, v_ref, o_ref, s_s, e_s, *, n_ctx_blocks, with_ctx, lc):
    qi = pl.program_id(1)
    tq = q_ref.shape[1]

    def body(lk):
        low = _first_head_lanes((tq, LANES))
        out_low = lax.broadcasted_iota(jnp.int32, (tq, LANES), 1) < B_HEAD_DIM
        k = k_ref[0, :lk, :]
        v1 = _with_ones_column(v_ref[0, :lk, :])
        outs = {}
        pairs = B_GROUP // 2

        def scores(u):
            g, pair = divmod(u, pairs)
            parts = []
            for j in (2 * pair, 2 * pair + 1):
                x = q_ref[0, :, j * LANES:(j + 1) * LANES]
                zero = jnp.zeros_like(x)
                parts.append(jnp.where(low, x, zero) if g == 0 else jnp.where(low, zero, x))
            return _dot_nt(jnp.concatenate(parts, axis=0), k)

        def finish(u, pv):
            g, pair = divmod(u, pairs)
            o = pv[:, :LANES] * (1.0 / pv[:, LANES:LANES + 1])
            outs[(g, 2 * pair)] = o[:tq]
            outs[(g, 2 * pair + 1)] = o[tq:]

        _attention_units(B_KV_HEADS * pairs, scores, finish, lambda u: v1, s_s, e_s, 2 * tq, lk)
        for j in range(B_GROUP):
            o_ref[0, :, j * LANES:(j + 1) * LANES] = jnp.where(out_low, outs[(0, j)], outs[(1, j)]).astype(BF16)

    @pl.when(qi < n_ctx_blocks)
    def _():
        if with_ctx:
            body(lc)
        else:
            o_ref[...] = jnp.zeros_like(o_ref)

    @pl.when(qi >= n_ctx_blocks)
    def _():
        body(k_ref.shape[1])


def _gqa_attn(qb, kb, vb, lc, with_ctx):
    b, ts, _ = qb.shape
    n_ctx_blocks = lc // TQ
    nq = ts // TQ
    kern = functools.partial(_gqa_kernel, n_ctx_blocks=n_ctx_blocks, with_ctx=with_ctx, lc=lc)
    return pl.pallas_call(
        kern,
        grid=(b, nq),
        in_specs=[pl.BlockSpec((1, TQ, MIX_W), lambda i, j: (i, j, 0)),
                  pl.BlockSpec((1, ts, KV_B), lambda i, j: (i, 0, 0)),
                  pl.BlockSpec((1, ts, KV_B), lambda i, j: (i, 0, 0))],
        out_specs=pl.BlockSpec((1, TQ, MIX_W), lambda i, j: (i, j, 0)),
        out_shape=jax.ShapeDtypeStruct((b, ts, MIX_W), BF16),
        scratch_shapes=[pltpu.VMEM((2, 2 * TQ, ts), F32), pltpu.VMEM((2, 2 * TQ, ts), BF16)],
        compiler_params=_cparams(2),
        name="gqa_attn",
    )(qb, kb, vb)


def _mlstm_kernel(q_ref, k_ref, v_ref, o_ref, g_ref, gt_ref, cwq_ref, cwk_ref, nw_ref, out_ref,
                  q_s, kt_s, bc_s, ac_s, rows_s, hacc_s, st_s, *, lc):
    ts = q_ref.shape[1]
    hp = q_ref.shape[2] // C_HEAD_DIM
    head0 = pl.program_id(1) * hp
    nc = ts // C_CHUNK
    ncc = lc // C_CHUNK
    ch = C_CHUNK

    row = lax.broadcasted_iota(jnp.int32, (ts, LANES), 0)
    prev_ok = (row != 0) & (row != lc)
    next_ok = (row != lc - 1) & (row != ts - 1)

    def conv(x, w):
        xp = jnp.where(prev_ok, pltpu.roll(x, 1, 0), 0.0)
        xn = jnp.where(next_ok, pltpu.roll(x, ts - 1, 0), 0.0)
        return _silu(xp * w[0:1] + x * w[1:2] + xn * w[2:3])

    for j in range(hp):
        cols = slice(j * LANES, (j + 1) * LANES)
        q_s[:, cols] = conv(q_ref[0, :, cols], cwq_ref[0, :, cols]).astype(BF16)
        y = conv(k_ref[0, :, cols], cwk_ref[0, :, cols]) * (C_HEAD_DIM ** -0.5)
        for c in range(nc):
            kt_s[c, cols, :] = y[c * ch:(c + 1) * ch, :].T.astype(BF16)

    ri = lax.broadcasted_iota(jnp.int32, (ch, ch), 0)
    ci = lax.broadcasted_iota(jnp.int32, (ch, ch), 1)
    lower = jnp.where(ci <= ri, 1.0, 0.0).astype(BF16)
    upper = jnp.where(ci >= ri, 1.0, 0.0).astype(BF16)
    lane = ci
    rowi = lax.broadcasted_iota(jnp.int32, (N_GATES, ch), 0)
    for c in range(nc):
        rs = slice(c * ch, (c + 1) * ch)
        g = g_ref[0, rs, :]
        g1, g2, g3 = _split3(g)
        pre = _dot(lower, g1) + _dot(lower, g2) + _dot(lower, g3)
        suf = _dot(upper, g1) + _dot(upper, g2) + _dot(upper, g3)
        gt = gt_ref[:, rs]
        t1, t2, t3 = _split3(gt)
        pre_t = _dot(t1, upper) + _dot(t2, upper) + _dot(t3, upper)
        suf_t = _dot(t1, lower) + _dot(t2, lower) + _dot(t3, lower)
        for j in range(hp):
            for direction in range(2):
                idx = j * 2 + direction
                li = head0 + j + 8 * direction
                lf = li + 4
                cum, cum_t = (pre, pre_t) if direction == 0 else (suf, suf_t)
                b_col = jnp.sum(jnp.where(lane == lf, cum, 0.0), axis=-1, keepdims=True)
                i_col = jnp.sum(jnp.where(lane == li, g, 0.0), axis=-1, keepdims=True)
                bc_s[idx, rs, :] = jnp.broadcast_to(b_col, (ch, LANES))
                ac_s[idx, rs, :] = jnp.broadcast_to(i_col - b_col, (ch, LANES))
                b_row = jnp.sum(jnp.where(rowi == lf, cum_t, 0.0), axis=0, keepdims=True)
                i_row = jnp.sum(jnp.where(rowi == li, gt, 0.0), axis=0, keepdims=True)
                rows_s[c, 2 * idx:2 * idx + 1, :] = b_row
                rows_s[c, 2 * idx + 1:2 * idx + 2, :] = i_row - b_row

    hacc_s[...] = jnp.zeros_like(hacc_s)
    st_s[...] = jnp.zeros_like(st_s)
    tri_f = ci <= ri
    tri_b = ci >= ri

    def chain(c, j, direction, m):
        idx = j * 2 + direction
        c0 = pl.multiple_of(c * ch, ch)
        cols = slice(j * LANES, (j + 1) * LANES)
        q = q_s[pl.ds(c0, ch), cols]
        kt = kt_s[c, cols, :]
        v = v_ref[0, pl.ds(c0, ch), cols]
        bc = bc_s[idx, pl.ds(c0, ch), :]
        ac = ac_s[idx, pl.ds(c0, ch), :]
        rows = rows_s[c]
        b_row = rows[2 * idx:2 * idx + 1, :]
        ib_row = rows[2 * idx + 1:2 * idx + 2, :]
        tri = tri_f if direction == 0 else tri_b
        log_d = jnp.where(tri, bc + ib_row, NEG_BIG)
        m_intra = jnp.max(log_d, axis=-1, keepdims=True)
        log_inter = bc + m
        m_t = jnp.maximum(log_inter, m_intra)
        dm = jnp.exp(log_d - m_t)
        w_inter = jnp.exp(log_inter - m_t)
        s = _dot(q, kt) * dm
        st = st_s[idx]
        inter = _dot(q, st.astype(BF16))
        num = _dot(s.astype(BF16), v) + w_inter * inter[:, :LANES]
        den = jnp.sum(s, axis=-1, keepdims=True) + w_inter * inter[:, LANES:LANES + 1]
        hout = num / jnp.maximum(jnp.abs(den), jnp.exp(-m_t))
        hacc_s[pl.ds(c0, ch), cols] = hacc_s[pl.ds(c0, ch), cols] + hout
        total = b_row[:, ch - 1:ch] if direction == 0 else b_row[:, 0:1]
        m_new = jnp.maximum(total + m, jnp.max(total + ib_row, axis=-1, keepdims=True))
        w = jnp.exp(total + ac - m_new)
        decay = jnp.exp(total + m - m_new)
        wv = jnp.concatenate([w * v.astype(F32), jnp.where(lane == 0, w, 0.0)], axis=1).astype(BF16)
        st_s[idx] = decay * st + _dot(kt, wv)
        return m_new

    def step(i, ms):
        c_f = i
        c_b = jnp.where(i < ncc, ncc - 1 - i, nc + ncc - 1 - i)
        out = []
        for j in range(hp):
            out.append(chain(c_f, j, 0, ms[j * 2]))
            out.append(chain(c_b, j, 1, ms[j * 2 + 1]))
        return tuple(out)

    lax.fori_loop(0, nc, step, tuple(jnp.zeros((1, 1), F32) for _ in range(2 * hp)), unroll=2)

    for j in range(hp):
        cols = slice(j * LANES, (j + 1) * LANES)
        x = hacc_s[:, cols]
        ms = jnp.mean(x * x, axis=-1, keepdims=True)
        y = x * lax.rsqrt(ms + EPS) * nw_ref[0, :, cols]
        out_ref[0, :, cols] = (o_ref[0, :, cols].astype(F32) * y).astype(BF16)


def _mlstm(qkc, vc, oc, g, gt, conv_w, norm_w, lc, heads_per_step=2):
    b, ts, _ = vc.shape
    wq = heads_per_step * C_HEAD_DIM
    nhp = C_HEADS // heads_per_step
    nc = ts // C_CHUNK
    cw = conv_w.reshape(C_CONV, 2 * nhp, wq).transpose(1, 0, 2)
    nw = norm_w.reshape(1, nhp, wq).transpose(1, 0, 2)
    kern = functools.partial(_mlstm_kernel, lc=lc)
    tokw = lambda off: pl.BlockSpec((1, ts, wq), lambda i, p: (i, 0, p + off))
    return pl.pallas_call(
        kern,
        grid=(b, nhp),
        in_specs=[tokw(0), tokw(nhp), tokw(0), tokw(0),
                  pl.BlockSpec((1, ts, LANES), lambda i, p: (i, 0, 0)),
                  pl.BlockSpec((N_GATES, ts), lambda i, p: (0, i)),
                  pl.BlockSpec((1, C_CONV, wq), lambda i, p: (p, 0, 0)),
                  pl.BlockSpec((1, C_CONV, wq), lambda i, p: (p + nhp, 0, 0)),
                  pl.BlockSpec((1, 1, wq), lambda i, p: (p, 0, 0))],
        out_specs=tokw(0),
        out_shape=jax.ShapeDtypeStruct((b, ts, MIX_W), BF16),
        scratch_shapes=[pltpu.VMEM((ts, wq), BF16),
                        pltpu.VMEM((nc, wq, C_CHUNK), BF16),
                        pltpu.VMEM((2 * heads_per_step, ts, LANES), F32),
                        pltpu.VMEM((2 * heads_per_step, ts, LANES), F32),
                        pltpu.VMEM((nc, 4 * heads_per_step, C_CHUNK), F32),
                        pltpu.VMEM((ts, wq), F32),
                        pltpu.VMEM((2 * heads_per_step, C_HEAD_DIM, 2 * LANES), F32)],
        compiler_params=_cparams(2),
        name="mlstm",
    )(qkc, qkc, vc, oc, g, gt, cw, cw, nw)


def _merge_kernel(*refs, n_x, sps, ncs):
    x_refs = refs[:n_x]
    (mod_ref, n1_ref, n2_ref, oa_ref, ob_ref, oc_ref, wm_ref, wa_ref, wb_ref, wc_ref,
     wo_ref, wr_ref, xo_ref, h2_ref, aff_ref) = refs[n_x:]
    d = D_MODEL
    mod = mod_ref[0]
    (x,) = _segment_rows(x_refs, pl.program_id(0), lambda k: k, 1, sps, ncs)
    h = _norm_mod(x, n1_ref[...], mod[:, 0:d], mod[:, d:2 * d]).astype(BF16)
    merged = (_sigmoid(_dot(h, wm_ref[:, 0:d])) * _dot(oa_ref[...], wa_ref[...])
              + _sigmoid(_dot(h, wm_ref[:, d:2 * d])) * _dot(ob_ref[...], wb_ref[...])
              + _sigmoid(_dot(h, wm_ref[:, 2 * d:3 * d])) * _dot(oc_ref[...], wc_ref[...]))
    y = _dot(merged.astype(BF16), wo_ref[...])
    x1 = x + mod[:, 2 * d:3 * d] * y
    xo_ref[...] = x1
    h2 = _norm_mod(x1, n2_ref[...], mod[:, 3 * d:4 * d], mod[:, 4 * d:5 * d])
    h2b = h2.astype(BF16)
    h2_ref[...] = h2b
    h2l = (h2 - h2b.astype(F32)).astype(BF16)
    wr = wr_ref[...]
    wrh = wr.astype(BF16)
    wrl = (wr - wrh.astype(F32)).astype(BF16)
    logits = _dot(h2b, wrh) + _dot(h2b, wrl) + _dot(h2l, wrh)
    lane = lax.broadcasted_iota(jnp.int32, logits.shape, 1)
    valid = lane < N_EXPERTS
    logits = jnp.where(valid, logits, NEG_BIG)
    e = jnp.where(valid, jnp.exp(logits - jnp.max(logits, axis=-1, keepdims=True)), 0.0)
    aff_ref[...] = e / jnp.sum(e, axis=-1, keepdims=True)


def _merge(x_parts, mod, n1, n2, oa, ob, oc, wm, wa, wb, wc, wo, wr, ts, lc, with_ctx):
    d = x_parts[0].shape[-1]
    rows = sum(p.shape[0] for p in x_parts)
    sps, ncs = ts // SEG, lc // SEG
    seg_of = _segment_of_step(sps, ncs, with_ctx)
    n_steps = rows // SEG if with_ctx else (rows // ts) * (sps - ncs)
    tok = lambda width: pl.BlockSpec((SEG, width), lambda k: (seg_of(k), 0))
    full = lambda a: pl.BlockSpec(a.shape, lambda k: (0,) * a.ndim)
    split = len(x_parts) > 1
    assert not split or with_ctx
    x_specs = _split_x_specs(seg_of, d, sps, ncs) if split else [tok(d)]
    return pl.pallas_call(
        functools.partial(_merge_kernel, n_x=len(x_parts), sps=sps, ncs=ncs),
        grid=(n_steps,),
        in_specs=x_specs + [_mod_spec_one_seg(mod.shape[-1], sps, ncs, seg_of),
                  full(n1), full(n2), tok(MIX_W), tok(MIX_W), tok(MIX_W),
                  full(wm), full(wa), full(wb), full(wc), full(wo), full(wr)],
        out_specs=[tok(d), tok(d), tok(LANES)],
        out_shape=[jax.ShapeDtypeStruct((rows, d), F32),
                   jax.ShapeDtypeStruct((rows, d), BF16),
                   jax.ShapeDtypeStruct((rows, LANES), F32)],
        input_output_aliases={} if split else {0: 0},
        compiler_params=_cparams(1),
        name="merge_out",
    )(*x_parts, mod, n1, n2, oa, ob, oc, wm, wa, wb, wc, wo, wr)


def _route_kernel(aff_ref, post_ref, posr_ref, affr_ref, *, cap, row0):
    n = post_ref.shape[0]
    ch = LANES
    aff = aff_ref[0, row0:row0 + n, :]
    for bi in range(1, aff_ref.shape[0]):
        aff = aff + pltpu.roll(aff_ref[bi, row0:row0 + n, :], bi * N_EXPERTS, 1)

    def step(i, thr_bits):
        cand = thr_bits | jnp.left_shift(jnp.int32(1), 30 - i)
        cnt = jnp.sum((aff >= pltpu.bitcast(cand, F32)).astype(jnp.int32), axis=0, keepdims=True)
        return jnp.where(cnt >= cap, cand, thr_bits)

    thr = pltpu.bitcast(lax.fori_loop(0, 31, step, jnp.zeros((1, LANES), jnp.int32)), F32)
    gt = aff > thr
    eq = aff == thr
    need = cap - jnp.sum(gt.astype(jnp.int32), axis=0, keepdims=True)

    ri = lax.broadcasted_iota(jnp.int32, (ch, ch), 0)
    ci = lax.broadcasted_iota(jnp.int32, (ch, ch), 1)
    strict_lower = jnp.where(ci < ri, 1.0, 0.0).astype(BF16)

    def excl_cumsum(mask_f):
        carry = jnp.zeros((1, LANES), F32)
        blocks = []
        for c in range(n // ch):
            blk = mask_f[c * ch:(c + 1) * ch, :]
            blocks.append(_dot(strict_lower, blk.astype(BF16)) + carry)
            carry = carry + jnp.sum(blk, axis=0, keepdims=True)
        return jnp.concatenate(blocks, axis=0)

    eq_rank = excl_cumsum(jnp.where(eq, 1.0, 0.0))
    sel = gt | (eq & (eq_rank < need.astype(F32)))
    pos = excl_cumsum(jnp.where(sel, 1.0, 0.0))
    post = jnp.where(sel, pos, -1.0)
    post_ref[...] = post
    for c in range(n // ch):
        posr_ref[:, c * ch:(c + 1) * ch] = post[c * ch:(c + 1) * ch, :].T
        affr_ref[:, c * ch:(c + 1) * ch] = aff[c * ch:(c + 1) * ch, :].T


def _route(aff, cap, row0, n):
    return pl.pallas_call(
        functools.partial(_route_kernel, cap=cap, row0=row0),
        out_shape=[jax.ShapeDtypeStruct((n, LANES), F32),
                   jax.ShapeDtypeStruct((LANES, n), F32),
                   jax.ShapeDtypeStruct((LANES, n), F32)],
        compiler_params=pltpu.CompilerParams(vmem_limit_bytes=VMEM_LIMIT),
        name="route",
    )(aff)


def _gather_kernel(posr_ref, affr_ref, h_ref, xe_ref, gs_ref, *, cap, row0, n):
    ne = posr_ref.shape[0]
    slot = lax.broadcasted_iota(jnp.int32, (cap, n), 0).astype(F32)
    onehots = []
    for t in range(ne):
        pf = jnp.where(posr_ref[t] == slot, 1.0, 0.0)
        gs = jnp.sum(pf * affr_ref[t], axis=-1, keepdims=True)
        gs_ref[t] = jnp.broadcast_to(gs, (cap, LANES))
        onehots.append(pf.astype(BF16))
    xe = _dot(jnp.concatenate(onehots, axis=0), h_ref[0, row0:row0 + n, :])
    for t in range(ne):
        xe_ref[t] = xe[t * cap:(t + 1) * cap].astype(BF16)


def _gather(posr, affr, h2, cap, row0, n, b):
    ts, d = h2.shape[1:]
    be = posr.shape[0]
    posr3 = posr.reshape(be, 1, n)
    affr3 = affr.reshape(be, 1, n)
    e = N_EXPERTS
    ge = GATHER_EXPERTS
    return pl.pallas_call(
        functools.partial(_gather_kernel, cap=cap, row0=row0, n=n),
        grid=(b, e // ge),
        in_specs=[pl.BlockSpec((ge, 1, n), lambda i, j: (i * (e // ge) + j, 0, 0)),
                  pl.BlockSpec((ge, 1, n), lambda i, j: (i * (e // ge) + j, 0, 0)),
                  pl.BlockSpec((1, ts, d), lambda i, j: (i, 0, 0))],
        out_specs=[pl.BlockSpec((ge, cap, d), lambda i, j: (j, i, 0)),
                   pl.BlockSpec((ge, cap, LANES), lambda i, j: (j, i, 0))],
        out_shape=[jax.ShapeDtypeStruct((e, b * cap, d), BF16),
                   jax.ShapeDtypeStruct((e, b * cap, LANES), F32)],
        compiler_params=_cparams(2),
        name="expert_gather",
    )(posr3, affr3, h2)


def _ffn_kernel(*refs, n_groups):
    xe_refs = refs[0:n_groups]
    gs_refs = refs[n_groups:2 * n_groups]
    wg_ref, wu_ref, wd_ref = refs[2 * n_groups:2 * n_groups + 3]
    out_refs = refs[2 * n_groups + 3:3 * n_groups + 3]
    acc_refs = refs[3 * n_groups + 3:4 * n_groups + 3]
    f = pl.program_id(1)
    nf = pl.num_programs(1)
    wgb = wg_ref[0, 0].astype(BF16)
    wub = wu_ref[0, 0].astype(BF16)
    wdb = wd_ref[0, 0].astype(BF16)
    @pl.when(f == 0)
    def _():
        for acc_ref in acc_refs:
            acc_ref[...] = jnp.zeros_like(acc_ref)

    total = sum(r.shape[1] for r in xe_refs)
    n_blocks = max(1, total // FFN_ROWS)
    assert total % (16 * n_blocks) == 0
    rb = total // n_blocks
    for blk in range(n_blocks):
        pieces, start = [], 0
        for gi, r in enumerate(xe_refs):
            lo, hi = max(blk * rb, start), min((blk + 1) * rb, start + r.shape[1])
            if lo < hi:
                pieces.append((gi, lo - start, hi - lo))
            start += r.shape[1]
        xb = jnp.concatenate([xe_refs[gi][0, r0:r0 + n, :] for gi, r0, n in pieces], axis=0)
        a = _dot(xb, wgb)
        u = _dot(xb, wub)
        y = _dot((_silu(a) * u).astype(BF16), wdb)
        off = 0
        for gi, r0, n in pieces:
            acc_refs[gi][r0:r0 + n, :] = acc_refs[gi][r0:r0 + n, :] + y[off:off + n]
            off += n

    @pl.when(f == nf - 1)
    def _():
        for gs_ref, out_ref, acc_ref in zip(gs_refs, out_refs, acc_refs):
            out_ref[0] = (acc_ref[...] * gs_ref[0][:, 0:1]).astype(BF16)


def _ffn(xes, gss, w_gate, w_up, w_down, layer):
    n_groups = len(xes)
    _, e, d, ff = w_gate.shape
    nf = ff // FF_TILE
    in_specs = [pl.BlockSpec((1,) + x.shape[1:], lambda i, f: (i, 0, 0)) for x in xes]
    in_specs += [pl.BlockSpec((1,) + g.shape[1:], lambda i, f: (i, 0, 0)) for g in gss]
    in_specs += [pl.BlockSpec((1, 1, d, FF_TILE), lambda i, f: (layer, i, 0, f)),
                 pl.BlockSpec((1, 1, d, FF_TILE), lambda i, f: (layer, i, 0, f)),
                 pl.BlockSpec((1, 1, FF_TILE, d), lambda i, f: (layer, i, f, 0))]
    out = pl.pallas_call(
        functools.partial(_ffn_kernel, n_groups=n_groups),
        grid=(e, nf),
        in_specs=in_specs,
        out_specs=[pl.BlockSpec((1,) + x.shape[1:], lambda i, f: (i, 0, 0)) for x in xes],
        out_shape=[jax.ShapeDtypeStruct(x.shape, BF16) for x in xes],
        scratch_shapes=[pltpu.VMEM(x.shape[1:], F32) for x in xes],
        compiler_params=_cparams(2),
        name="expert_ffn",
    )(*xes, *gss, w_gate, w_up, w_down)
    return list(out)


def _combine_kernel(*refs, cap, final):
    if final:
        post_ref, yg_ref, x_ref, mod_ref, fw_ref, o_ref = refs
    else:
        post_ref, yg_ref, x_ref, mod_ref, o_ref = refs
    d = D_MODEL
    sample = pl.program_id(0)
    post = pltpu.roll(post_ref[...], (LANES - sample * N_EXPERTS) % LANES, 1)
    tn = post.shape[0]
    slot = lax.broadcasted_iota(jnp.int32, (tn, cap), 1).astype(F32)
    acc = jnp.zeros((tn, d), F32)
    for e in range(N_EXPERTS):
        onehot = jnp.where(post[:, e:e + 1] == slot, 1.0, 0.0).astype(BF16)
        acc = acc + _dot(onehot, yg_ref[e])
    x2 = x_ref[0] + mod_ref[0][:, 5 * d:6 * d] * acc
    if final:
        ms = jnp.mean(x2 * x2, axis=-1, keepdims=True)
        o_ref[0] = x2 * lax.rsqrt(ms + EPS) * fw_ref[...]
    else:
        o_ref[0] = x2


def _combine(post, yg, xs, mod, cap, row0, n, mod_row_ctx, final_w=None):
    b, ts, d = xs.shape
    tn = min(SEG, n)
    blk0 = row0 // tn
    final = final_w is not None
    in_specs = [pl.BlockSpec((tn, LANES), lambda i, j: (j, 0)),
                pl.BlockSpec((N_EXPERTS, cap, d), lambda i, j: (0, i, 0)),
                pl.BlockSpec((1, tn, d), lambda i, j: (i, j + blk0, 0)),
                pl.BlockSpec((1, 1, mod.shape[-1]), lambda i, j: (8 if mod_row_ctx else i, 0, 0))]
    args = [post, yg, xs, mod]
    if final:
        in_specs.append(pl.BlockSpec(final_w.shape, lambda i, j: (0, 0)))
        args.append(final_w)
        out_spec = pl.BlockSpec((1, tn, d), lambda i, j: (i, j, 0))
        out_shape = jax.ShapeDtypeStruct((b, n, d), F32)
        aliases = {}
    else:
        out_spec = pl.BlockSpec((1, tn, d), lambda i, j: (i, j + blk0, 0))
        out_shape = jax.ShapeDtypeStruct((b, ts, d), F32)
        aliases = {2: 0}
    return pl.pallas_call(
        functools.partial(_combine_kernel, cap=cap, final=final),
        grid=(b, n // tn),
        in_specs=in_specs,
        out_specs=out_spec,
        out_shape=out_shape,
        input_output_aliases=aliases,
        compiler_params=_cparams(2),
        name="expert_combine_final" if final else "expert_combine",
    )(*args)


def _rope_tables(n_lat, lc):
    n_rows = n_lat // GRID_W
    rows = jnp.repeat(jnp.arange(n_rows, dtype=F32), GRID_W)
    cols = jnp.tile(jnp.arange(GRID_W, dtype=F32), n_rows)
    n_freq = A_HEAD_DIM // 4
    inv_freq = ROPE_THETA ** (-jnp.arange(n_freq, dtype=F32) / n_freq)
    ang = jnp.concatenate([rows[:, None] * inv_freq, cols[:, None] * inv_freq], axis=-1)
    c, s = jnp.cos(ang), jnp.sin(ang)
    cos = jnp.concatenate([c, c, c, c], axis=-1)
    sin = jnp.concatenate([-s, -s, s, s], axis=-1)
    cos = jnp.concatenate([jnp.ones((lc, LANES), F32), cos], axis=0)
    sin = jnp.concatenate([jnp.zeros((lc, LANES), F32), sin], axis=0)
    return cos, sin


def _pair_rope_layout(w):
    d, n = w.shape
    q = LANES // 4
    return w.reshape(d, n // LANES, 2, 2, q).transpose(0, 1, 3, 2, 4).reshape(d, n)


def kernel(x, c, ctx, c_ctx, w_ada, b_ada, norm1_w, norm2_w, w_in, mlstm_conv_w, mlstm_gate_b, mlstm_norm_w,
           diff_lambda, diff_subln_w, gqa_qnorm_w, gqa_knorm_w, w_branch_a, w_branch_b, w_branch_c, w_out,
           w_router, w_exp_gate, w_exp_up, w_exp_down, final_norm_w):
    b, n_lat, d = x.shape
    lc = ctx.shape[1]
    depth = w_ada.shape[0]
    assert d == D_MODEL and b * N_EXPERTS <= LANES and b <= 8
    ts = lc + n_lat
    assert lc % SEG == 0 and n_lat % SEG == 0 and lc % TQ == 0 and (b * ts) % TM == 0

    x_parts = [ctx.reshape(b * lc, d), x.reshape(b * n_lat, d)]
    cvec = jnp.zeros((16, d), F32).at[:b].set(c).at[8].set(c_ctx)
    mods = _ada(cvec, w_ada, b_ada)
    cos, sin = _rope_tables(n_lat, lc)
    cap_lat = EC_CAPACITY_FACTOR * n_lat // N_EXPERTS
    cap_ctx = EC_CAPACITY_FACTOR * lc // N_EXPERTS
    out = None

    for layer in range(depth):
        with_ctx = layer < depth - 1
        mod = mods[layer].reshape(16, 1, 6 * d)
        wl = w_in[layer]
        bq0 = 3 * MIX_W
        w_bq = wl[:, bq0:bq0 + MIX_W].reshape(d, B_KV_HEADS, B_GROUP, B_HEAD_DIM).transpose(0, 2, 1, 3).reshape(d, MIX_W)
        kb0 = bq0 + MIX_W
        w_main = jnp.concatenate([_pair_rope_layout(wl[:, :2 * MIX_W]), wl[:, 2 * MIX_W:bq0], _pair_rope_layout(w_bq),
                                  _pair_rope_layout(wl[:, kb0:kb0 + KV_B]), wl[:, kb0 + KV_B:MAIN_COLS]], axis=1).astype(BF16)
        w_bb = w_branch_b[layer].reshape(B_KV_HEADS, B_GROUP, B_HEAD_DIM, d).transpose(1, 0, 2, 3).reshape(MIX_W, d)
        w_gates = wl[:, GATE_COL0:GATE_COL0 + N_GATES]
        wg = jnp.pad(w_gates, ((0, 0), (0, LANES - N_GATES))).astype(BF16)
        wgt = w_gates.T.astype(BF16)
        gb = jnp.pad(mlstm_gate_b[layer], (0, LANES - N_GATES)).reshape(1, LANES)
        gbt = mlstm_gate_b[layer].reshape(N_GATES, 1)
        n1 = norm1_w[layer].reshape(1, d)
        n2 = norm2_w[layer].reshape(1, d)
        qnw = _pair_rope_layout(jnp.tile(gqa_qnorm_w[layer], LANES // B_HEAD_DIM).reshape(1, LANES))
        knw = _pair_rope_layout(jnp.tile(gqa_knorm_w[layer], LANES // B_HEAD_DIM).reshape(1, LANES))

        flat = _inproj(x_parts, mod, n1, w_main, wg, wgt, gb, gbt, cos, sin, qnw, knw, ts, lc)
        qa, ka, va, qb, kb, vb, qkc, vc, oc, g = [a.reshape(b, ts, a.shape[-1]) for a in flat[:-1]]
        gt = flat[-1]

        lam_init = 0.8 - 0.6 * math.exp(-0.3 * layer)
        oa = _diff_attn(diff_lambda[layer], diff_subln_w[layer].reshape(1, 2 * A_HEAD_DIM),
                        qa, ka, va, lam_init, lc, with_ctx)
        ob = _gqa_attn(qb, kb, vb, lc, with_ctx)
        ocm = _mlstm(qkc, vc, oc, g, gt, mlstm_conv_w[layer], mlstm_norm_w[layer], lc)

        wm = wl[:, MERGE_COL0:].astype(BF16)
        wr = jnp.pad(w_router[layer], ((0, 0), (0, LANES - N_EXPERTS)))
        rows2 = lambda a: a.reshape(b * ts, a.shape[-1])
        xs, h2, aff = _merge(x_parts, mod, n1, n2, rows2(oa), rows2(ob), rows2(ocm), wm,
                             w_branch_a[layer].astype(BF16), w_bb.astype(BF16),
                             w_branch_c[layer].astype(BF16), w_out[layer].astype(BF16), wr, ts, lc, with_ctx)
        xs, h2, aff = [a.reshape(b, ts, a.shape[-1]) for a in (xs, h2, aff)]

        groups = [(lc, n_lat, cap_lat)]
        if with_ctx:
            groups.append((0, lc, cap_ctx))
        xes, gss, posts = [], [], []
        for row0, n, cap in groups:
            post, posr, affr = _route(aff, cap, row0, n)
            xe, gs = _gather(posr, affr, h2, cap, row0, n, b)
            xes.append(xe)
            gss.append(gs)
            posts.append(post)
        ygs = _ffn(xes, gss, w_exp_gate, w_exp_up, w_exp_down, layer)
        for gi, (row0, n, cap) in enumerate(groups):
            is_last = (layer == depth - 1) and gi == 0
            res = _combine(posts[gi], ygs[gi], xs, mod, cap, row0, n, mod_row_ctx=(row0 == 0),
                           final_w=final_norm_w.reshape(1, d) if is_last else None)
            if is_last:
                out = res
            else:
                xs = res
        x_parts = [xs.reshape(b * ts, d)]
    return out
```

```python
import functools
import math

import jax
import jax.numpy as jnp
from jax import lax
from jax.experimental import pallas as pl
from jax.experimental.pallas import tpu as pltpu

F32 = jnp.float32
BF16 = jnp.bfloat16

D_MODEL = 1024
DEPTH = 2
GRID_W = 64
ROPE_THETA = 10000.0
EPS = 1e-6
NEG_BIG = -1e30
MIX_W = D_MODEL // 2
A_HEAD_DIM = 64
A_HEADS = MIX_W // (2 * A_HEAD_DIM)
B_HEAD_DIM = 64
B_Q_HEADS = MIX_W // B_HEAD_DIM
B_KV_HEADS = 2
B_GROUP = B_Q_HEADS // B_KV_HEADS
C_HEAD_DIM = 128
C_HEADS = MIX_W // C_HEAD_DIM
C_CONV = 3
C_CHUNK = 128
N_BRANCH = 3
N_EXPERTS = 16
EXPERT_FF = 2 * D_MODEL
EC_CAPACITY_FACTOR = 2

LANES = 128
KV_B = B_KV_HEADS * B_HEAD_DIM
N_GATES = 4 * C_HEADS
MAIN_COLS = 8 * MIX_W + 2 * KV_B
GATE_COL0 = MAIN_COLS
MERGE_COL0 = MAIN_COLS + N_GATES
TM = 512
SEG = 256
TQ = 256
FF_TILE = 512
FFN_ROWS = 512
GATHER_EXPERTS = 4
SMALL_KEYS = 512
VMEM_LIMIT = 56 * 1024 * 1024
SCORE_SCALE = (A_HEAD_DIM ** -0.5) * math.log2(math.e)


def _cparams(n_axes, vmem=VMEM_LIMIT):
    return pltpu.CompilerParams(dimension_semantics=("arbitrary",) * n_axes, vmem_limit_bytes=vmem)


def _dot(a, b):
    return jnp.dot(a, b, preferred_element_type=F32)


def _dot_nt(a, b):
    return lax.dot_general(a, b, (((1,), (1,)), ((), ())), preferred_element_type=F32)


def _split3(x):
    a = x.astype(BF16)
    r = x - a.astype(F32)
    b = r.astype(BF16)
    c = (r - b.astype(F32)).astype(BF16)
    return a, b, c


def _sigmoid(x):
    return 1.0 / (1.0 + jnp.exp(-x))


def _silu(x):
    return x * _sigmoid(x)


def _log_sigmoid(x):
    return jnp.minimum(x, 0.0) - jnp.log(1.0 + jnp.exp(-jnp.abs(x)))


def _norm_mod(x, nw, shift, scale):
    ms = jnp.mean(x * x, axis=-1, keepdims=True)
    return (x * lax.rsqrt(ms + EPS) * nw) * (1.0 + scale) + shift


def _ada_kernel(c_ref, w_ref, b_ref, o_ref):
    s = _silu(c_ref[...])
    s1, s2, _ = _split3(s)
    w = w_ref[0]
    w1, w2, _ = _split3(w)
    o_ref[0] = _dot(s1, w1) + _dot(s1, w2) + _dot(s2, w1) + b_ref[0]


def _ada(cvec, w_ada, b_ada):
    depth, d, n = w_ada.shape
    tn = 1536
    return pl.pallas_call(
        _ada_kernel,
        grid=(depth, n // tn),
        in_specs=[pl.BlockSpec((16, d), lambda l, j: (0, 0)),
                  pl.BlockSpec((1, d, tn), lambda l, j: (l, 0, j)),
                  pl.BlockSpec((1, 1, tn), lambda l, j: (l, 0, j))],
        out_specs=pl.BlockSpec((1, 16, tn), lambda l, j: (l, 0, j)),
        out_shape=jax.ShapeDtypeStruct((depth, 16, n), F32),
        compiler_params=_cparams(2),
        name="ada_mod",
    )(cvec, w_ada, b_ada.reshape(depth, 1, n))


def _segment_rows(x_refs, k, first, count, segs_per_sample, n_ctx_segs):
    if len(x_refs) == 1:
        return [x_refs[0][i * SEG:(i + 1) * SEG, :] for i in range(count)]
    out = []
    for i in range(count):
        is_ctx = (first(k) + i) % segs_per_sample < n_ctx_segs
        out.append(jnp.where(is_ctx, x_refs[2 * i][...], x_refs[2 * i + 1][...]))
    return out


def _inproj_kernel(*refs, n_x, sps, ncs):
    x_refs = refs[:n_x]
    (mod0_ref, mod1_ref, nw_ref, w_ref, wg_ref, wgt_ref, gb_ref, gbt_ref,
     cos0_ref, cos1_ref, sin0_ref, sin1_ref, qnw_ref, knw_ref,
     qa_ref, ka_ref, va_ref, qb_ref, kb_ref, vb_ref, qkc_ref, vc_ref, oc_ref, g_ref, gt_ref) = refs[n_x:]
    d = D_MODEL
    tm = g_ref.shape[0]
    xs = _segment_rows(x_refs, pl.program_id(0), lambda k: 2 * k, 2, sps, ncs)
    hs = []
    for x, mod_ref in zip(xs, (mod0_ref, mod1_ref)):
        mod = mod_ref[0]
        hs.append(_norm_mod(x, nw_ref[...], mod[:, 0:d], mod[:, d:2 * d]).astype(BF16))
    h = jnp.concatenate(hs, axis=0)
    cos = jnp.concatenate([cos0_ref[...], cos1_ref[...]], axis=0)
    sin = jnp.concatenate([sin0_ref[...], sin1_ref[...]], axis=0)
    lane = lax.broadcasted_iota(jnp.int32, (tm, LANES), 1)

    def rope(p):
        return p * cos + pltpu.roll(p, LANES // 2, 1) * sin

    first_head = _first_head_lanes((tm, LANES))

    def head_inv_rms(p):
        sq = p * p
        sa = jnp.sum(jnp.where(first_head, sq, 0.0), axis=-1, keepdims=True)
        sb = jnp.sum(jnp.where(first_head, 0.0, sq), axis=-1, keepdims=True)
        return lax.rsqrt(jnp.where(first_head, sa, sb) * (1.0 / B_HEAD_DIM) + EPS)

    def proj(c0):
        return _dot(h, w_ref[:, c0:c0 + 2 * LANES])

    def halves(p):
        return p[:, :LANES], p[:, LANES:]

    pk, pv = halves(proj(2048))
    vb_ref[...] = pv.astype(BF16)
    chunks = [p for j in range(2) for p in halves(proj(1536 + j * 256))] + [pk]
    for j in range(4):
        qkc_ref[:, j * 256:(j + 1) * 256] = proj(2304 + j * 256)
    qnw = qnw_ref[...]
    for i, p in enumerate(chunks[:-1]):
        qb_ref[:, i * LANES:(i + 1) * LANES] = (rope(p * head_inv_rms(p) * qnw) * SCORE_SCALE).astype(BF16)
    kb_ref[...] = rope(pk * head_inv_rms(pk) * knw_ref[...]).astype(BF16)
    for j in range(2):
        for half, p in enumerate(halves(proj(j * 256))):
            c = j * 256 + half * LANES
            qa_ref[:, c:c + LANES] = (rope(p) * SCORE_SCALE).astype(BF16)
    for j in range(2):
        for half, p in enumerate(halves(proj(512 + j * 256))):
            c = j * 256 + half * LANES
            ka_ref[:, c:c + LANES] = rope(p).astype(BF16)
    for j in range(2):
        va_ref[:, j * 256:(j + 1) * 256] = proj(1024 + j * 256).astype(BF16)
    for j in range(2):
        vc_ref[:, j * 256:(j + 1) * 256] = proj(3328 + j * 256).astype(BF16)
    for j in range(2):
        oc_ref[:, j * 256:(j + 1) * 256] = _sigmoid(proj(3840 + j * 256)).astype(BF16)
    g = _dot(h, wg_ref[...]) + gb_ref[...]
    is_f = ((lane % 8) >= 4) & (lane < N_GATES)
    g_ref[...] = jnp.where(is_f, _log_sigmoid(g), g)
    gt = _dot_nt(wgt_ref[...], h) + gbt_ref[...]
    row = lax.broadcasted_iota(jnp.int32, (N_GATES, tm), 0)
    gt_ref[...] = jnp.where((row % 8) >= 4, _log_sigmoid(gt), gt)


def _seg_rows(k, half, segs_per_sample, n_ctx_segs):
    seg = 2 * k + half
    sample = seg // segs_per_sample
    within = seg % segs_per_sample
    return jnp.where(within < n_ctx_segs, 8, sample), within


def _mod_specs(width, segs_per_sample, n_ctx_segs):
    return [pl.BlockSpec((1, 1, width), lambda k, h=half: (_seg_rows(k, h, segs_per_sample, n_ctx_segs)[0], 0, 0))
            for half in range(2)]


def _segment_of_step(segs_per_sample, n_ctx_segs, with_ctx):
    if with_ctx:
        return lambda k: k
    n_lat = segs_per_sample - n_ctx_segs
    return lambda k: (k // n_lat) * segs_per_sample + n_ctx_segs + k % n_lat


def _mod_spec_one_seg(width, segs_per_sample, n_ctx_segs, seg_of):
    def index(k):
        seg = seg_of(k)
        return jnp.where(seg % segs_per_sample < n_ctx_segs, 8, seg // segs_per_sample), 0, 0
    return pl.BlockSpec((1, 1, width), index)


def _split_x_specs(seg_of, d, sps, ncs):
    nls = sps - ncs

    def ctx_index(k):
        seg = seg_of(k)
        return (seg // sps) * ncs + jnp.minimum(seg % sps, ncs - 1), 0

    def lat_index(k):
        seg = seg_of(k)
        return (seg // sps) * nls + jnp.clip(seg % sps - ncs, 0, nls - 1), 0

    return [pl.BlockSpec((SEG, d), ctx_index), pl.BlockSpec((SEG, d), lat_index)]


def _inproj(x_parts, mod, nw, w_main, wg, wgt, gb, gbt, cos, sin, qnw, knw, ts, lc):
    d = x_parts[0].shape[-1]
    rows = sum(p.shape[0] for p in x_parts)
    sps = ts // SEG
    ncs = lc // SEG
    tok = lambda width: pl.BlockSpec((TM, width), lambda k: (k, 0))
    full = lambda a: pl.BlockSpec(a.shape, lambda k: (0,) * a.ndim)
    table = [pl.BlockSpec((SEG, LANES), lambda k, h=half: (_seg_rows(k, h, sps, ncs)[1], 0)) for half in range(2)]
    outs = [(MIX_W, BF16), (MIX_W, BF16), (MIX_W, BF16), (MIX_W, BF16), (KV_B, BF16), (KV_B, BF16),
            (2 * MIX_W, F32), (MIX_W, BF16), (MIX_W, BF16), (LANES, F32)]
    out_shape = [jax.ShapeDtypeStruct((rows, w), dt) for w, dt in outs]
    out_specs = [tok(w) for w, _ in outs]
    out_shape.append(jax.ShapeDtypeStruct((N_GATES, rows), F32))
    out_specs.append(pl.BlockSpec((N_GATES, TM), lambda k: (0, k)))
    if len(x_parts) == 1:
        x_specs, x_args = [tok(d)], x_parts
    else:
        x_specs = [sp for half in range(2) for sp in _split_x_specs(lambda k, h=half: 2 * k + h, d, sps, ncs)]
        x_args = list(x_parts) * 2
    return pl.pallas_call(
        functools.partial(_inproj_kernel, n_x=len(x_args), sps=sps, ncs=ncs),
        grid=(rows // TM,),
        in_specs=x_specs + _mod_specs(mod.shape[-1], sps, ncs)
        + [full(nw), full(w_main), full(wg), full(wgt), full(gb), full(gbt)]
        + table + table + [full(qnw), full(knw)],
        out_specs=out_specs,
        out_shape=out_shape,
        compiler_params=_cparams(1),
        name="in_proj",
    )(*x_args, mod, mod, nw, w_main, wg, wgt, gb, gbt, cos, cos, sin, sin, qnw, knw)


def _first_head_lanes(shape):
    lane = lax.broadcasted_iota(jnp.int32, shape, 1)
    return (lane // (LANES // 4)) % 2 == 0


def _softmax_numerators(sb, eb, rows, lk):
    maxes = [jnp.max(sb[r:r + 8, :lk], axis=-1, keepdims=True) for r in range(0, rows, 8)]
    for rb in range(rows // 16):
        parts = [jnp.exp2(sb[rb * 16 + sub * 8:rb * 16 + sub * 8 + 8, :lk] - maxes[2 * rb + sub]) for sub in range(2)]
        eb[rb * 16:(rb + 1) * 16, :lk] = jnp.concatenate(parts, axis=0).astype(BF16)


def _with_ones_column(v):
    lane = lax.broadcasted_iota(jnp.int32, v.shape, 1)
    ones = jnp.where(lane == 0, 1.0, 0.0).astype(v.dtype)
    return jnp.concatenate([v, ones], axis=1)


def _attention_units(n_units, scores, finish, values, s_s, e_s, rows, lk):
    if lk <= SMALL_KEYS:
        for u in range(n_units):
            s = scores(u)
            e = jnp.exp2(s - jnp.max(s, axis=-1, keepdims=True))
            finish(u, _dot(e.astype(BF16), values(u)))
        return
    s_s[0, :, :lk] = scores(0)
    for u in range(n_units):
        if u + 1 < n_units:
            s_s[(u + 1) % 2, :, :lk] = scores(u + 1)
        _softmax_numerators(s_s.at[u % 2], e_s.at[u % 2], rows, lk)
        finish(u, _dot(e_s[u % 2, :, :lk], values(u)))


def _diff_attn_kernel(lam_ref, sub_ref, q_ref, k_ref, v_ref, o_ref, s_s, e_s, *, lam_init, n_ctx_blocks, with_ctx, lc):
    qi = pl.program_id(1)
    lv = lam_ref[...]
    lam = (jnp.exp(jnp.sum(lv[0:1] * lv[1:2], axis=-1, keepdims=True))
           - jnp.exp(jnp.sum(lv[2:3] * lv[3:4], axis=-1, keepdims=True)) + lam_init)
    tq = q_ref.shape[1]
    w = 2 * A_HEAD_DIM

    def body(lk):
        low = _first_head_lanes((tq, w))

        def scores(h):
            cols = slice(h * w, (h + 1) * w)
            q = q_ref[0, :, cols]
            zero = jnp.zeros_like(q)
            qs = jnp.concatenate([jnp.where(low, q, zero), jnp.where(low, zero, q)], axis=0)
            return _dot_nt(qs, k_ref[0, :lk, cols])

        def values(h):
            return _with_ones_column(v_ref[0, :lk, h * w:(h + 1) * w])

        def finish(h, pv):
            o = (pv[:tq, :w] * (1.0 / pv[:tq, w:w + 1])
                 - pv[tq:, :w] * (lam / pv[tq:, w:w + 1]))
            ms = jnp.mean(o * o, axis=-1, keepdims=True)
            o_ref[0, :, h * w:(h + 1) * w] = (o * lax.rsqrt(ms + EPS) * sub_ref[...] * (1.0 - lam_init)).astype(BF16)

        _attention_units(A_HEADS, scores, finish, values, s_s, e_s, 2 * tq, lk)

    @pl.when(qi < n_ctx_blocks)
    def _():
        if with_ctx:
            body(lc)
        else:
            o_ref[...] = jnp.zeros_like(o_ref)

    @pl.when(qi >= n_ctx_blocks)
    def _():
        body(k_ref.shape[1])


def _diff_attn(lam_vecs, sub_w, qa, ka, va, lam_init, lc, with_ctx):
    b, ts, _ = qa.shape
    n_ctx_blocks = lc // TQ
    nq = ts // TQ
    kern = functools.partial(_diff_attn_kernel, lam_init=lam_init, n_ctx_blocks=n_ctx_blocks, with_ctx=with_ctx, lc=lc)
    return pl.pallas_call(
        kern,
        grid=(b, nq),
        in_specs=[pl.BlockSpec(lam_vecs.shape, lambda i, j: (0, 0)),
                  pl.BlockSpec(sub_w.shape, lambda i, j: (0, 0)),
                  pl.BlockSpec((1, TQ, MIX_W), lambda i, j: (i, j, 0)),
                  pl.BlockSpec((1, ts, MIX_W), lambda i, j: (i, 0, 0)),
                  pl.BlockSpec((1, ts, MIX_W), lambda i, j: (i, 0, 0))],
        out_specs=pl.BlockSpec((1, TQ, MIX_W), lambda i, j: (i, j, 0)),
        out_shape=jax.ShapeDtypeStruct((b, ts, MIX_W), BF16),
        scratch_shapes=[pltpu.VMEM((2, 2 * TQ, ts), F32), pltpu.VMEM((2, 2 * TQ, ts), BF16)],
        compiler_params=_cparams(2),
        name="diff_attn",
    )(lam_vecs, sub_w, qa, ka, va)


def _gqa_kernel(q_ref, k_ref, v_ref, o_ref, s_s, e_s, *, n_ctx_blocks, with_ctx, lc):
    qi = pl.program_id(1)
    tq = q_ref.shape[1]

    def body(lk):
        low = _first_head_lanes((tq, LANES))
        out_low = lax.broadcasted_iota(jnp.int32, (tq, LANES), 1) < B_HEAD_DIM
        k = k_ref[0, :lk, :]
        v1 = _with_ones_column(v_ref[0, :lk, :])
        outs = {}
        pairs = B_GROUP // 2

        def scores(u):
            g, pair = divmod(u, pairs)
            parts = []
            for j in (2 * pair, 2 * pair + 1):
                x = q_ref[0, :, j * LANES:(j + 1) * LANES]
                zero = jnp.zeros_like(x)
                parts.append(jnp.where(low, x, zero) if g == 0 else jnp.where(low, zero, x))
            return _dot_nt(jnp.concatenate(parts, axis=0), k)

        def finish(u, pv):
            g, pair = divmod(u, pairs)
            o = pv[:, :LANES] * (1.0 / pv[:, LANES:LANES + 1])
            outs[(g, 2 * pair)] = o[:tq]
            outs[(g, 2 * pair + 1)] = o[tq:]

        _attention_units(B_KV_HEADS * pairs, scores, finish, lambda u: v1, s_s, e_s, 2 * tq, lk)
        for j in range(B_GROUP):
            o_ref[0, :, j * LANES:(j + 1) * LANES] = jnp.where(out_low, outs[(0, j)], outs[(1, j)]).astype(BF16)

    @pl.when(qi < n_ctx_blocks)
    def _():
        if with_ctx:
            body(lc)
        else:
            o_ref[...] = jnp.zeros_like(o_ref)

    @pl.when(qi >= n_ctx_blocks)
    def _():
        body(k_ref.shape[1])


def _gqa_attn(qb, kb, vb, lc, with_ctx):
    b, ts, _ = qb.shape
    n_ctx_blocks = lc // TQ
    nq = ts // TQ
    kern = functools.partial(_gqa_kernel, n_ctx_blocks=n_ctx_blocks, with_ctx=with_ctx, lc=lc)
    return pl.pallas_call(
        kern,
        grid=(b, nq),
        in_specs=[pl.BlockSpec((1, TQ, MIX_W), lambda i, j: (i, j, 0)),
                  pl.BlockSpec((1, ts, KV_B), lambda i, j: (i, 0, 0)),
                  pl.BlockSpec((1, ts, KV_B), lambda i, j: (i, 0, 0))],
        out_specs=pl.BlockSpec((1, TQ, MIX_W), lambda i, j: (i, j, 0)),
        out_shape=jax.ShapeDtypeStruct((b, ts, MIX_W), BF16),
        scratch_shapes=[pltpu.VMEM((2, 2 * TQ, ts), F32), pltpu.VMEM((2, 2 * TQ, ts), BF16)],
        compiler_params=_cparams(2),
        name="gqa_attn",
    )(qb, kb, vb)


def _mlstm_kernel(q_ref, k_ref, v_ref, o_ref, g_ref, gt_ref, cwq_ref, cwk_ref, nw_ref, out_ref,
                  q_s, kt_s, bc_s, ac_s, rows_s, hacc_s, st_s, *, lc):
    ts = q_ref.shape[1]
    hp = q_ref.shape[2] // C_HEAD_DIM
    head0 = pl.program_id(1) * hp
    nc = ts // C_CHUNK
    ncc = lc // C_CHUNK
    ch = C_CHUNK

    row = lax.broadcasted_iota(jnp.int32, (ts, LANES), 0)
    prev_ok = (row != 0) & (row != lc)
    next_ok = (row != lc - 1) & (row != ts - 1)

    def conv(x, w):
        xp = jnp.where(prev_ok, pltpu.roll(x, 1, 0), 0.0)
        xn = jnp.where(next_ok, pltpu.roll(x, ts - 1, 0), 0.0)
        return _silu(xp * w[0:1] + x * w[1:2] + xn * w[2:3])

    for j in range(hp):
        cols = slice(j * LANES, (j + 1) * LANES)
        q_s[:, cols] = conv(q_ref[0, :, cols], cwq_ref[0, :, cols]).astype(BF16)
        y = conv(k_ref[0, :, cols], cwk_ref[0, :, cols]) * (C_HEAD_DIM ** -0.5)
        for c in range(nc):
            kt_s[c, cols, :] = y[c * ch:(c + 1) * ch, :].T.astype(BF16)

    ri = lax.broadcasted_iota(jnp.int32, (ch, ch), 0)
    ci = lax.broadcasted_iota(jnp.int32, (ch, ch), 1)
    lower = jnp.where(ci <= ri, 1.0, 0.0).astype(BF16)
    upper = jnp.where(ci >= ri, 1.0, 0.0).astype(BF16)
    lane = ci
    rowi = lax.broadcasted_iota(jnp.int32, (N_GATES, ch), 0)
    for c in range(nc):
        rs = slice(c * ch, (c + 1) * ch)
        g = g_ref[0, rs, :]
        g1, g2, g3 = _split3(g)
        pre = _dot(lower, g1) + _dot(lower, g2) + _dot(lower, g3)
        suf = _dot(upper, g1) + _dot(upper, g2) + _dot(upper, g3)
        gt = gt_ref[:, rs]
        t1, t2, t3 = _split3(gt)
        pre_t = _dot(t1, upper) + _dot(t2, upper) + _dot(t3, upper)
        suf_t = _dot(t1, lower) + _dot(t2, lower) + _dot(t3, lower)
        for j in range(hp):
            for direction in range(2):
                idx = j * 2 + direction
                li = head0 + j + 8 * direction
                lf = li + 4
                cum, cum_t = (pre, pre_t) if direction == 0 else (suf, suf_t)
                b_col = jnp.sum(jnp.where(lane == lf, cum, 0.0), axis=-1, keepdims=True)
                i_col = jnp.sum(jnp.where(lane == li, g, 0.0), axis=-1, keepdims=True)
                bc_s[idx, rs, :] = jnp.broadcast_to(b_col, (ch, LANES))
                ac_s[idx, rs, :] = jnp.broadcast_to(i_col - b_col, (ch, LANES))
                b_row = jnp.sum(jnp.where(rowi == lf, cum_t, 0.0), axis=0, keepdims=True)
                i_row = jnp.sum(jnp.where(rowi == li, gt, 0.0), axis=0, keepdims=True)
                rows_s[c, 2 * idx:2 * idx + 1, :] = b_row
                rows_s[c, 2 * idx + 1:2 * idx + 2, :] = i_row - b_row

    hacc_s[...] = jnp.zeros_like(hacc_s)
    st_s[...] = jnp.zeros_like(st_s)
    tri_f = ci <= ri
    tri_b = ci >= ri

    def chain(c, j, direction, m):
        idx = j * 2 + direction
        c0 = pl.multiple_of(c * ch, ch)
        cols = slice(j * LANES, (j + 1) * LANES)
        q = q_s[pl.ds(c0, ch), cols]
        kt = kt_s[c, cols, :]
        v = v_ref[0, pl.ds(c0, ch), cols]
        bc = bc_s[idx, pl.ds(c0, ch), :]
        ac = ac_s[idx, pl.ds(c0, ch), :]
        rows = rows_s[c]
        b_row = rows[2 * idx:2 * idx + 1, :]
        ib_row = rows[2 * idx + 1:2 * idx + 2, :]
        tri = tri_f if direction == 0 else tri_b
        log_d = jnp.where(tri, bc + ib_row, NEG_BIG)
        m_intra = jnp.max(log_d, axis=-1, keepdims=True)
        log_inter = bc + m
        m_t = jnp.maximum(log_inter, m_intra)
        dm = jnp.exp(log_d - m_t)
        w_inter = jnp.exp(log_inter - m_t)
        s = _dot(q, kt) * dm
        st = st_s[idx]
        inter = _dot(q, st.astype(BF16))
        num = _dot(s.astype(BF16), v) + w_inter * inter[:, :LANES]
        den = jnp.sum(s, axis=-1, keepdims=True) + w_inter * inter[:, LANES:LANES + 1]
        hout = num / jnp.maximum(jnp.abs(den), jnp.exp(-m_t))
        hacc_s[pl.ds(c0, ch), cols] = hacc_s[pl.ds(c0, ch), cols] + hout
        total = b_row[:, ch - 1:ch] if direction == 0 else b_row[:, 0:1]
        m_new = jnp.maximum(total + m, jnp.max(total + ib_row, axis=-1, keepdims=True))
        w = jnp.exp(total + ac - m_new)
        decay = jnp.exp(total + m - m_new)
        wv = jnp.concatenate([w * v.astype(F32), jnp.where(lane == 0, w, 0.0)], axis=1).astype(BF16)
        st_s[idx] = decay * st + _dot(kt, wv)
        return m_new

    def step(i, ms):
        c_f = i
        c_b = jnp.where(i < ncc, ncc - 1 - i, nc + ncc - 1 - i)
        out = []
        for j in range(hp):
            out.append(chain(c_f, j, 0, ms[j * 2]))
            out.append(chain(c_b, j, 1, ms[j * 2 + 1]))
        return tuple(out)

    lax.fori_loop(0, nc, step, tuple(jnp.zeros((1, 1), F32) for _ in range(2 * hp)), unroll=2)

    for j in range(hp):
        cols = slice(j * LANES, (j + 1) * LANES)
        x = hacc_s[:, cols]
        ms = jnp.mean(x * x, axis=-1, keepdims=True)
        y = x * lax.rsqrt(ms + EPS) * nw_ref[0, :, cols]
        out_ref[0, :, cols] = (o_ref[0, :, cols].astype(F32) * y).astype(BF16)


def _mlstm(qkc, vc, oc, g, gt, conv_w, norm_w, lc, heads_per_step=2):
    b, ts, _ = vc.shape
    wq = heads_per_step * C_HEAD_DIM
    nhp = C_HEADS // heads_per_step
    nc = ts // C_CHUNK
    cw = conv_w.reshape(C_CONV, 2 * nhp, wq).transpose(1, 0, 2)
    nw = norm_w.reshape(1, nhp, wq).transpose(1, 0, 2)
    kern = functools.partial(_mlstm_kernel, lc=lc)
    tokw = lambda off: pl.BlockSpec((1, ts, wq), lambda i, p: (i, 0, p + off))
    return pl.pallas_call(
        kern,
        grid=(b, nhp),
        in_specs=[tokw(0), tokw(nhp), tokw(0), tokw(0),
                  pl.BlockSpec((1, ts, LANES), lambda i, p: (i, 0, 0)),
                  pl.BlockSpec((N_GATES, ts), lambda i, p: (0, i)),
                  pl.BlockSpec((1, C_CONV, wq), lambda i, p: (p, 0, 0)),
                  pl.BlockSpec((1, C_CONV, wq), lambda i, p: (p + nhp, 0, 0)),
                  pl.BlockSpec((1, 1, wq), lambda i, p: (p, 0, 0))],
        out_specs=tokw(0),
        out_shape=jax.ShapeDtypeStruct((b, ts, MIX_W), BF16),
        scratch_shapes=[pltpu.VMEM((ts, wq), BF16),
                        pltpu.VMEM((nc, wq, C_CHUNK), BF16),
                        pltpu.VMEM((2 * heads_per_step, ts, LANES), F32),
                        pltpu.VMEM((2 * heads_per_step, ts, LANES), F32),
                        pltpu.VMEM((nc, 4 * heads_per_step, C_CHUNK), F32),
                        pltpu.VMEM((ts, wq), F32),
                        pltpu.VMEM((2 * heads_per_step, C_HEAD_DIM, 2 * LANES), F32)],
        compiler_params=_cparams(2),
        name="mlstm",
    )(qkc, qkc, vc, oc, g, gt, cw, cw, nw)


def _merge_kernel(*refs, n_x, sps, ncs):
    x_refs = refs[:n_x]
    (mod_ref, n1_ref, n2_ref, oa_ref, ob_ref, oc_ref, wm_ref, wa_ref, wb_ref, wc_ref,
     wo_ref, wr_ref, xo_ref, h2_ref, aff_ref) = refs[n_x:]
    d = D_MODEL
    mod = mod_ref[0]
    (x,) = _segment_rows(x_refs, pl.program_id(0), lambda k: k, 1, sps, ncs)
    h = _norm_mod(x, n1_ref[...], mod[:, 0:d], mod[:, d:2 * d]).astype(BF16)
    merged = (_sigmoid(_dot(h, wm_ref[:, 0:d])) * _dot(oa_ref[...], wa_ref[...])
              + _sigmoid(_dot(h, wm_ref[:, d:2 * d])) * _dot(ob_ref[...], wb_ref[...])
              + _sigmoid(_dot(h, wm_ref[:, 2 * d:3 * d])) * _dot(oc_ref[...], wc_ref[...]))
    y = _dot(merged.astype(BF16), wo_ref[...])
    x1 = x + mod[:, 2 * d:3 * d] * y
    xo_ref[...] = x1
    h2 = _norm_mod(x1, n2_ref[...], mod[:, 3 * d:4 * d], mod[:, 4 * d:5 * d])
    h2b = h2.astype(BF16)
    h2_ref[...] = h2b
    h2l = (h2 - h2b.astype(F32)).astype(BF16)
    wr = wr_ref[...]
    wrh = wr.astype(BF16)
    wrl = (wr - wrh.astype(F32)).astype(BF16)
    logits = _dot(h2b, wrh) + _dot(h2b, wrl) + _dot(h2l, wrh)
    lane = lax.broadcasted_iota(jnp.int32, logits.shape, 1)
    valid = lane < N_EXPERTS
    logits = jnp.where(valid, logits, NEG_BIG)
    e = jnp.where(valid, jnp.exp(logits - jnp.max(logits, axis=-1, keepdims=True)), 0.0)
    aff_ref[...] = e / jnp.sum(e, axis=-1, keepdims=True)


def _merge(x_parts, mod, n1, n2, oa, ob, oc, wm, wa, wb, wc, wo, wr, ts, lc, with_ctx):
    d = x_parts[0].shape[-1]
    rows = sum(p.shape[0] for p in x_parts)
    sps, ncs = ts // SEG, lc // SEG
    seg_of = _segment_of_step(sps, ncs, with_ctx)
    n_steps = rows // SEG if with_ctx else (rows // ts) * (sps - ncs)
    tok = lambda width: pl.BlockSpec((SEG, width), lambda k: (seg_of(k), 0))
    full = lambda a: pl.BlockSpec(a.shape, lambda k: (0,) * a.ndim)
    split = len(x_parts) > 1
    assert not split or with_ctx
    x_specs = _split_x_specs(seg_of, d, sps, ncs) if split else [tok(d)]
    return pl.pallas_call(
        functools.partial(_merge_kernel, n_x=len(x_parts), sps=sps, ncs=ncs),
        grid=(n_steps,),
        in_specs=x_specs + [_mod_spec_one_seg(mod.shape[-1], sps, ncs, seg_of),
                  full(n1), full(n2), tok(MIX_W), tok(MIX_W), tok(MIX_W),
                  full(wm), full(wa), full(wb), full(wc), full(wo), full(wr)],
        out_specs=[tok(d), tok(d), tok(LANES)],
        out_shape=[jax.ShapeDtypeStruct((rows, d), F32),
                   jax.ShapeDtypeStruct((rows, d), BF16),
                   jax.ShapeDtypeStruct((rows, LANES), F32)],
        input_output_aliases={} if split else {0: 0},
        compiler_params=_cparams(1),
        name="merge_out",
    )(*x_parts, mod, n1, n2, oa, ob, oc, wm, wa, wb, wc, wo, wr)


def _route_kernel(aff_ref, post_ref, posr_ref, affr_ref, *, cap, row0):
    n = post_ref.shape[0]
    ch = LANES
    aff = aff_ref[0, row0:row0 + n, :]
    for bi in range(1, aff_ref.shape[0]):
        aff = aff + pltpu.roll(aff_ref[bi, row0:row0 + n, :], bi * N_EXPERTS, 1)

    def step(i, thr_bits):
        cand = thr_bits | jnp.left_shift(jnp.int32(1), 30 - i)
        cnt = jnp.sum((aff >= pltpu.bitcast(cand, F32)).astype(jnp.int32), axis=0, keepdims=True)
        return jnp.where(cnt >= cap, cand, thr_bits)

    thr = pltpu.bitcast(lax.fori_loop(0, 31, step, jnp.zeros((1, LANES), jnp.int32)), F32)
    gt = aff > thr
    eq = aff == thr
    need = cap - jnp.sum(gt.astype(jnp.int32), axis=0, keepdims=True)

    ri = lax.broadcasted_iota(jnp.int32, (ch, ch), 0)
    ci = lax.broadcasted_iota(jnp.int32, (ch, ch), 1)
    strict_lower = jnp.where(ci < ri, 1.0, 0.0).astype(BF16)

    def excl_cumsum(mask_f):
        carry = jnp.zeros((1, LANES), F32)
        blocks = []
        for c in range(n // ch):
            blk = mask_f[c * ch:(c + 1) * ch, :]
            blocks.append(_dot(strict_lower, blk.astype(BF16)) + carry)
            carry = carry + jnp.sum(blk, axis=0, keepdims=True)
        return jnp.concatenate(blocks, axis=0)

    eq_rank = excl_cumsum(jnp.where(eq, 1.0, 0.0))
    sel = gt | (eq & (eq_rank < need.astype(F32)))
    pos = excl_cumsum(jnp.where(sel, 1.0, 0.0))
    post = jnp.where(sel, pos, -1.0)
    post_ref[...] = post
    for c in range(n // ch):
        posr_ref[:, c * ch:(c + 1) * ch] = post[c * ch:(c + 1) * ch, :].T
        affr_ref[:, c * ch:(c + 1) * ch] = aff[c * ch:(c + 1) * ch, :].T


def _route(aff, cap, row0, n):
    return pl.pallas_call(
        functools.partial(_route_kernel, cap=cap, row0=row0),
        out_shape=[jax.ShapeDtypeStruct((n, LANES), F32),
                   jax.ShapeDtypeStruct((LANES, n), F32),
                   jax.ShapeDtypeStruct((LANES, n), F32)],
        compiler_params=pltpu.CompilerParams(vmem_limit_bytes=VMEM_LIMIT),
        name="route",
    )(aff)


def _gather_kernel(posr_ref, affr_ref, h_ref, xe_ref, gs_ref, *, cap, row0, n):
    ne = posr_ref.shape[0]
    slot = lax.broadcasted_iota(jnp.int32, (cap, n), 0).astype(F32)
    onehots = []
    for t in range(ne):
        pf = jnp.where(posr_ref[t] == slot, 1.0, 0.0)
        gs = jnp.sum(pf * affr_ref[t], axis=-1, keepdims=True)
        gs_ref[t] = jnp.broadcast_to(gs, (cap, LANES))
        onehots.append(pf.astype(BF16))
    xe = _dot(jnp.concatenate(onehots, axis=0), h_ref[0, row0:row0 + n, :])
    for t in range(ne):
        xe_ref[t] = xe[t * cap:(t + 1) * cap].astype(BF16)


def _gather(posr, affr, h2, cap, row0, n, b):
    ts, d = h2.shape[1:]
    be = posr.shape[0]
    posr3 = posr.reshape(be, 1, n)
    affr3 = affr.reshape(be, 1, n)
    e = N_EXPERTS
    ge = GATHER_EXPERTS
    return pl.pallas_call(
        functools.partial(_gather_kernel, cap=cap, row0=row0, n=n),
        grid=(b, e // ge),
        in_specs=[pl.BlockSpec((ge, 1, n), lambda i, j: (i * (e // ge) + j, 0, 0)),
                  pl.BlockSpec((ge, 1, n), lambda i, j: (i * (e // ge) + j, 0, 0)),
                  pl.BlockSpec((1, ts, d), lambda i, j: (i, 0, 0))],
        out_specs=[pl.BlockSpec((ge, cap, d), lambda i, j: (j, i, 0)),
                   pl.BlockSpec((ge, cap, LANES), lambda i, j: (j, i, 0))],
        out_shape=[jax.ShapeDtypeStruct((e, b * cap, d), BF16),
                   jax.ShapeDtypeStruct((e, b * cap, LANES), F32)],
        compiler_params=_cparams(2),
        name="expert_gather",
    )(posr3, affr3, h2)


def _ffn_kernel(*refs, n_groups):
    xe_refs = refs[0:n_groups]
    gs_refs = refs[n_groups:2 * n_groups]
    wg_ref, wu_ref, wd_ref = refs[2 * n_groups:2 * n_groups + 3]
    out_refs = refs[2 * n_groups + 3:3 * n_groups + 3]
    acc_refs = refs[3 * n_groups + 3:4 * n_groups + 3]
    f = pl.program_id(1)
    @pl.when(f == 0)
    def _():
        for acc_ref in acc_refs:
            acc_ref[...] = jnp.zeros_like(acc_ref)

    wgb = wg_ref[0, 0].astype(BF16)
    wub = wu_ref[0, 0].astype(BF16)
    wdb = wd_ref[0, 0].astype(BF16)

    total = sum(r.shape[1] for r in xe_refs)
    n_blocks = max(1, total // FFN_ROWS)
    assert total % (16 * n_blocks) == 0
    rb = total // n_blocks
    for blk in range(n_blocks):
        pieces, start = [], 0
        for gi, r in enumerate(xe_refs):
            lo, hi = max(blk * rb, start), min((blk + 1) * rb, start + r.shape[1])
            if lo < hi:
                pieces.append((gi, lo - start, hi - lo))
            start += r.shape[1]
        xb = jnp.concatenate([xe_refs[gi][0, r0:r0 + n, :] for gi, r0, n in pieces], axis=0)
        a = _dot(xb, wgb)
        u = _dot(xb, wub)
        y = _dot((_silu(a) * u).astype(BF16), wdb)
        off = 0
        for gi, r0, n in pieces:
            new = acc_refs[gi][r0:r0 + n, :] + y[off:off + n]
            acc_refs[gi][r0:r0 + n, :] = new
            out_refs[gi][0, r0:r0 + n, :] = (new * gs_refs[gi][0, r0:r0 + n, 0:1]).astype(BF16)
            off += n


def _ffn(xes, gss, w_gate, w_up, w_down, layer):
    n_groups = len(xes)
    _, e, d, ff = w_gate.shape
    nf = ff // FF_TILE
    in_specs = [pl.BlockSpec((1,) + x.shape[1:], lambda i, f: (i, 0, 0)) for x in xes]
    in_specs += [pl.BlockSpec((1,) + g.shape[1:], lambda i, f: (i, 0, 0)) for g in gss]
    in_specs += [pl.BlockSpec((1, 1, d, FF_TILE), lambda i, f: (layer, i, 0, f)),
                 pl.BlockSpec((1, 1, d, FF_TILE), lambda i, f: (layer, i, 0, f)),
                 pl.BlockSpec((1, 1, FF_TILE, d), lambda i, f: (layer, i, f, 0))]
    out = pl.pallas_call(
        functools.partial(_ffn_kernel, n_groups=n_groups),
        grid=(e, nf),
        in_specs=in_specs,
        out_specs=[pl.BlockSpec((1,) + x.shape[1:], lambda i, f: (i, 0, 0)) for x in xes],
        out_shape=[jax.ShapeDtypeStruct(x.shape, BF16) for x in xes],
        scratch_shapes=[pltpu.VMEM(x.shape[1:], F32) for x in xes],
        compiler_params=_cparams(2),
        name="expert_ffn",
    )(*xes, *gss, w_gate, w_up, w_down)
    return list(out)


def _combine_kernel(*refs, cap, final):
    if final:
        post_ref, yg_ref, x_ref, mod_ref, fw_ref, o_ref = refs
    else:
        post_ref, yg_ref, x_ref, mod_ref, o_ref = refs
    d = D_MODEL
    sample = pl.program_id(0)
    post = pltpu.roll(post_ref[...], (LANES - sample * N_EXPERTS) % LANES, 1)
    tn = post.shape[0]
    slot = lax.broadcasted_iota(jnp.int32, (tn, cap), 1).astype(F32)
    acc = jnp.zeros((tn, d), F32)
    for e in range(N_EXPERTS):
        onehot = jnp.where(post[:, e:e + 1] == slot, 1.0, 0.0).astype(BF16)
        acc = acc + _dot(onehot, yg_ref[e])
    x2 = x_ref[0] + mod_ref[0][:, 5 * d:6 * d] * acc
    if final:
        ms = jnp.mean(x2 * x2, axis=-1, keepdims=True)
        o_ref[0] = x2 * lax.rsqrt(ms + EPS) * fw_ref[...]
    else:
        o_ref[0] = x2


def _combine(post, yg, xs, mod, cap, row0, n, mod_row_ctx, final_w=None):
    b, ts, d = xs.shape
    tn = min(SEG, n)
    blk0 = row0 // tn
    final = final_w is not None
    in_specs = [pl.BlockSpec((tn, LANES), lambda i, j: (j, 0)),
                pl.BlockSpec((N_EXPERTS, cap, d), lambda i, j: (0, i, 0)),
                pl.BlockSpec((1, tn, d), lambda i, j: (i, j + blk0, 0)),
                pl.BlockSpec((1, 1, mod.shape[-1]), lambda i, j: (8 if mod_row_ctx else i, 0, 0))]
    args = [post, yg, xs, mod]
    if final:
        in_specs.append(pl.BlockSpec(final_w.shape, lambda i, j: (0, 0)))
        args.append(final_w)
        out_spec = pl.BlockSpec((1, tn, d), lambda i, j: (i, j, 0))
        out_shape = jax.ShapeDtypeStruct((b, n, d), F32)
        aliases = {}
    else:
        out_spec = pl.BlockSpec((1, tn, d), lambda i, j: (i, j + blk0, 0))
        out_shape = jax.ShapeDtypeStruct((b, ts, d), F32)
        aliases = {2: 0}
    return pl.pallas_call(
        functools.partial(_combine_kernel, cap=cap, final=final),
        grid=(b, n // tn),
        in_specs=in_specs,
        out_specs=out_spec,
        out_shape=out_shape,
        input_output_aliases=aliases,
        compiler_params=_cparams(2),
        name="expert_combine_final" if final else "expert_combine",
    )(*args)


def _rope_tables(n_lat, lc):
    n_rows = n_lat // GRID_W
    rows = jnp.repeat(jnp.arange(n_rows, dtype=F32), GRID_W)
    cols = jnp.tile(jnp.arange(GRID_W, dtype=F32), n_rows)
    n_freq = A_HEAD_DIM // 4
    inv_freq = ROPE_THETA ** (-jnp.arange(n_freq, dtype=F32) / n_freq)
    ang = jnp.concatenate([rows[:, None] * inv_freq, cols[:, None] * inv_freq], axis=-1)
    c, s = jnp.cos(ang), jnp.sin(ang)
    cos = jnp.concatenate([c, c, c, c], axis=-1)
    sin = jnp.concatenate([-s, -s, s, s], axis=-1)
    cos = jnp.concatenate([jnp.ones((lc, LANES), F32), cos], axis=0)
    sin = jnp.concatenate([jnp.zeros((lc, LANES), F32), sin], axis=0)
    return cos, sin


def _pair_rope_layout(w):
    d, n = w.shape
    q = LANES // 4
    return w.reshape(d, n // LANES, 2, 2, q).transpose(0, 1, 3, 2, 4).reshape(d, n)


def kernel(x, c, ctx, c_ctx, w_ada, b_ada, norm1_w, norm2_w, w_in, mlstm_conv_w, mlstm_gate_b, mlstm_norm_w,
           diff_lambda, diff_subln_w, gqa_qnorm_w, gqa_knorm_w, w_branch_a, w_branch_b, w_branch_c, w_out,
           w_router, w_exp_gate, w_exp_up, w_exp_down, final_norm_w):
    b, n_lat, d = x.shape
    lc = ctx.shape[1]
    depth = w_ada.shape[0]
    assert d == D_MODEL and b * N_EXPERTS <= LANES and b <= 8
    ts = lc + n_lat
    assert lc % SEG == 0 and n_lat % SEG == 0 and lc % TQ == 0 and (b * ts) % TM == 0

    x_parts = [ctx.reshape(b * lc, d), x.reshape(b * n_lat, d)]
    cvec = jnp.zeros((16, d), F32).at[:b].set(c).at[8].set(c_ctx)
    mods = _ada(cvec, w_ada, b_ada)
    cos, sin = _rope_tables(n_lat, lc)
    cap_lat = EC_CAPACITY_FACTOR * n_lat // N_EXPERTS
    cap_ctx = EC_CAPACITY_FACTOR * lc // N_EXPERTS
    out = None

    for layer in range(depth):
        with_ctx = layer < depth - 1
        mod = mods[layer].reshape(16, 1, 6 * d)
        wl = w_in[layer]
        bq0 = 3 * MIX_W
        w_bq = wl[:, bq0:bq0 + MIX_W].reshape(d, B_KV_HEADS, B_GROUP, B_HEAD_DIM).transpose(0, 2, 1, 3).reshape(d, MIX_W)
        kb0 = bq0 + MIX_W
        w_main = jnp.concatenate([_pair_rope_layout(wl[:, :2 * MIX_W]), wl[:, 2 * MIX_W:bq0], _pair_rope_layout(w_bq),
                                  _pair_rope_layout(wl[:, kb0:kb0 + KV_B]), wl[:, kb0 + KV_B:MAIN_COLS]], axis=1).astype(BF16)
        w_bb = w_branch_b[layer].reshape(B_KV_HEADS, B_GROUP, B_HEAD_DIM, d).transpose(1, 0, 2, 3).reshape(MIX_W, d)
        w_gates = wl[:, GATE_COL0:GATE_COL0 + N_GATES]
        wg = jnp.pad(w_gates, ((0, 0), (0, LANES - N_GATES))).astype(BF16)
        wgt = w_gates.T.astype(BF16)
        gb = jnp.pad(mlstm_gate_b[layer], (0, LANES - N_GATES)).reshape(1, LANES)
        gbt = mlstm_gate_b[layer].reshape(N_GATES, 1)
        n1 = norm1_w[layer].reshape(1, d)
        n2 = norm2_w[layer].reshape(1, d)
        qnw = _pair_rope_layout(jnp.tile(gqa_qnorm_w[layer], LANES // B_HEAD_DIM).reshape(1, LANES))
        knw = _pair_rope_layout(jnp.tile(gqa_knorm_w[layer], LANES // B_HEAD_DIM).reshape(1, LANES))

        flat = _inproj(x_parts, mod, n1, w_main, wg, wgt, gb, gbt, cos, sin, qnw, knw, ts, lc)
        qa, ka, va, qb, kb, vb, qkc, vc, oc, g = [a.reshape(b, ts, a.shape[-1]) for a in flat[:-1]]
        gt = flat[-1]

        lam_init = 0.8 - 0.6 * math.exp(-0.3 * layer)
        oa = _diff_attn(diff_lambda[layer], diff_subln_w[layer].reshape(1, 2 * A_HEAD_DIM),
                        qa, ka, va, lam_init, lc, with_ctx)
        ob = _gqa_attn(qb, kb, vb, lc, with_ctx)
        ocm = _mlstm(qkc, vc, oc, g, gt, mlstm_conv_w[layer], mlstm_norm_w[layer], lc)

        wm = wl[:, MERGE_COL0:].astype(BF16)
        wr = jnp.pad(w_router[layer], ((0, 0), (0, LANES - N_EXPERTS)))
        rows2 = lambda a: a.reshape(b * ts, a.shape[-1])
        xs, h2, aff = _merge(x_parts, mod, n1, n2, rows2(oa), rows2(ob), rows2(ocm), wm,
                             w_branch_a[layer].astype(BF16), w_bb.astype(BF16),
                             w_branch_c[layer].astype(BF16), w_out[layer].astype(BF16), wr, ts, lc, with_ctx)
        xs, h2, aff = [a.reshape(b, ts, a.shape[-1]) for a in (xs, h2, aff)]

        groups = [(lc, n_lat, cap_lat)]
        if with_ctx:
            groups.append((0, lc, cap_ctx))
        xes, gss, posts = [], [], []
        for row0, n, cap in groups:
            post, posr, affr = _route(aff, cap, row0, n)
            xe, gs = _gather(posr, affr, h2, cap, row0, n, b)
            xes.append(xe)
            gss.append(gs)
            posts.append(post)
        ygs = _ffn(xes, gss, w_exp_gate, w_exp_up, w_exp_down, layer)
        for gi, (row0, n, cap) in enumerate(groups):
            is_last = (layer == depth - 1) and gi == 0
            res = _combine(posts[gi], ygs[gi], xs, mod, cap, row0, n, mod_row_ctx=(row0 == 0),
                           final_w=final_norm_w.reshape(1, d) if is_last else None)
            if is_last:
                out = res
            else:
                xs = res
        x_parts = [xs.reshape(b * ts, d)]
    return out
```
